```python
import math
import numpy as np
import jax
import jax.numpy as jnp
from jax import lax


D_MODEL = 2048
BATCH = 2
SEQ = 4096
DEPTH = 1

D_MIX = D_MODEL
SSD_WIDTH = D_MIX // 2
ATT_WIDTH = D_MIX - SSD_WIDTH

SSD_HEAD_DIM = 64
SSD_HEADS = SSD_WIDTH // SSD_HEAD_DIM
SSD_GROUPS = 2
SSD_STATE = 128
SSD_CHUNK = 128
CONV_WIDTH = 4
CONV_CH = SSD_WIDTH + 2 * SSD_GROUPS * SSD_STATE

ATT_HEAD_DIM = 64
ATT_HEADS = ATT_WIDTH // ATT_HEAD_DIM
ATT_KV_HEADS = 4
ATT_GROUP = ATT_HEADS // ATT_KV_HEADS
KV_WIDTH = ATT_KV_HEADS * ATT_HEAD_DIM
CMP_BLOCK = 32
CMP_STRIDE = 16
CMP_HIDDEN = 256
SEL_BLOCK = 64
N_SELECT = 16
WINDOW = 512
Q_BLOCK = 128
N_BRANCH = 3

ROPE_THETA = 500000.0
ROPE_DIM = ATT_HEAD_DIM // 4

D_FF = -((-8 * D_MODEL) // (3 * 256)) * 256

NORM_EPS = 1e-6
NEG_INF = -1e30
FORCE_SCORE = 1e4

IN_WIDTHS = (SSD_WIDTH, CONV_CH, SSD_HEADS, ATT_WIDTH, KV_WIDTH, KV_WIDTH, KV_WIDTH, KV_WIDTH, KV_WIDTH, KV_WIDTH, ATT_HEADS * N_BRANCH)
D_IN = sum(IN_WIDTHS)

kernel_name = 'hymba_ssd_nsa_hybrid_block'


def rmsnorm(x, w):
    xf = x.astype(jnp.float32)
    xf = xf * lax.rsqrt(jnp.mean(xf * xf, axis=-1, keepdims=True) + NORM_EPS)
    return xf.astype(x.dtype) * w


def rope_tables(seq, dtype):
    inv = 1.0 / (ROPE_THETA ** (jnp.arange(0, ROPE_DIM, 2, dtype=jnp.float32) / ROPE_DIM))
    ang = jnp.arange(seq, dtype=jnp.float32)[:, None] * inv[None, :]
    return jnp.cos(ang).astype(dtype), jnp.sin(ang).astype(dtype)


def partial_rope(x, cos, sin):
    half = ROPE_DIM // 2
    c = cos[None, :, None, :]
    s = sin[None, :, None, :]
    x1 = x[..., :half]
    x2 = x[..., half:ROPE_DIM]
    return jnp.concatenate([x1 * c - x2 * s, x2 * c + x1 * s, x[..., ROPE_DIM:]], axis=-1)


def causal_depthwise_conv(u, w, b):
    y = lax.conv_general_dilated(u, w[:, None, :].astype(u.dtype), window_strides=(1,), padding=((CONV_WIDTH - 1, 0),), dimension_numbers=('NWC', 'WIO', 'NWC'), feature_group_count=u.shape[-1])
    return y + b


def ssd_chunked(xdt, adt, bm, cm):
    bsz, seq = xdt.shape[:2]
    nc = seq // SSD_CHUNK
    hg = SSD_HEADS // SSD_GROUPS
    x = xdt.reshape(bsz, nc, SSD_CHUNK, SSD_GROUPS, hg, SSD_HEAD_DIM)
    a = adt.reshape(bsz, nc, SSD_CHUNK, SSD_GROUPS, hg)
    b = bm.reshape(bsz, nc, SSD_CHUNK, SSD_GROUPS, SSD_STATE)
    c = cm.reshape(bsz, nc, SSD_CHUNK, SSD_GROUPS, SSD_STATE)
    a_cum = jnp.cumsum(a, axis=2)
    causal = jnp.tril(jnp.ones((SSD_CHUNK, SSD_CHUNK), dtype=bool))[:, :, None, None]
    seg = a_cum[:, :, :, None] - a_cum[:, :, None, :]
    decay = jnp.exp(jnp.where(causal, seg, -jnp.inf))
    cb = jnp.einsum('bclgn,bcsgn->bclsg', c, b)
    y_diag = jnp.einsum('bclsgh,bcsghp->bclghp', cb[..., None] * decay, x)
    decay_states = jnp.exp(a_cum[:, :, -1:] - a_cum)
    states = jnp.einsum('bclgn,bclghp->bcghpn', b, x * decay_states[..., None])
    chunk_decay = jnp.exp(a_cum[:, :, -1])

    def step(h, inp):
        st, dec = inp
        return h * dec[..., None, None] + st, h

    h0 = jnp.zeros_like(states[:, 0])
    _, prev = lax.scan(step, h0, (jnp.moveaxis(states, 1, 0), jnp.moveaxis(chunk_decay, 1, 0)))
    prev = jnp.moveaxis(prev, 0, 1)
    y_off = jnp.einsum('bclgn,bcghpn->bclghp', c, prev) * jnp.exp(a_cum)[..., None]
    return (y_diag + y_off).reshape(bsz, seq, SSD_HEADS, SSD_HEAD_DIM)


def ssd_mixer(z, xbc, dt_raw, conv_w, conv_b, dt_bias, a_log, d_skip, norm_w):
    bsz, seq = z.shape[:2]
    xbc = jax.nn.silu(causal_depthwise_conv(xbc, conv_w, conv_b))
    xs, bm, cm = jnp.split(xbc, [SSD_WIDTH, SSD_WIDTH + SSD_GROUPS * SSD_STATE], axis=-1)
    xh = xs.reshape(bsz, seq, SSD_HEADS, SSD_HEAD_DIM)
    dt = jax.nn.softplus(dt_raw.astype(jnp.float32) + dt_bias.astype(jnp.float32))
    a = -jnp.exp(a_log.astype(jnp.float32))
    y = ssd_chunked(xh * dt[..., None], a * dt, bm.reshape(bsz, seq, SSD_GROUPS, SSD_STATE), cm.reshape(bsz, seq, SSD_GROUPS, SSD_STATE))
    y = y + d_skip[:, None] * xh
    y = y.reshape(bsz, seq, SSD_WIDTH).astype(z.dtype)
    return rmsnorm(y * jax.nn.silu(z), norm_w)


def selection_overlap(n_cmp, n_sel):
    cs = np.arange(n_cmp)[:, None] * CMP_STRIDE
    ce = cs + CMP_BLOCK
    ss = np.arange(n_sel)[None, :] * SEL_BLOCK
    se = ss + SEL_BLOCK
    ov = np.clip(np.minimum(ce, se) - np.maximum(cs, ss), 0, None)
    return (ov / CMP_BLOCK).astype(np.float32)


def compress_tokens(kv, w1, w2, pe):
    bsz, seq = kv.shape[:2]
    n_cmp = (seq - CMP_BLOCK) // CMP_STRIDE + 1
    idx = np.arange(n_cmp)[:, None] * CMP_STRIDE + np.arange(CMP_BLOCK)[None, :]
    blocks = kv[:, idx] + pe[None, None, :, None, :]
    flat = jnp.swapaxes(blocks, 2, 3).reshape(bsz, n_cmp, ATT_KV_HEADS, CMP_BLOCK * ATT_HEAD_DIM)
    return jax.nn.silu(flat @ w1) @ w2


def nsa_mixer(q, kc, vc, ks, vs, kw, vw, gate_raw, cmp_w1_k, cmp_w2_k, cmp_w1_v, cmp_w2_v, cmp_pe_k, cmp_pe_v, cos, sin):
    bsz, seq = q.shape[:2]
    hd = ATT_HEAD_DIM
    scale = hd ** -0.5
    q = q.reshape(bsz, seq, ATT_HEADS, hd)
    kc, vc, ks, vs, kw, vw = [t.reshape(bsz, seq, ATT_KV_HEADS, hd) for t in (kc, vc, ks, vs, kw, vw)]
    q_grp = q.reshape(bsz, seq, ATT_KV_HEADS, ATT_GROUP, hd)
    q_rot = partial_rope(q, cos, sin).reshape(bsz, seq, ATT_KV_HEADS, ATT_GROUP, hd)
    ks = partial_rope(ks, cos, sin)
    kw = partial_rope(kw, cos, sin)
    t_pos = jnp.arange(seq)

    k_cmp = compress_tokens(kc, cmp_w1_k, cmp_w2_k, cmp_pe_k)
    v_cmp = compress_tokens(vc, cmp_w1_v, cmp_w2_v, cmp_pe_v)
    n_cmp = k_cmp.shape[1]
    cmp_end = jnp.arange(n_cmp) * CMP_STRIDE + CMP_BLOCK - 1
    cmp_mask = cmp_end[None, :] <= t_pos[:, None]
    s_cmp = jnp.einsum('bshgd,bihd->bhgsi', q_grp, k_cmp).astype(jnp.float32) * scale
    p_cmp = jax.nn.softmax(jnp.where(cmp_mask, s_cmp, NEG_INF), axis=-1) * cmp_mask
    o_cmp = jnp.einsum('bhgsi,bihd->bshgd', p_cmp.astype(v_cmp.dtype), v_cmp)

    n_sel = seq // SEL_BLOCK
    k_top = min(N_SELECT, n_sel)
    overlap = jnp.asarray(selection_overlap(n_cmp, n_sel))
    imp = jnp.einsum('bhgsi,ij->bhsj', p_cmp, overlap)
    blk = jnp.arange(n_sel)[None, :]
    cur = (t_pos // SEL_BLOCK)[:, None]
    imp = jnp.where((blk == 0) | (blk == cur) | (blk == cur - 1), FORCE_SCORE, imp)
    imp = jnp.where(blk <= cur, imp, -1.0)
    top_val, top_idx = lax.top_k(imp, k_top)
    top_valid = top_val >= 0.0

    ks_blocks = ks.reshape(bsz, n_sel, SEL_BLOCK, ATT_KV_HEADS, hd).transpose(0, 3, 1, 2, 4)
    vs_blocks = vs.reshape(bsz, n_sel, SEL_BLOCK, ATT_KV_HEADS, hd).transpose(0, 3, 1, 2, 4)
    kw_pad = jnp.pad(kw, ((0, 0), (WINDOW, 0), (0, 0), (0, 0)))
    vw_pad = jnp.pad(vw, ((0, 0), (WINDOW, 0), (0, 0), (0, 0)))
    gather_blocks = jax.vmap(jax.vmap(lambda blocks, ix: blocks[ix]))
    q_offs = jnp.arange(Q_BLOCK)
    w_offs = jnp.arange(WINDOW + Q_BLOCK)
    s_offs = jnp.arange(SEL_BLOCK)

    def query_block(qb):
        s0 = qb * Q_BLOCK
        tq = s0 + q_offs
        qblk = lax.dynamic_slice_in_dim(q_rot, s0, Q_BLOCK, axis=1)
        ib = lax.dynamic_slice_in_dim(top_idx, s0, Q_BLOCK, axis=2)
        vb = lax.dynamic_slice_in_dim(top_valid, s0, Q_BLOCK, axis=2)
        kg = gather_blocks(ks_blocks, ib)
        vg = gather_blocks(vs_blocks, ib)
        kpos = ib[..., None] * SEL_BLOCK + s_offs
        m_sel = (kpos <= tq[:, None, None]) & vb[..., None]
        s_sel = jnp.einsum('bqhgd,bhqjkd->bhgqjk', qblk, kg).astype(jnp.float32) * scale
        s_sel = jnp.where(m_sel[:, :, None], s_sel, NEG_INF).reshape(bsz, ATT_KV_HEADS, ATT_GROUP, Q_BLOCK, k_top * SEL_BLOCK)
        p_sel = jax.nn.softmax(s_sel, axis=-1).astype(vg.dtype)
        o_sel = jnp.einsum('bhgqn,bhqnd->bqhgd', p_sel, vg.reshape(bsz, ATT_KV_HEADS, Q_BLOCK, k_top * SEL_BLOCK, hd))
        kwb = lax.dynamic_slice_in_dim(kw_pad, s0, WINDOW + Q_BLOCK, axis=1)
        vwb = lax.dynamic_slice_in_dim(vw_pad, s0, WINDOW + Q_BLOCK, axis=1)
        kp = s0 - WINDOW + w_offs
        dist = tq[:, None] - kp[None, :]
        m_win = (dist >= 0) & (dist < WINDOW) & (kp[None, :] >= 0)
        s_win = jnp.einsum('bqhgd,bkhd->bhgqk', qblk, kwb).astype(jnp.float32) * scale
        p_win = jax.nn.softmax(jnp.where(m_win, s_win, NEG_INF), axis=-1).astype(vwb.dtype)
        o_win = jnp.einsum('bhgqk,bkhd->bqhgd', p_win, vwb)
        return o_sel, o_win

    o_sel, o_win = lax.map(query_block, jnp.arange(seq // Q_BLOCK))
    o_sel = jnp.moveaxis(o_sel, 0, 1).reshape(bsz, seq, ATT_KV_HEADS, ATT_GROUP, hd)
    o_win = jnp.moveaxis(o_win, 0, 1).reshape(bsz, seq, ATT_KV_HEADS, ATT_GROUP, hd)
    g = jax.nn.sigmoid(gate_raw.astype(jnp.float32)).reshape(bsz, seq, ATT_KV_HEADS, ATT_GROUP, N_BRANCH, 1).astype(q.dtype)
    o = g[..., 0, :] * o_cmp + g[..., 1, :] * o_sel + g[..., 2, :] * o_win
    return o.reshape(bsz, seq, ATT_WIDTH)


def setup_inputs(seed: int = 0) -> dict:
    key = jax.random.key(seed)
    k = jax.random.split(key, 24)
    f32 = jnp.float32
    L = DEPTH

    def dense(kk, shape, fan_in):
        return jax.random.normal(kk, shape, f32) * fan_in ** -0.5

    def gain(kk, shape):
        return 1.0 + 0.05 * jax.random.normal(kk, shape, f32)

    dt = jnp.exp(jax.random.uniform(k[5], (L, SSD_HEADS), f32, math.log(1e-3), math.log(1e-1)))
    return {
        'x': jax.random.normal(k[0], (BATCH, SEQ, D_MODEL), f32),
        'attn_norm_w': gain(k[1], (L, D_MODEL)),
        'w_in': dense(k[2], (L, D_MODEL, D_IN), D_MODEL),
        'conv_w': dense(k[3], (L, CONV_WIDTH, CONV_CH), CONV_WIDTH),
        'conv_b': 0.01 * jax.random.normal(k[4], (L, CONV_CH), f32),
        'dt_bias': dt + jnp.log(-jnp.expm1(-dt)),
        'a_log': jnp.log(jax.random.uniform(k[6], (L, SSD_HEADS), f32, 1.0, 16.0)),
        'd_skip': 1.0 + 0.1 * jax.random.normal(k[7], (L, SSD_HEADS), f32),
        'ssd_norm_w': gain(k[8], (L, SSD_WIDTH)),
        'cmp_w1_k': dense(k[9], (L, CMP_BLOCK * ATT_HEAD_DIM, CMP_HIDDEN), CMP_BLOCK * ATT_HEAD_DIM),
        'cmp_w2_k': dense(k[10], (L, CMP_HIDDEN, ATT_HEAD_DIM), CMP_HIDDEN),
        'cmp_w1_v': dense(k[11], (L, CMP_BLOCK * ATT_HEAD_DIM, CMP_HIDDEN), CMP_BLOCK * ATT_HEAD_DIM),
        'cmp_w2_v': dense(k[12], (L, CMP_HIDDEN, ATT_HEAD_DIM), CMP_HIDDEN),
        'cmp_pe_k': 0.1 * jax.random.normal(k[13], (L, CMP_BLOCK, ATT_HEAD_DIM), f32),
        'cmp_pe_v': 0.1 * jax.random.normal(k[14], (L, CMP_BLOCK, ATT_HEAD_DIM), f32),
        'w_out': dense(k[15], (L, D_MIX, D_MODEL), D_MIX),
        'ffn_norm_w': gain(k[16], (L, D_MODEL)),
        'w_gate': dense(k[17], (L, D_MODEL, D_FF), D_MODEL),
        'w_up': dense(k[18], (L, D_MODEL, D_FF), D_MODEL),
        'w_down': dense(k[19], (L, D_FF, D_MODEL), D_FF),
        'final_norm_w': gain(k[20], (D_MODEL,)),
    }


def reference(x, attn_norm_w, w_in, conv_w, conv_b, dt_bias, a_log, d_skip, ssd_norm_w, cmp_w1_k, cmp_w2_k, cmp_w1_v, cmp_w2_v, cmp_pe_k, cmp_pe_v, w_out, ffn_norm_w, w_gate, w_up, w_down, final_norm_w):
    cos, sin = rope_tables(x.shape[1], x.dtype)
    split_at = np.cumsum(IN_WIDTHS)[:-1].tolist()
    h = x
    for l in range(DEPTH):
        u = rmsnorm(h, attn_norm_w[l])
        z, xbc, dt_raw, q, kc, vc, ks, vs, kw, vw, gate_raw = jnp.split(u @ w_in[l], split_at, axis=-1)
        y_ssd = ssd_mixer(z, xbc, dt_raw, conv_w[l], conv_b[l], dt_bias[l], a_log[l], d_skip[l], ssd_norm_w[l])
        y_att = nsa_mixer(q, kc, vc, ks, vs, kw, vw, gate_raw, cmp_w1_k[l], cmp_w2_k[l], cmp_w1_v[l], cmp_w2_v[l], cmp_pe_k[l], cmp_pe_v[l], cos, sin)
        mixed = jnp.concatenate([y_ssd.astype(h.dtype), y_att.astype(h.dtype)], axis=-1)
        h = h + mixed @ w_out[l]
        v = rmsnorm(h, ffn_norm_w[l])
        h = h + (jax.nn.silu(v @ w_gate[l]) * (v @ w_up[l])) @ w_down[l]
    return rmsnorm(h, final_norm_w)
```

```python
import functools

import numpy as np
import jax
import jax.numpy as jnp
from jax import lax
from jax.experimental import pallas as pl
from jax.experimental.pallas import tpu as pltpu

F32 = jnp.float32
BF16 = jnp.bfloat16
HI = lax.Precision.HIGHEST

D_MODEL = 2048
SSD_WIDTH = 1024
ATT_WIDTH = 1024
SSD_HEAD_DIM = 64
SSD_HEADS = 16
SSD_GROUPS = 2
SSD_STATE = 128
SSD_CHUNK = 128
CONV_WIDTH = 4
CONV_CH = SSD_WIDTH + 2 * SSD_GROUPS * SSD_STATE
HEAD_DIM = 64
ATT_HEADS = 16
KV_HEADS = 4
ATT_GROUP = 4
KV_WIDTH = KV_HEADS * HEAD_DIM
CMP_BLOCK = 32
CMP_STRIDE = 16
CMP_HIDDEN = 256
SEL_BLOCK = 64
N_SELECT = 16
WINDOW = 512
N_BRANCH = 3
ROPE_THETA = 500000.0
ROPE_DIM = 16
D_FF = 5632
NORM_EPS = 1e-6
NEG_INF = -1e30
FORCE_SCORE = 1e4
SCALE = HEAD_DIM ** -0.5
BLOCK_BIAS = -(2.0 ** 100)

LANES = 128
VMEM_LIMIT = 56 * 1024 * 1024

NP = 5376
COL_XBC = 0
COL_KV = 1536
COL_Z = 3072
COL_Q = 4096
COL_DT = 5120
COL_GATE = 5248

TQ = 128
TK = 512
ROWS = ATT_GROUP * TQ


def _silu(x):
    return x * jax.nn.sigmoid(x)


def _cparams(sem):
    return pltpu.CompilerParams(dimension_semantics=sem, vmem_limit_bytes=VMEM_LIMIT)


def _inproj_kernel(x_ref, nw_ref, w_ref, o_ref, u_ref):
    @pl.when(pl.program_id(1) == 0)
    def _():
        x = x_ref[...]
        ms = jnp.mean(x * x, axis=-1, keepdims=True)
        u_ref[...] = ((x * lax.rsqrt(ms + NORM_EPS)) * nw_ref[...]).astype(BF16)

    o_ref[...] = jnp.dot(u_ref[...], w_ref[...], preferred_element_type=F32)


def _in_proj(x2, norm_w, w_perm):
    t = x2.shape[0]
    tm, tn = 1024, 768
    return pl.pallas_call(
        _inproj_kernel,
        grid=(t // tm, NP // tn),
        in_specs=[
            pl.BlockSpec((tm, D_MODEL), lambda i, j: (i, 0)),
            pl.BlockSpec((1, D_MODEL), lambda i, j: (0, 0)),
            pl.BlockSpec((D_MODEL, tn), lambda i, j: (0, j)),
        ],
        out_specs=pl.BlockSpec((tm, tn), lambda i, j: (i, j)),
        out_shape=jax.ShapeDtypeStruct((t, NP), F32),
        scratch_shapes=[pltpu.VMEM((tm, D_MODEL), BF16)],
        compiler_params=_cparams(("parallel", "arbitrary")),
        name="in_proj",
    )(x2, norm_w, w_perm)


def _ssd_kernel(xbc_ref, z_ref, dt_ref, cw_ref, cb_ref, dtb_ref, alog_ref, dskip_ref, nw_ref,
                ltri_ref, ex_ref, o_ref, ext_ref, state_ref, y_ref):
    L = SSD_CHUNK
    c = pl.program_id(1)

    @pl.when(c == 0)
    def _():
        ext_ref[0:8, :] = jnp.zeros((8, CONV_CH), F32)
        state_ref[...] = jnp.zeros(state_ref.shape, F32)

    ext_ref[8:8 + L, :] = xbc_ref[...]
    w = cw_ref[...]
    y = (ext_ref[5:5 + L, :] * w[0:1, :] + ext_ref[6:6 + L, :] * w[1:2, :]
         + ext_ref[7:7 + L, :] * w[2:3, :] + ext_ref[8:8 + L, :] * w[3:4, :]) + cb_ref[...]
    tail = ext_ref[L:L + 8, :]
    ext_ref[0:8, :] = tail
    act = _silu(y)
    xs = act[:, :SSD_WIDTH]
    bm = act[:, SSD_WIDTH:SSD_WIDTH + SSD_GROUPS * SSD_STATE]
    cm = act[:, SSD_WIDTH + SSD_GROUPS * SSD_STATE:]

    v = dt_ref[...] + dtb_ref[...]
    dt = jnp.maximum(v, 0.0) + jnp.log1p(jnp.exp(-jnp.abs(v)))
    a = -jnp.exp(alog_ref[...])
    adt = a * dt
    acum = jnp.dot(ltri_ref[...], adt, precision=HI, preferred_element_type=F32)
    acum_t = acum.T
    last = acum[L - 1:L, :]
    stacked = jnp.concatenate([dt, jnp.exp(acum), jnp.exp(last - acum)], axis=0)
    expanded = jnp.dot(stacked, ex_ref[...], precision=HI, preferred_element_type=F32)
    dt_e = expanded[0:L]
    expa_e = expanded[L:2 * L]
    dst_e = expanded[2 * L:3 * L]

    xdt = xs * dt_e
    xds_b = (xdt * dst_e).astype(BF16)
    row = lax.broadcasted_iota(jnp.int32, (L, L), 0)
    col = lax.broadcasted_iota(jnp.int32, (L, L), 1)
    causal = row >= col
    lane = lax.broadcasted_iota(jnp.int32, (L, LANES), 1)
    hg = SSD_HEADS // SSD_GROUPS
    gw = hg * SSD_HEAD_DIM
    for g in range(SSD_GROUPS):
        bg = bm[:, g * SSD_STATE:(g + 1) * SSD_STATE]
        cg_b = cm[:, g * SSD_STATE:(g + 1) * SSD_STATE].astype(BF16)
        bg_b = bg.astype(BF16)
        cb = lax.dot_general(cg_b, bg_b, (((1,), (1,)), ((), ())), preferred_element_type=F32)
        hprev = state_ref[g]
        yoff = jnp.dot(cg_b, hprev.astype(BF16), preferred_element_type=F32) * expa_e[:, g * gw:(g + 1) * gw]
        snew = jnp.dot(bg.T.astype(BF16), xds_b[:, g * gw:(g + 1) * gw], preferred_element_type=F32)
        state_ref[g] = hprev * expa_e[L - 1:L, g * gw:(g + 1) * gw] + snew
        for k in range(hg // 2):
            pair = g * (hg // 2) + k
            ms = []
            for hh in (2 * pair, 2 * pair + 1):
                seg = acum[:, hh:hh + 1] - acum_t[hh:hh + 1, :]
                decay = jnp.exp(jnp.where(causal, seg, -jnp.inf))
                ms.append((cb * decay).astype(BF16))
            mpair = jnp.concatenate(ms, axis=1)
            xp = xdt[:, pair * LANES:(pair + 1) * LANES]
            xblk = jnp.concatenate([jnp.where(lane < SSD_HEAD_DIM, xp, 0.0),
                                    jnp.where(lane >= SSD_HEAD_DIM, xp, 0.0)], axis=0).astype(BF16)
            yd = jnp.dot(mpair, xblk, preferred_element_type=F32)
            y_ref[:, pair * LANES:(pair + 1) * LANES] = yd + yoff[:, k * LANES:(k + 1) * LANES]

    yy = y_ref[...] + dskip_ref[...] * xs
    z = z_ref[...]
    yz = yy * _silu(z)
    ms2 = jnp.mean(yz * yz, axis=-1, keepdims=True)
    o_ref[...] = ((yz * lax.rsqrt(ms2 + NORM_EPS)) * nw_ref[...]).astype(o_ref.dtype)


def _ssd(proj, bsz, seq, conv_w, conv_b, dtb_pad, alog_pad, dskip_e, norm_w):
    nc = seq // SSD_CHUNK
    L = SSD_CHUNK
    ltri = jnp.asarray(np.tril(np.ones((L, L), np.float32)))
    ex = np.zeros((LANES, SSD_WIDTH), np.float32)
    for h in range(SSD_HEADS):
        ex[h, h * SSD_HEAD_DIM:(h + 1) * SSD_HEAD_DIM] = 1.0
    ex = jnp.asarray(ex)
    const = lambda shape: pl.BlockSpec(shape, lambda b, c: (0,) * len(shape))
    return pl.pallas_call(
        _ssd_kernel,
        grid=(bsz, nc),
        in_specs=[
            pl.BlockSpec((L, CONV_CH), lambda b, c: (b * nc + c, COL_XBC // CONV_CH)),
            pl.BlockSpec((L, SSD_WIDTH), lambda b, c: (b * nc + c, COL_Z // SSD_WIDTH)),
            pl.BlockSpec((L, LANES), lambda b, c: (b * nc + c, COL_DT // LANES)),
            const((CONV_WIDTH, CONV_CH)), const((1, CONV_CH)), const((1, LANES)), const((1, LANES)),
            const((1, SSD_WIDTH)), const((1, SSD_WIDTH)), const((L, L)), const((LANES, SSD_WIDTH)),
        ],
        out_specs=pl.BlockSpec((L, SSD_WIDTH), lambda b, c: (b * nc + c, 0)),
        out_shape=jax.ShapeDtypeStruct((bsz * seq, SSD_WIDTH), BF16),
        scratch_shapes=[
            pltpu.VMEM((L + 8, CONV_CH), F32),
            pltpu.VMEM((SSD_GROUPS, SSD_STATE, (SSD_HEADS // SSD_GROUPS) * SSD_HEAD_DIM), F32),
            pltpu.VMEM((L, SSD_WIDTH), F32),
        ],
        compiler_params=_cparams(("parallel", "arbitrary")),
        name="ssd",
    )(proj, proj, proj, conv_w, conv_b, dtb_pad, alog_pad, dskip_e, norm_w, ltri, ex)


def _rope(x, cos, s1, s2):
    n = x.shape[-1]
    half = ROPE_DIM // 2
    return x * cos + pltpu.roll(x, n - half, 1) * s1 + pltpu.roll(x, half, 1) * s2


def _kvprep_kernel(ks_ref, vs_ref, kw_ref, vw_ref, cos_ref, s1_ref, s2_ref, sel_ref,
                   kso_ref, vso_ref, kwo_ref, vwo_ref, *, tiles_per_seq):
    tr = ks_ref.shape[0]
    wide = KV_HEADS * LANES
    s0 = (pl.program_id(0) % tiles_per_seq) * tr
    cos, s1, s2 = cos_ref[...], s1_ref[...], s2_ref[...]
    sel = sel_ref[...]
    lane = lax.broadcasted_iota(jnp.int32, (tr, wide), 1) % LANES
    blk = (s0 + lax.broadcasted_iota(jnp.int32, (tr, wide), 0)) // SEL_BLOCK
    ebias = jnp.where(lane - SEL_BLOCK == blk, BLOCK_BIAS, 0.0)
    ones = jnp.where(lane == HEAD_DIM, 1.0, 0.0)

    def place(x):
        return jnp.dot(x.astype(BF16), sel, preferred_element_type=F32)

    kso_ref[...] = (place(_rope(ks_ref[...], cos, s1, s2)) + ebias).astype(BF16)
    kwo_ref[...] = place(_rope(kw_ref[...], cos, s1, s2)).astype(BF16)
    vso_ref[...] = (place(vs_ref[...]) + ones).astype(BF16)
    vwo_ref[...] = (place(vw_ref[...]) + ones).astype(BF16)


def _head_place_matrix():
    m = np.zeros((KV_WIDTH, KV_HEADS * LANES), np.float32)
    for h in range(KV_HEADS):
        for d in range(HEAD_DIM):
            m[h * HEAD_DIM + d, h * LANES + d] = 1.0
    return m


def _kv_prep(proj, seq, cos_t, s1_t, s2_t):
    t = proj.shape[0]
    tr = 512
    tps = seq // tr
    sel = jnp.asarray(_head_place_matrix(), BF16)
    kvb = COL_KV // KV_WIDTH
    seg = lambda k: pl.BlockSpec((tr, KV_WIDTH), lambda i: (i, kvb + k))
    tab = pl.BlockSpec((tr, KV_WIDTH), lambda i: (i % tps, 0))
    wide = KV_HEADS * LANES
    out = pl.BlockSpec((tr, wide), lambda i: (i, 0))
    shp = jax.ShapeDtypeStruct((t, wide), BF16)
    return pl.pallas_call(
        functools.partial(_kvprep_kernel, tiles_per_seq=tps),
        grid=(t // tr,),
        in_specs=[seg(2), seg(3), seg(4), seg(5), tab, tab, tab,
                  pl.BlockSpec((KV_WIDTH, wide), lambda i: (0, 0))],
        out_specs=[out, out, out, out],
        out_shape=[shp, shp, shp, shp],
        compiler_params=_cparams(("parallel",)),
        name="kv_prep",
    )(proj, proj, proj, proj, cos_t, s1_t, s2_t, sel)


def _compress_kernel(gk_ref, gv_ref, pek_ref, pev_ref, w1k_ref, w1v_ref, w2kt_ref, w2v_ref,
                     kblk_ref, vblk_ref, shift_ref):
    ng = gk_ref.shape[2]
    half = CMP_BLOCK // 2 * HEAD_DIM

    def hidden(g_ref, pe_ref, w1_ref):
        g = g_ref[0, 0]
        top = jnp.dot((g + pe_ref[0:1, :]).astype(BF16), w1_ref[0:half, :], preferred_element_type=F32)
        bot = jnp.dot((g + pe_ref[1:2, :]).astype(BF16), w1_ref[half:2 * half, :], preferred_element_type=F32)
        shift_ref[0:ng, :] = bot
        shift_ref[ng:ng + 8, :] = jnp.zeros((8, CMP_HIDDEN), F32)
        return _silu(top + shift_ref[1:ng + 1, :]).astype(BF16)

    hk = hidden(gk_ref, pek_ref, w1k_ref)
    kt = lax.dot_general(w2kt_ref[...], hk, (((1,), (1,)), ((), ())), preferred_element_type=F32)
    kblk_ref[...] = jnp.zeros(kblk_ref.shape, BF16)
    for g in range(ATT_GROUP):
        kblk_ref[0, 0, g * HEAD_DIM:(g + 1) * HEAD_DIM, g * ng:(g + 1) * ng] = kt.astype(BF16)

    hv = hidden(gv_ref, pev_ref, w1v_ref)
    vc = jnp.dot(hv, w2v_ref[...], preferred_element_type=F32)
    vblk_ref[...] = jnp.zeros(vblk_ref.shape, BF16)
    for g in range(ATT_GROUP):
        vblk_ref[0, 0, g * ng:(g + 1) * ng, g * LANES:(g + 1) * LANES] = vc.astype(BF16)


def _compress(gk, gv, pek, pev, w1k, w1v, w2kt, w2v):
    bsz, _, ng, width = gk.shape
    gspec = pl.BlockSpec((1, 1, ng, width), lambda b, h: (b, h, 0, 0))
    const = lambda shape: pl.BlockSpec(shape, lambda b, h: (0,) * len(shape))
    return pl.pallas_call(
        _compress_kernel,
        grid=(bsz, KV_HEADS),
        in_specs=[gspec, gspec, const((2, width)), const((2, width)),
                  const((2 * width, CMP_HIDDEN)), const((2 * width, CMP_HIDDEN)),
                  const((HEAD_DIM, CMP_HIDDEN)), const((CMP_HIDDEN, LANES))],
        out_specs=[pl.BlockSpec((1, 1, KV_WIDTH, ATT_GROUP * ng), lambda b, h: (b, h, 0, 0)),
                   pl.BlockSpec((1, 1, ATT_GROUP * ng, ATT_GROUP * LANES), lambda b, h: (b, h, 0, 0))],
        out_shape=[jax.ShapeDtypeStruct((bsz, KV_HEADS, KV_WIDTH, ATT_GROUP * ng), BF16),
                   jax.ShapeDtypeStruct((bsz, KV_HEADS, ATT_GROUP * ng, ATT_GROUP * LANES), BF16)],
        scratch_shapes=[pltpu.VMEM((ng + 8, CMP_HIDDEN), F32)],
        compiler_params=_cparams(("parallel", "parallel")),
        name="compress",
    )(gk, gv, pek, pev, w1k, w1v, w2kt, w2v)


def _nsacmp_kernel(q_ref, kblk_ref, vblk_ref, cos_ref, s1_ref, s2_ref, ovt_ref, selq_ref, placen_ref, eye_ref,
                   qaug_ref, ocmp_ref):
    ncp = kblk_ref.shape[3] // ATT_GROUP
    nsel = ovt_ref.shape[0]
    s0 = pl.program_id(2) * TQ
    q = q_ref[...]
    s_all = jnp.dot(q.astype(BF16), kblk_ref[0, 0], preferred_element_type=F32) * SCALE
    tq_pos = s0 + lax.broadcasted_iota(jnp.int32, (TQ, ncp), 0)
    cmp_end = lax.broadcasted_iota(jnp.int32, (TQ, ncp), 1) * CMP_STRIDE + (CMP_BLOCK - 1)
    mask = cmp_end <= tq_pos
    maskf = mask.astype(F32)
    ps = []
    for g in range(ATT_GROUP):
        s = jnp.where(mask, s_all[:, g * ncp:(g + 1) * ncp], NEG_INF)
        m = jnp.max(s, axis=-1, keepdims=True)
        e = jnp.exp(s - m)
        ps.append(e / jnp.sum(e, axis=-1, keepdims=True) * maskf)
    p_all = jnp.concatenate(ps, axis=1)

    ow = jnp.dot(p_all.astype(BF16), vblk_ref[0, 0], preferred_element_type=F32)
    ocmp_ref[0, 0] = jnp.concatenate([ow[:, g * LANES:(g + 1) * LANES] for g in range(ATT_GROUP)], axis=0)

    imp = lax.dot_general(ovt_ref[...], p_all, (((1,), (1,)), ((), ())), precision=HI,
                          preferred_element_type=F32)
    j = lax.broadcasted_iota(jnp.int32, (nsel, TQ), 0)
    cur = (s0 + lax.broadcasted_iota(jnp.int32, (nsel, TQ), 1)) // SEL_BLOCK
    imp = jnp.where((j == 0) | (j == cur) | (j == cur - 1), FORCE_SCORE, imp)
    imp = jnp.where(j <= cur, imp, -1.0)
    rank = jnp.zeros((nsel, TQ), F32)
    for jp in range(nsel):
        r = imp[jp:jp + 1, :]
        tie = jnp.where(j > jp, 1.0, 0.0)
        rank = rank + jnp.where(r > imp, 1.0, jnp.where(r == imp, tie, 0.0))
    notsel_t = jnp.where((rank < float(N_SELECT)) & (imp >= 0.0), 0.0, 1.0).astype(BF16)
    notsel = lax.dot_general(eye_ref[...], notsel_t, (((1,), (1,)), ((), ())),
                             preferred_element_type=F32)

    q_rot = (_rope(q, cos_ref[...], s1_ref[...], s2_ref[...]) * SCALE).astype(BF16)
    qw = (jnp.dot(q_rot, selq_ref[...], preferred_element_type=F32)
          + jnp.dot(notsel.astype(BF16), placen_ref[...], preferred_element_type=F32))
    qaug_ref[0, 0] = jnp.concatenate([qw[:, g * LANES:(g + 1) * LANES] for g in range(ATT_GROUP)],
                                     axis=0).astype(BF16)


def _selection_overlap_t(n_cmp_pad, n_cmp, n_sel):
    cs = np.arange(n_cmp)[:, None] * CMP_STRIDE
    ce = cs + CMP_BLOCK
    ss = np.arange(n_sel)[None, :] * SEL_BLOCK
    se = ss + SEL_BLOCK
    ov = np.clip(np.minimum(ce, se) - np.maximum(cs, ss), 0, None) / CMP_BLOCK
    full = np.zeros((n_cmp_pad, n_sel), np.float32)
    full[:n_cmp] = ov
    return np.tile(full, (ATT_GROUP, 1)).T.copy()


def _nsa_cmp(proj, kblk, vblk, bsz, seq, cos_t, s1_t, s2_t):
    nqt = seq // TQ
    ncp = kblk.shape[3] // ATT_GROUP
    n_cmp = (seq - CMP_BLOCK) // CMP_STRIDE + 1
    nsel = seq // SEL_BLOCK
    ovt = jnp.asarray(_selection_overlap_t(ncp, n_cmp, nsel))
    selq = jnp.asarray(_head_place_matrix(), BF16)
    placen = np.zeros((nsel, ATT_GROUP * LANES), np.float32)
    for g in range(ATT_GROUP):
        for jb in range(nsel):
            placen[jb, g * LANES + HEAD_DIM + jb] = 1.0
    placen = jnp.asarray(placen, BF16)
    eye = jnp.asarray(np.eye(TQ, dtype=np.float32), BF16)
    const = lambda shape: pl.BlockSpec(shape, lambda b, h, t: (0,) * len(shape))
    tab = pl.BlockSpec((TQ, KV_WIDTH), lambda b, h, t: (t, 0))
    qb = COL_Q // KV_WIDTH
    stacked = pl.BlockSpec((1, 1, ROWS, LANES), lambda b, h, t: (b, h, t, 0))
    return pl.pallas_call(
        _nsacmp_kernel,
        grid=(bsz, KV_HEADS, nqt),
        in_specs=[
            pl.BlockSpec((TQ, KV_WIDTH), lambda b, h, t: (b * nqt + t, qb + h)),
            pl.BlockSpec((1, 1, KV_WIDTH, ATT_GROUP * ncp), lambda b, h, t: (b, h, 0, 0)),
            pl.BlockSpec((1, 1, ATT_GROUP * ncp, ATT_GROUP * LANES), lambda b, h, t: (b, h, 0, 0)),
            tab, tab, tab,
            const((nsel, ATT_GROUP * ncp)), const((KV_WIDTH, ATT_GROUP * LANES)),
            const((nsel, ATT_GROUP * LANES)), const((TQ, TQ)),
        ],
        out_specs=[stacked, stacked],
        out_shape=[jax.ShapeDtypeStruct((bsz, KV_HEADS, nqt * ROWS, LANES), BF16),
                   jax.ShapeDtypeStruct((bsz, KV_HEADS, nqt * ROWS, LANES), F32)],
        compiler_params=_cparams(("parallel", "parallel", "parallel")),
        name="nsa_cmp",
    )(proj, kblk, vblk, cos_t, s1_t, s2_t, ovt, selq, placen, eye)


def _nsaattn_kernel(qaug_ref, ocmp_ref, ks_ref, vs_ref, kw_ref, vw_ref, gate_ref, gsel_ref, o_ref,
                    m_ref, acc_ref):
    qt = pl.program_id(2)
    s0 = qt * TQ
    q = qaug_ref[0, 0]
    nt = (((1,), (1,)), ((), ()))
    q_pos = s0 + lax.broadcasted_iota(jnp.int32, (ROWS, 1), 0) % TQ

    m_ref[...] = jnp.full(m_ref.shape, NEG_INF, F32)
    acc_ref[...] = jnp.zeros(acc_ref.shape, F32)

    def tile(kb, causal):
        start = pl.multiple_of(kb * TK, TK)
        k = ks_ref[pl.ds(start, TK), :]
        v = vs_ref[pl.ds(start, TK), :]
        s = lax.dot_general(q, k, nt, preferred_element_type=F32)
        if causal:
            k_pos = start + lax.broadcasted_iota(jnp.int32, (ROWS, TK), 1)
            s = jnp.where(k_pos <= q_pos, s, NEG_INF)
        m_prev = m_ref[...]
        m_next = jnp.maximum(m_prev, jnp.max(s, axis=-1, keepdims=True))
        p = jnp.exp(s - pltpu.repeat(m_next, TK // LANES, axis=1))
        acc_ref[...] = acc_ref[...] * jnp.exp(m_prev - m_next) + jnp.dot(
            p.astype(BF16), v, preferred_element_type=F32)
        m_ref[...] = m_next

    nfull = (qt * TQ) // TK

    def body(kb, carry):
        tile(kb, False)
        return carry

    lax.fori_loop(0, nfull, body, 0)
    tile(nfull, True)
    acc = acc_ref[...]
    o_sel = acc / acc[:, HEAD_DIM:HEAD_DIM + 1]

    wk = WINDOW + TQ
    wstart = pl.multiple_of(jnp.maximum(s0 - WINDOW, 0), TQ)
    kwin = kw_ref[pl.ds(wstart, wk), :]
    vwin = vw_ref[pl.ds(wstart, wk), :]
    sw = lax.dot_general(q, kwin, nt, preferred_element_type=F32)
    dist = q_pos - (wstart + lax.broadcasted_iota(jnp.int32, (ROWS, wk), 1))
    sw = jnp.where((dist >= 0) & (dist < WINDOW), sw, NEG_INF)
    pw = jnp.exp(sw - jnp.max(sw, axis=-1, keepdims=True))
    ow = jnp.dot(pw.astype(BF16), vwin, preferred_element_type=F32)
    o_win = ow / ow[:, HEAD_DIM:HEAD_DIM + 1]

    sig = jax.nn.sigmoid(gate_ref[...])
    sg = jnp.dot(sig, gsel_ref[0], precision=HI, preferred_element_type=F32)
    ocmp = ocmp_ref[0, 0]
    comb = []
    for g in range(ATT_GROUP):
        r = slice(g * TQ, (g + 1) * TQ)
        c = g * N_BRANCH
        comb.append(sg[:, c:c + 1] * ocmp[r] + sg[:, c + 1:c + 2] * o_sel[r] + sg[:, c + 2:c + 3] * o_win[r])
    lane = lax.broadcasted_iota(jnp.int32, (TQ, LANES), 1)
    pairs = [jnp.where(lane < HEAD_DIM, comb[2 * k], pltpu.roll(comb[2 * k + 1], HEAD_DIM, 1))
             for k in range(ATT_GROUP // 2)]
    o_ref[...] = jnp.concatenate(pairs, axis=1).astype(o_ref.dtype)


def _nsa_attn(qaug, ocmp, ksa, vsa, kwa, vwa, proj, bsz, seq):
    nqt = seq // TQ
    gsel = np.zeros((KV_HEADS, LANES, LANES), np.float32)
    for h in range(KV_HEADS):
        for c in range(ATT_GROUP * N_BRANCH):
            gsel[h, h * ATT_GROUP * N_BRANCH + c, c] = 1.0
    gsel = jnp.asarray(gsel)
    stacked = pl.BlockSpec((1, 1, ROWS, LANES), lambda b, h, t: (b, h, t, 0))
    kv = pl.BlockSpec((seq, LANES), lambda b, h, t: (b, h))
    return pl.pallas_call(
        _nsaattn_kernel,
        grid=(bsz, KV_HEADS, nqt),
        in_specs=[stacked, stacked, kv, kv, kv, kv,
                  pl.BlockSpec((TQ, LANES), lambda b, h, t: (b * nqt + t, COL_GATE // LANES)),
                  pl.BlockSpec((1, LANES, LANES), lambda b, h, t: (h, 0, 0))],
        out_specs=pl.BlockSpec((TQ, KV_WIDTH), lambda b, h, t: (b * nqt + t, h)),
        out_shape=jax.ShapeDtypeStruct((bsz * seq, ATT_WIDTH), BF16),
        scratch_shapes=[pltpu.VMEM((ROWS, LANES), F32), pltpu.VMEM((ROWS, LANES), F32)],
        compiler_params=_cparams(("parallel", "parallel", "arbitrary")),
        name="nsa_attn",
    )(qaug, ocmp, ksa, vsa, kwa, vwa, proj, gsel)


def _outproj_kernel(x_ref, ys_ref, ya_ref, w1_ref, w2_ref, o_ref):
    o_ref[...] = (x_ref[...] + jnp.dot(ys_ref[...], w1_ref[...], preferred_element_type=F32)
                  + jnp.dot(ya_ref[...], w2_ref[...], preferred_element_type=F32))


def _out_proj(x2, ys, ya, wo_b):
    t = x2.shape[0]
    tm = 512
    return pl.pallas_call(
        _outproj_kernel,
        grid=(t // tm,),
        in_specs=[
            pl.BlockSpec((tm, D_MODEL), lambda i: (i, 0)),
            pl.BlockSpec((tm, SSD_WIDTH), lambda i: (i, 0)),
            pl.BlockSpec((tm, ATT_WIDTH), lambda i: (i, 0)),
            pl.BlockSpec((SSD_WIDTH, D_MODEL), lambda i: (0, 0)),
            pl.BlockSpec((ATT_WIDTH, D_MODEL), lambda i: (1, 0)),
        ],
        out_specs=pl.BlockSpec((tm, D_MODEL), lambda i: (i, 0)),
        out_shape=jax.ShapeDtypeStruct((t, D_MODEL), F32),
        compiler_params=_cparams(("parallel",)),
        name="out_proj",
    )(x2, ys, ya, wo_b, wo_b)


def _ffn_kernel(h_ref, nw_ref, fw_ref, wg_ref, wu_ref, wd_ref, o_ref, v_ref, acc_ref):
    j = pl.program_id(1)

    @pl.when(j == 0)
    def _():
        h = h_ref[...]
        ms = jnp.mean(h * h, axis=-1, keepdims=True)
        v_ref[...] = ((h * lax.rsqrt(ms + NORM_EPS)) * nw_ref[...]).astype(BF16)
        acc_ref[...] = jnp.zeros(acc_ref.shape, F32)

    v = v_ref[...]
    gate = jnp.dot(v, wg_ref[...], preferred_element_type=F32)
    up = jnp.dot(v, wu_ref[...], preferred_element_type=F32)
    acc_ref[...] += jnp.dot((_silu(gate) * up).astype(BF16), wd_ref[...], preferred_element_type=F32)

    @pl.when(j == pl.num_programs(1) - 1)
    def _():
        h2 = h_ref[...] + acc_ref[...]
        ms = jnp.mean(h2 * h2, axis=-1, keepdims=True)
        o_ref[...] = (h2 * lax.rsqrt(ms + NORM_EPS)) * fw_ref[...]


def _ffn(h1, ffn_nw, final_w, wg_b, wu_b, wd_b):
    t = h1.shape[0]
    tm, tf = 512, 512
    return pl.pallas_call(
        _ffn_kernel,
        grid=(t // tm, D_FF // tf),
        in_specs=[
            pl.BlockSpec((tm, D_MODEL), lambda i, j: (i, 0)),
            pl.BlockSpec((1, D_MODEL), lambda i, j: (0, 0)),
            pl.BlockSpec((1, D_MODEL), lambda i, j: (0, 0)),
            pl.BlockSpec((D_MODEL, tf), lambda i, j: (0, j)),
            pl.BlockSpec((D_MODEL, tf), lambda i, j: (0, j)),
            pl.BlockSpec((tf, D_MODEL), lambda i, j: (j, 0)),
        ],
        out_specs=pl.BlockSpec((tm, D_MODEL), lambda i, j: (i, 0)),
        out_shape=jax.ShapeDtypeStruct((t, D_MODEL), F32),
        scratch_shapes=[pltpu.VMEM((tm, D_MODEL), BF16), pltpu.VMEM((tm, D_MODEL), F32)],
        compiler_params=_cparams(("parallel", "arbitrary")),
        name="ffn",
    )(h1, ffn_nw, final_w, wg_b, wu_b, wd_b)


def _rope_tables(seq):
    inv = 1.0 / (ROPE_THETA ** (jnp.arange(0, ROPE_DIM, 2, dtype=F32) / ROPE_DIM))
    ang = jnp.arange(seq, dtype=F32)[:, None] * inv[None, :]
    cos, sin = jnp.cos(ang), jnp.sin(ang)
    half = ROPE_DIM // 2
    rest_one = jnp.ones((seq, HEAD_DIM - ROPE_DIM), F32)
    rest_zero = jnp.zeros((seq, HEAD_DIM - ROPE_DIM), F32)
    zero_h = jnp.zeros((seq, half), F32)
    cos_h = jnp.concatenate([cos, cos, rest_one], axis=1)
    s1_h = jnp.concatenate([-sin, zero_h, rest_zero], axis=1)
    s2_h = jnp.concatenate([zero_h, sin, rest_zero], axis=1)
    tile = lambda a: jnp.tile(a, (1, KV_HEADS))
    return tile(cos_h), tile(s1_h), tile(s2_h)


def _pad_lanes(a, width):
    return jnp.pad(a, ((0, 0), (0, width - a.shape[1])))


def _layer(h2d, bsz, seq, p):
    (attn_norm_w, w_in, conv_w, conv_b, dt_bias, a_log, d_skip, ssd_norm_w, cmp_w1_k, cmp_w2_k, cmp_w1_v,
     cmp_w2_v, cmp_pe_k, cmp_pe_v, w_out, ffn_norm_w, w_gate, w_up, w_down) = p
    o_xbc, o_dt, o_q, o_kv, o_gate = 1024, 2560, 2576, 3600, 5136
    w_perm = jnp.concatenate([
        w_in[:, o_xbc:o_dt], w_in[:, o_kv:o_gate], w_in[:, :o_xbc], w_in[:, o_q:o_kv],
        _pad_lanes(w_in[:, o_dt:o_q], LANES), _pad_lanes(w_in[:, o_gate:], LANES)], axis=1).astype(BF16)
    proj = _in_proj(h2d, attn_norm_w[None, :], w_perm)

    y_ssd = _ssd(proj, bsz, seq, conv_w, conv_b[None, :], _pad_lanes(dt_bias[None, :], LANES),
                 _pad_lanes(a_log[None, :], LANES), jnp.repeat(d_skip, SSD_HEAD_DIM)[None, :],
                 ssd_norm_w[None, :])

    cos_t, s1_t, s2_t = _rope_tables(seq)
    ksa, vsa, kwa, vwa = _kv_prep(proj, seq, cos_t, s1_t, s2_t)

    ng = seq // CMP_STRIDE

    def groups(col):
        a = proj[:, col:col + KV_WIDTH].reshape(bsz, ng, CMP_STRIDE, KV_HEADS, HEAD_DIM)
        return a.transpose(0, 3, 1, 2, 4).reshape(bsz, KV_HEADS, ng, CMP_STRIDE * HEAD_DIM)

    pe2 = lambda pe: pe.reshape(2, CMP_STRIDE * HEAD_DIM)
    kblk, vblk = _compress(groups(COL_KV), groups(COL_KV + KV_WIDTH), pe2(cmp_pe_k), pe2(cmp_pe_v),
                           cmp_w1_k.astype(BF16), cmp_w1_v.astype(BF16), cmp_w2_k.T.astype(BF16),
                           _pad_lanes(cmp_w2_v, LANES).astype(BF16))
    qaug, ocmp = _nsa_cmp(proj, kblk, vblk, bsz, seq, cos_t, s1_t, s2_t)
    y_att = _nsa_attn(qaug, ocmp, ksa, vsa, kwa, vwa, proj, bsz, seq)

    h1 = _out_proj(h2d, y_ssd, y_att, w_out.astype(BF16))
    return h1, (ffn_norm_w, w_gate.astype(BF16), w_up.astype(BF16), w_down.astype(BF16))


def kernel(x, attn_norm_w, w_in, conv_w, conv_b, dt_bias, a_log, d_skip, ssd_norm_w, cmp_w1_k, cmp_w2_k,
           cmp_w1_v, cmp_w2_v, cmp_pe_k, cmp_pe_v, w_out, ffn_norm_w, w_gate, w_up, w_down, final_norm_w):
    bsz, seq, _ = x.shape
    depth = w_in.shape[0]
    assert depth == 1, "the final rmsnorm is fused into the last layer's ffn kernel"
    h = x.reshape(bsz * seq, D_MODEL)
    l = 0
    params = (attn_norm_w[l], w_in[l], conv_w[l], conv_b[l], dt_bias[l], a_log[l], d_skip[l], ssd_norm_w[l],
              cmp_w1_k[l], cmp_w2_k[l], cmp_w1_v[l], cmp_w2_v[l], cmp_pe_k[l], cmp_pe_v[l], w_out[l],
              ffn_norm_w[l], w_gate[l], w_up[l], w_down[l])
    h1, (fnw, wg_b, wu_b, wd_b) = _layer(h, bsz, seq, params)
    out = _ffn(h1, fnw[None, :], final_norm_w[None, :], wg_b, wu_b, wd_b)
    return out.reshape(bsz, seq, D_MODEL)
```

```python
import functools

import numpy as np
import jax
import jax.numpy as jnp
from jax import lax
from jax.experimental import pallas as pl
from jax.experimental.pallas import tpu as pltpu

F32 = jnp.float32
BF16 = jnp.bfloat16
HI = lax.Precision.HIGHEST

D_MODEL = 2048
SSD_WIDTH = 1024
ATT_WIDTH = 1024
SSD_HEAD_DIM = 64
SSD_HEADS = 16
SSD_GROUPS = 2
SSD_STATE = 128
SSD_CHUNK = 128
CONV_WIDTH = 4
CONV_CH = SSD_WIDTH + 2 * SSD_GROUPS * SSD_STATE
HEAD_DIM = 64
ATT_HEADS = 16
KV_HEADS = 4
ATT_GROUP = 4
KV_WIDTH = KV_HEADS * HEAD_DIM
CMP_BLOCK = 32
CMP_STRIDE = 16
CMP_HIDDEN = 256
SEL_BLOCK = 64
N_SELECT = 16
WINDOW = 512
N_BRANCH = 3
ROPE_THETA = 500000.0
ROPE_DIM = 16
D_FF = 5632
NORM_EPS = 1e-6
NEG_INF = -1e30
FORCE_SCORE = 1e4
SCALE = HEAD_DIM ** -0.5
BLOCK_BIAS = -(2.0 ** 100)

LANES = 128
VMEM_LIMIT = 56 * 1024 * 1024

NP = 5376
COL_XBC = 0
COL_KV = 1536
COL_Z = 3072
COL_Q = 4096
COL_DT = 5120
COL_GATE = 5248

TQ = 256
TK = 512
ROWS = ATT_GROUP * TQ


def _silu(x):
    return x * jax.nn.sigmoid(x)


def _cparams(sem):
    return pltpu.CompilerParams(dimension_semantics=sem, vmem_limit_bytes=VMEM_LIMIT)


def _inproj_kernel(x_ref, nw_ref, w_ref, o_ref, u_ref):
    @pl.when(pl.program_id(1) == 0)
    def _():
        x = x_ref[...]
        ms = jnp.mean(x * x, axis=-1, keepdims=True)
        u_ref[...] = ((x * lax.rsqrt(ms + NORM_EPS)) * nw_ref[...]).astype(BF16)

    o_ref[...] = jnp.dot(u_ref[...], w_ref[...], preferred_element_type=F32)


def _in_proj(x2, norm_w, w_perm):
    t = x2.shape[0]
    tm, tn = 1024, 768
    return pl.pallas_call(
        _inproj_kernel,
        grid=(t // tm, NP // tn),
        in_specs=[
            pl.BlockSpec((tm, D_MODEL), lambda i, j: (i, 0)),
            pl.BlockSpec((1, D_MODEL), lambda i, j: (0, 0)),
            pl.BlockSpec((D_MODEL, tn), lambda i, j: (0, j)),
        ],
        out_specs=pl.BlockSpec((tm, tn), lambda i, j: (i, j)),
        out_shape=jax.ShapeDtypeStruct((t, NP), F32),
        scratch_shapes=[pltpu.VMEM((tm, D_MODEL), BF16)],
        compiler_params=_cparams(("parallel", "arbitrary")),
        name="in_proj",
    )(x2, norm_w, w_perm)


def _ssd_kernel(xbc_ref, z_ref, dt_ref, cw_ref, cb_ref, dtb_ref, alog_ref, dskip_ref, nw_ref,
                ltri_ref, ex_ref, o_ref, ext_ref, state_ref, y_ref):
    L = SSD_CHUNK
    c = pl.program_id(1)

    @pl.when(c == 0)
    def _():
        ext_ref[0:8, :] = jnp.zeros((8, CONV_CH), F32)
        state_ref[...] = jnp.zeros(state_ref.shape, F32)

    ext_ref[8:8 + L, :] = xbc_ref[...]
    w = cw_ref[...]
    y = (ext_ref[5:5 + L, :] * w[0:1, :] + ext_ref[6:6 + L, :] * w[1:2, :]
         + ext_ref[7:7 + L, :] * w[2:3, :] + ext_ref[8:8 + L, :] * w[3:4, :]) + cb_ref[...]
    tail = ext_ref[L:L + 8, :]
    ext_ref[0:8, :] = tail
    act = _silu(y)
    xs = act[:, :SSD_WIDTH]
    bm = act[:, SSD_WIDTH:SSD_WIDTH + SSD_GROUPS * SSD_STATE]
    cm = act[:, SSD_WIDTH + SSD_GROUPS * SSD_STATE:]

    v = dt_ref[...] + dtb_ref[...]
    dt = jnp.maximum(v, 0.0) + jnp.log1p(jnp.exp(-jnp.abs(v)))
    a = -jnp.exp(alog_ref[...])
    adt = a * dt
    acum = jnp.dot(ltri_ref[...], adt, precision=HI, preferred_element_type=F32)
    acum_t = acum.T
    last = acum[L - 1:L, :]
    stacked = jnp.concatenate([dt, jnp.exp(acum), jnp.exp(last - acum)], axis=0)
    expanded = jnp.dot(stacked, ex_ref[...], precision=HI, preferred_element_type=F32)
    dt_e = expanded[0:L]
    expa_e = expanded[L:2 * L]
    dst_e = expanded[2 * L:3 * L]

    xdt = xs * dt_e
    xds_b = (xdt * dst_e).astype(BF16)
    row = lax.broadcasted_iota(jnp.int32, (L, L), 0)
    col = lax.broadcasted_iota(jnp.int32, (L, L), 1)
    causal = row >= col
    lane = lax.broadcasted_iota(jnp.int32, (L, LANES), 1)
    hg = SSD_HEADS // SSD_GROUPS
    gw = hg * SSD_HEAD_DIM
    for g in range(SSD_GROUPS):
        bg = bm[:, g * SSD_STATE:(g + 1) * SSD_STATE]
        cg_b = cm[:, g * SSD_STATE:(g + 1) * SSD_STATE].astype(BF16)
        bg_b = bg.astype(BF16)
        cb = lax.dot_general(cg_b, bg_b, (((1,), (1,)), ((), ())), preferred_element_type=F32)
        hprev = state_ref[g]
        yoff = jnp.dot(cg_b, hprev.astype(BF16), preferred_element_type=F32) * expa_e[:, g * gw:(g + 1) * gw]
        snew = jnp.dot(bg.T.astype(BF16), xds_b[:, g * gw:(g + 1) * gw], preferred_element_type=F32)
        state_ref[g] = hprev * expa_e[L - 1:L, g * gw:(g + 1) * gw] + snew
        for k in range(hg // 2):
            pair = g * (hg // 2) + k
            ms = []
            for hh in (2 * pair, 2 * pair + 1):
                seg = acum[:, hh:hh + 1] - acum_t[hh:hh + 1, :]
                decay = jnp.exp(jnp.where(causal, seg, -jnp.inf))
                ms.append((cb * decay).astype(BF16))
            mpair = jnp.concatenate(ms, axis=1)
            xp = xdt[:, pair * LANES:(pair + 1) * LANES]
            xblk = jnp.concatenate([jnp.where(lane < SSD_HEAD_DIM, xp, 0.0),
                                    jnp.where(lane >= SSD_HEAD_DIM, xp, 0.0)], axis=0).astype(BF16)
            yd = jnp.dot(mpair, xblk, preferred_element_type=F32)
            y_ref[:, pair * LANES:(pair + 1) * LANES] = yd + yoff[:, k * LANES:(k + 1) * LANES]

    yy = y_ref[...] + dskip_ref[...] * xs
    z = z_ref[...]
    yz = yy * _silu(z)
    ms2 = jnp.mean(yz * yz, axis=-1, keepdims=True)
    o_ref[...] = ((yz * lax.rsqrt(ms2 + NORM_EPS)) * nw_ref[...]).astype(o_ref.dtype)


def _ssd(proj, bsz, seq, conv_w, conv_b, dtb_pad, alog_pad, dskip_e, norm_w):
    nc = seq // SSD_CHUNK
    L = SSD_CHUNK
    ltri = jnp.asarray(np.tril(np.ones((L, L), np.float32)))
    ex = np.zeros((LANES, SSD_WIDTH), np.float32)
    for h in range(SSD_HEADS):
        ex[h, h * SSD_HEAD_DIM:(h + 1) * SSD_HEAD_DIM] = 1.0
    ex = jnp.asarray(ex)
    const = lambda shape: pl.BlockSpec(shape, lambda b, c: (0,) * len(shape))
    return pl.pallas_call(
        _ssd_kernel,
        grid=(bsz, nc),
        in_specs=[
            pl.BlockSpec((L, CONV_CH), lambda b, c: (b * nc + c, COL_XBC // CONV_CH)),
            pl.BlockSpec((L, SSD_WIDTH), lambda b, c: (b * nc + c, COL_Z // SSD_WIDTH)),
            pl.BlockSpec((L, LANES), lambda b, c: (b * nc + c, COL_DT // LANES)),
            const((CONV_WIDTH, CONV_CH)), const((1, CONV_CH)), const((1, LANES)), const((1, LANES)),
            const((1, SSD_WIDTH)), const((1, SSD_WIDTH)), const((L, L)), const((LANES, SSD_WIDTH)),
        ],
        out_specs=pl.BlockSpec((L, SSD_WIDTH), lambda b, c: (b * nc + c, 0)),
        out_shape=jax.ShapeDtypeStruct((bsz * seq, SSD_WIDTH), BF16),
        scratch_shapes=[
            pltpu.VMEM((L + 8, CONV_CH), F32),
            pltpu.VMEM((SSD_GROUPS, SSD_STATE, (SSD_HEADS // SSD_GROUPS) * SSD_HEAD_DIM), F32),
            pltpu.VMEM((L, SSD_WIDTH), F32),
        ],
        compiler_params=_cparams(("parallel", "arbitrary")),
        name="ssd",
    )(proj, proj, proj, conv_w, conv_b, dtb_pad, alog_pad, dskip_e, norm_w, ltri, ex)


def _rope(x, cos, s1, s2):
    n = x.shape[-1]
    half = ROPE_DIM // 2
    return x * cos + pltpu.roll(x, n - half, 1) * s1 + pltpu.roll(x, half, 1) * s2


def _kvprep_kernel(ks_ref, vs_ref, kw_ref, vw_ref, cos_ref, s1_ref, s2_ref, sel_ref,
                   kso_ref, vso_ref, kwo_ref, vwo_ref, *, tiles_per_seq):
    tr = ks_ref.shape[0]
    wide = KV_HEADS * LANES
    s0 = (pl.program_id(0) % tiles_per_seq) * tr
    cos, s1, s2 = cos_ref[...], s1_ref[...], s2_ref[...]
    sel = sel_ref[...]
    lane = lax.broadcasted_iota(jnp.int32, (tr, wide), 1) % LANES
    blk = (s0 + lax.broadcasted_iota(jnp.int32, (tr, wide), 0)) // SEL_BLOCK
    ebias = jnp.where(lane - SEL_BLOCK == blk, BLOCK_BIAS, 0.0)
    ones = jnp.where(lane == HEAD_DIM, 1.0, 0.0)

    def place(x):
        return jnp.dot(x.astype(BF16), sel, preferred_element_type=F32)

    kso_ref[...] = (place(_rope(ks_ref[...], cos, s1, s2)) + ebias).astype(BF16)
    kwo_ref[...] = place(_rope(kw_ref[...], cos, s1, s2)).astype(BF16)
    vso_ref[...] = (place(vs_ref[...]) + ones).astype(BF16)
    vwo_ref[...] = (place(vw_ref[...]) + ones).astype(BF16)


def _head_place_matrix():
    m = np.zeros((KV_WIDTH, KV_HEADS * LANES), np.float32)
    for h in range(KV_HEADS):
        for d in range(HEAD_DIM):
            m[h * HEAD_DIM + d, h * LANES + d] = 1.0
    return m


def _kv_prep(proj, seq, cos_t, s1_t, s2_t):
    t = proj.shape[0]
    tr = 512
    tps = seq // tr
    sel = jnp.asarray(_head_place_matrix(), BF16)
    kvb = COL_KV // KV_WIDTH
    seg = lambda k: pl.BlockSpec((tr, KV_WIDTH), lambda i: (i, kvb + k))
    tab = pl.BlockSpec((tr, KV_WIDTH), lambda i: (i % tps, 0))
    wide = KV_HEADS * LANES
    out = pl.BlockSpec((tr, wide), lambda i: (i, 0))
    shp = jax.ShapeDtypeStruct((t, wide), BF16)
    return pl.pallas_call(
        functools.partial(_kvprep_kernel, tiles_per_seq=tps),
        grid=(t // tr,),
        in_specs=[seg(2), seg(3), seg(4), seg(5), tab, tab, tab,
                  pl.BlockSpec((KV_WIDTH, wide), lambda i: (0, 0))],
        out_specs=[out, out, out, out],
        out_shape=[shp, shp, shp, shp],
        compiler_params=_cparams(("parallel",)),
        name="kv_prep",
    )(proj, proj, proj, proj, cos_t, s1_t, s2_t, sel)


def _compress_kernel(gk_ref, gv_ref, pek_ref, pev_ref, w1k_ref, w1v_ref, w2kt_ref, w2v_ref,
                     kblk_ref, vblk_ref, shift_ref):
    ng = gk_ref.shape[2]
    half = CMP_BLOCK // 2 * HEAD_DIM

    def hidden(g_ref, pe_ref, w1_ref):
        g = g_ref[0, 0]
        top = jnp.dot((g + pe_ref[0:1, :]).astype(BF16), w1_ref[0:half, :], preferred_element_type=F32)
        bot = jnp.dot((g + pe_ref[1:2, :]).astype(BF16), w1_ref[half:2 * half, :], preferred_element_type=F32)
        shift_ref[0:ng, :] = bot
        shift_ref[ng:ng + 8, :] = jnp.zeros((8, CMP_HIDDEN), F32)
        return _silu(top + shift_ref[1:ng + 1, :]).astype(BF16)

    hk = hidden(gk_ref, pek_ref, w1k_ref)
    kt = lax.dot_general(w2kt_ref[...], hk, (((1,), (1,)), ((), ())), preferred_element_type=F32)
    kblk_ref[...] = jnp.zeros(kblk_ref.shape, BF16)
    for g in range(ATT_GROUP):
        kblk_ref[0, 0, g * HEAD_DIM:(g + 1) * HEAD_DIM, g * ng:(g + 1) * ng] = kt.astype(BF16)

    hv = hidden(gv_ref, pev_ref, w1v_ref)
    vc = jnp.dot(hv, w2v_ref[...], preferred_element_type=F32)
    vblk_ref[...] = jnp.zeros(vblk_ref.shape, BF16)
    for g in range(ATT_GROUP):
        vblk_ref[0, 0, g * ng:(g + 1) * ng, g * LANES:(g + 1) * LANES] = vc.astype(BF16)


def _compress(gk, gv, pek, pev, w1k, w1v, w2kt, w2v):
    bsz, _, ng, width = gk.shape
    gspec = pl.BlockSpec((1, 1, ng, width), lambda b, h: (b, h, 0, 0))
    const = lambda shape: pl.BlockSpec(shape, lambda b, h: (0,) * len(shape))
    return pl.pallas_call(
        _compress_kernel,
        grid=(bsz, KV_HEADS),
        in_specs=[gspec, gspec, const((2, width)), const((2, width)),
                  const((2 * width, CMP_HIDDEN)), const((2 * width, CMP_HIDDEN)),
                  const((HEAD_DIM, CMP_HIDDEN)), const((CMP_HIDDEN, LANES))],
        out_specs=[pl.BlockSpec((1, 1, KV_WIDTH, ATT_GROUP * ng), lambda b, h: (b, h, 0, 0)),
                   pl.BlockSpec((1, 1, ATT_GROUP * ng, ATT_GROUP * LANES), lambda b, h: (b, h, 0, 0))],
        out_shape=[jax.ShapeDtypeStruct((bsz, KV_HEADS, KV_WIDTH, ATT_GROUP * ng), BF16),
                   jax.ShapeDtypeStruct((bsz, KV_HEADS, ATT_GROUP * ng, ATT_GROUP * LANES), BF16)],
        scratch_shapes=[pltpu.VMEM((ng + 8, CMP_HIDDEN), F32)],
        compiler_params=_cparams(("parallel", "parallel")),
        name="compress",
    )(gk, gv, pek, pev, w1k, w1v, w2kt, w2v)


TQC = TQ
SUBLANES = 8


def _nsacmp_kernel(q_ref, kblk_ref, vblk_ref, cos_ref, s1_ref, s2_ref, ovt_ref, selq_ref, placen_ref, eye_ref,
                   qaug_ref, ocmp_ref):
    ncp = kblk_ref.shape[3] // ATT_GROUP
    nsel = ovt_ref.shape[0]
    nt = (((1,), (1,)), ((), ()))
    s0 = pl.program_id(2) * TQC
    q = q_ref[...]
    s_all = jnp.dot(q.astype(BF16), kblk_ref[0, 0], preferred_element_type=F32) * SCALE
    tq_pos = s0 + lax.broadcasted_iota(jnp.int32, (TQC, ncp), 0)
    cmp_end = lax.broadcasted_iota(jnp.int32, (TQC, ncp), 1) * CMP_STRIDE + (CMP_BLOCK - 1)
    mask = cmp_end <= tq_pos
    maskf = mask.astype(F32)
    ps = []
    for g in range(ATT_GROUP):
        s = jnp.where(mask, s_all[:, g * ncp:(g + 1) * ncp], NEG_INF)
        m = jnp.max(s, axis=-1, keepdims=True)
        e = jnp.exp(s - m)
        ps.append(e / jnp.sum(e, axis=-1, keepdims=True) * maskf)
    p_all = jnp.concatenate(ps, axis=1)

    def stack_rows(wide):
        return jnp.concatenate([wide[u * TQ:(u + 1) * TQ, g * LANES:(g + 1) * LANES]
                                for u in range(TQC // TQ) for g in range(ATT_GROUP)], axis=0)

    ow = jnp.dot(p_all.astype(BF16), vblk_ref[0, 0], preferred_element_type=F32)
    ocmp_ref[0, 0] = stack_rows(ow)

    psum = (ps[0] + ps[1]) + (ps[2] + ps[3])
    p_hi = psum.astype(BF16)
    p_lo = (psum - p_hi.astype(F32)).astype(BF16)
    ovt = ovt_ref[...]
    imp = (lax.dot_general(ovt, p_hi, nt, preferred_element_type=F32)
           + lax.dot_general(ovt, p_lo, nt, preferred_element_type=F32))
    j = lax.broadcasted_iota(jnp.int32, (nsel, TQC), 0)
    cur = (s0 + lax.broadcasted_iota(jnp.int32, (nsel, TQC), 1)) // SEL_BLOCK
    imp = jnp.where((j == 0) | (j == cur) | (j == cur - 1), FORCE_SCORE, imp)
    imp = jnp.where(j <= cur, imp, -1.0)
    nblk = nsel // SUBLANES
    blocks = [imp[k * SUBLANES:(k + 1) * SUBLANES, :] for k in range(nblk)]
    ranks = [jnp.zeros((SUBLANES, TQC), F32) for _ in range(nblk)]
    sub = lax.broadcasted_iota(jnp.int32, (SUBLANES, TQC), 0)
    for jp in range(nsel):
        r = jnp.broadcast_to(imp[jp:jp + 1, :], (SUBLANES, TQC))
        for k in range(nblk):
            if k > jp // SUBLANES:
                inc = jnp.where(r >= blocks[k], 1.0, 0.0)
            elif k < jp // SUBLANES:
                inc = jnp.where(r > blocks[k], 1.0, 0.0)
            else:
                tie = jnp.where(sub > jp % SUBLANES, 1.0, 0.0)
                inc = jnp.where(r > blocks[k], 1.0, jnp.where(r == blocks[k], tie, 0.0))
            ranks[k] = ranks[k] + inc
    rank = jnp.concatenate(ranks, axis=0)
    notsel_t = jnp.where((rank < float(N_SELECT)) & (imp >= 0.0), 0.0, 1.0).astype(BF16)
    notsel = lax.dot_general(eye_ref[...], notsel_t, nt, preferred_element_type=F32)

    q_rot = (_rope(q, cos_ref[...], s1_ref[...], s2_ref[...]) * SCALE).astype(BF16)
    qw = (jnp.dot(q_rot, selq_ref[...], preferred_element_type=F32)
          + jnp.dot(notsel.astype(BF16), placen_ref[...], preferred_element_type=F32))
    qaug_ref[0, 0] = stack_rows(qw).astype(BF16)


def _selection_overlap_t(n_cmp_pad, n_cmp, n_sel):
    cs = np.arange(n_cmp)[:, None] * CMP_STRIDE
    ce = cs + CMP_BLOCK
    ss = np.arange(n_sel)[None, :] * SEL_BLOCK
    se = ss + SEL_BLOCK
    ov = np.clip(np.minimum(ce, se) - np.maximum(cs, ss), 0, None) / CMP_BLOCK
    full = np.zeros((n_cmp_pad, n_sel), np.float32)
    full[:n_cmp] = ov
    return full.T.copy()


def _nsa_cmp(proj, kblk, vblk, bsz, seq, cos_t, s1_t, s2_t):
    nqc = seq // TQC
    ncp = kblk.shape[3] // ATT_GROUP
    n_cmp = (seq - CMP_BLOCK) // CMP_STRIDE + 1
    nsel = seq // SEL_BLOCK
    ovt = jnp.asarray(_selection_overlap_t(ncp, n_cmp, nsel), BF16)
    selq = jnp.asarray(_head_place_matrix(), BF16)
    placen = np.zeros((nsel, ATT_GROUP * LANES), np.float32)
    for g in range(ATT_GROUP):
        for jb in range(nsel):
            placen[jb, g * LANES + HEAD_DIM + jb] = 1.0
    placen = jnp.asarray(placen, BF16)
    eye = jnp.asarray(np.eye(TQC, dtype=np.float32), BF16)
    const = lambda shape: pl.BlockSpec(shape, lambda b, h, t: (0,) * len(shape))
    tab = pl.BlockSpec((TQC, KV_WIDTH), lambda b, h, t: (t, 0))
    qb = COL_Q // KV_WIDTH
    rows = (TQC // TQ) * ROWS
    stacked = pl.BlockSpec((1, 1, rows, LANES), lambda b, h, t: (b, h, t, 0))
    return pl.pallas_call(
        _nsacmp_kernel,
        grid=(bsz, KV_HEADS, nqc),
        in_specs=[
            pl.BlockSpec((TQC, KV_WIDTH), lambda b, h, t: (b * nqc + t, qb + h)),
            pl.BlockSpec((1, 1, KV_WIDTH, ATT_GROUP * ncp), lambda b, h, t: (b, h, 0, 0)),
            pl.BlockSpec((1, 1, ATT_GROUP * ncp, ATT_GROUP * LANES), lambda b, h, t: (b, h, 0, 0)),
            tab, tab, tab,
            const((nsel, ncp)), const((KV_WIDTH, ATT_GROUP * LANES)),
            const((nsel, ATT_GROUP * LANES)), const((TQC, TQC)),
        ],
        out_specs=[stacked, stacked],
        out_shape=[jax.ShapeDtypeStruct((bsz, KV_HEADS, nqc * rows, LANES), BF16),
                   jax.ShapeDtypeStruct((bsz, KV_HEADS, nqc * rows, LANES), F32)],
        compiler_params=_cparams(("parallel", "parallel", "parallel")),
        name="nsa_cmp",
    )(proj, kblk, vblk, cos_t, s1_t, s2_t, ovt, selq, placen, eye)


def _nsaattn_kernel(qaug_ref, ocmp_ref, ks_ref, vs_ref, kw_ref, vw_ref, gate_ref, gsel_ref, o_ref,
                    m_ref, acc_ref):
    qt = pl.program_id(2)
    s0 = qt * TQ
    q = qaug_ref[0, 0]
    nt = (((1,), (1,)), ((), ()))
    q_pos = s0 + lax.broadcasted_iota(jnp.int32, (ROWS, 1), 0) % TQ

    m_ref[...] = jnp.full(m_ref.shape, NEG_INF, F32)
    acc_ref[...] = jnp.zeros(acc_ref.shape, F32)

    def tile(kb, causal):
        start = pl.multiple_of(kb * TK, TK)
        k = ks_ref[pl.ds(start, TK), :]
        v = vs_ref[pl.ds(start, TK), :]
        s = lax.dot_general(q, k, nt, preferred_element_type=F32)
        if causal:
            k_pos = start + lax.broadcasted_iota(jnp.int32, (ROWS, TK), 1)
            s = jnp.where(k_pos <= q_pos, s, NEG_INF)
        m_prev = m_ref[...]
        m_next = jnp.maximum(m_prev, jnp.max(s, axis=-1, keepdims=True))
        p = jnp.exp(s - pltpu.repeat(m_next, TK // LANES, axis=1))
        acc_ref[...] = acc_ref[...] * jnp.exp(m_prev - m_next) + jnp.dot(
            p.astype(BF16), v, preferred_element_type=F32)
        m_ref[...] = m_next

    nfull = (qt * TQ) // TK

    def body(kb, carry):
        tile(kb, False)
        return carry

    lax.fori_loop(0, nfull, body, 0)
    tile(nfull, True)
    acc = acc_ref[...]
    o_sel = acc / acc[:, HEAD_DIM:HEAD_DIM + 1]

    wk = WINDOW + TQ
    wstart = pl.multiple_of(jnp.maximum(s0 - WINDOW, 0), TQ)
    kwin = kw_ref[pl.ds(wstart, wk), :]
    vwin = vw_ref[pl.ds(wstart, wk), :]
    sw = lax.dot_general(q, kwin, nt, preferred_element_type=F32)
    dist = q_pos - (wstart + lax.broadcasted_iota(jnp.int32, (ROWS, wk), 1))
    sw = jnp.where((dist >= 0) & (dist < WINDOW), sw, NEG_INF)
    pw = jnp.exp(sw - jnp.max(sw, axis=-1, keepdims=True))
    ow = jnp.dot(pw.astype(BF16), vwin, preferred_element_type=F32)
    o_win = ow / ow[:, HEAD_DIM:HEAD_DIM + 1]

    sig = jax.nn.sigmoid(gate_ref[...])
    sg = jnp.dot(sig, gsel_ref[0], precision=HI, preferred_element_type=F32)
    ocmp = ocmp_ref[0, 0]
    comb = []
    for g in range(ATT_GROUP):
        r = slice(g * TQ, (g + 1) * TQ)
        c = g * N_BRANCH
        comb.append(sg[:, c:c + 1] * ocmp[r] + sg[:, c + 1:c + 2] * o_sel[r] + sg[:, c + 2:c + 3] * o_win[r])
    lane = lax.broadcasted_iota(jnp.int32, (TQ, LANES), 1)
    pairs = [jnp.where(lane < HEAD_DIM, comb[2 * k], pltpu.roll(comb[2 * k + 1], HEAD_DIM, 1))
             for k in range(ATT_GROUP // 2)]
    o_ref[...] = jnp.concatenate(pairs, axis=1).astype(o_ref.dtype)


def _nsa_attn(qaug, ocmp, ksa, vsa, kwa, vwa, proj, bsz, seq):
    nqt = seq // TQ
    gsel = np.zeros((KV_HEADS, LANES, LANES), np.float32)
    for h in range(KV_HEADS):
        for c in range(ATT_GROUP * N_BRANCH):
            gsel[h, h * ATT_GROUP * N_BRANCH + c, c] = 1.0
    gsel = jnp.asarray(gsel)
    stacked = pl.BlockSpec((1, 1, ROWS, LANES), lambda b, h, t: (b, h, t, 0))
    kv = pl.BlockSpec((seq, LANES), lambda b, h, t: (b, h))
    return pl.pallas_call(
        _nsaattn_kernel,
        grid=(bsz, KV_HEADS, nqt),
        in_specs=[stacked, stacked, kv, kv, kv, kv,
                  pl.BlockSpec((TQ, LANES), lambda b, h, t: (b * nqt + t, COL_GATE // LANES)),
                  pl.BlockSpec((1, LANES, LANES), lambda b, h, t: (h, 0, 0))],
        out_specs=pl.BlockSpec((TQ, KV_WIDTH), lambda b, h, t: (b * nqt + t, h)),
        out_shape=jax.ShapeDtypeStruct((bsz * seq, ATT_WIDTH), BF16),
        scratch_shapes=[pltpu.VMEM((ROWS, LANES), F32), pltpu.VMEM((ROWS, LANES), F32)],
        compiler_params=_cparams(("parallel", "parallel", "arbitrary")),
        name="nsa_attn",
    )(qaug, ocmp, ksa, vsa, kwa, vwa, proj, gsel)


def _outproj_kernel(x_ref, ys_ref, ya_ref, w1_ref, w2_ref, o_ref):
    o_ref[...] = (x_ref[...] + jnp.dot(ys_ref[...], w1_ref[...], preferred_element_type=F32)
                  + jnp.dot(ya_ref[...], w2_ref[...], preferred_element_type=F32))


def _out_proj(x2, ys, ya, wo_b):
    t = x2.shape[0]
    tm = 512
    return pl.pallas_call(
        _outproj_kernel,
        grid=(t // tm,),
        in_specs=[
            pl.BlockSpec((tm, D_MODEL), lambda i: (i, 0)),
            pl.BlockSpec((tm, SSD_WIDTH), lambda i: (i, 0)),
            pl.BlockSpec((tm, ATT_WIDTH), lambda i: (i, 0)),
            pl.BlockSpec((SSD_WIDTH, D_MODEL), lambda i: (0, 0)),
            pl.BlockSpec((ATT_WIDTH, D_MODEL), lambda i: (1, 0)),
        ],
        out_specs=pl.BlockSpec((tm, D_MODEL), lambda i: (i, 0)),
        out_shape=jax.ShapeDtypeStruct((t, D_MODEL), F32),
        compiler_params=_cparams(("parallel",)),
        name="out_proj",
    )(x2, ys, ya, wo_b, wo_b)


def _ffn_kernel(h_ref, nw_ref, fw_ref, wg_ref, wu_ref, wd_ref, o_ref, v_ref, acc_ref):
    j = pl.program_id(1)

    @pl.when(j == 0)
    def _():
        h = h_ref[...]
        ms = jnp.mean(h * h, axis=-1, keepdims=True)
        v_ref[...] = ((h * lax.rsqrt(ms + NORM_EPS)) * nw_ref[...]).astype(BF16)
        acc_ref[...] = jnp.zeros(acc_ref.shape, F32)

    v = v_ref[...]
    gate = jnp.dot(v, wg_ref[...], preferred_element_type=F32)
    up = jnp.dot(v, wu_ref[...], preferred_element_type=F32)
    acc_ref[...] += jnp.dot((_silu(gate) * up).astype(BF16), wd_ref[...], preferred_element_type=F32)

    @pl.when(j == pl.num_programs(1) - 1)
    def _():
        h2 = h_ref[...] + acc_ref[...]
        ms = jnp.mean(h2 * h2, axis=-1, keepdims=True)
        o_ref[...] = (h2 * lax.rsqrt(ms + NORM_EPS)) * fw_ref[...]


def _ffn(h1, ffn_nw, final_w, wg_b, wu_b, wd_b):
    t = h1.shape[0]
    tm, tf = 512, 512
    return pl.pallas_call(
        _ffn_kernel,
        grid=(t // tm, D_FF // tf),
        in_specs=[
            pl.BlockSpec((tm, D_MODEL), lambda i, j: (i, 0)),
            pl.BlockSpec((1, D_MODEL), lambda i, j: (0, 0)),
            pl.BlockSpec((1, D_MODEL), lambda i, j: (0, 0)),
            pl.BlockSpec((D_MODEL, tf), lambda i, j: (0, j)),
            pl.BlockSpec((D_MODEL, tf), lambda i, j: (0, j)),
            pl.BlockSpec((tf, D_MODEL), lambda i, j: (j, 0)),
        ],
        out_specs=pl.BlockSpec((tm, D_MODEL), lambda i, j: (i, 0)),
        out_shape=jax.ShapeDtypeStruct((t, D_MODEL), F32),
        scratch_shapes=[pltpu.VMEM((tm, D_MODEL), BF16), pltpu.VMEM((tm, D_MODEL), F32)],
        compiler_params=_cparams(("parallel", "arbitrary")),
        name="ffn",
    )(h1, ffn_nw, final_w, wg_b, wu_b, wd_b)


def _rope_tables(seq):
    inv = 1.0 / (ROPE_THETA ** (jnp.arange(0, ROPE_DIM, 2, dtype=F32) / ROPE_DIM))
    ang = jnp.arange(seq, dtype=F32)[:, None] * inv[None, :]
    cos, sin = jnp.cos(ang), jnp.sin(ang)
    half = ROPE_DIM // 2
    rest_one = jnp.ones((seq, HEAD_DIM - ROPE_DIM), F32)
    rest_zero = jnp.zeros((seq, HEAD_DIM - ROPE_DIM), F32)
    zero_h = jnp.zeros((seq, half), F32)
    cos_h = jnp.concatenate([cos, cos, rest_one], axis=1)
    s1_h = jnp.concatenate([-sin, zero_h, rest_zero], axis=1)
    s2_h = jnp.concatenate([zero_h, sin, rest_zero], axis=1)
    tile = lambda a: jnp.tile(a, (1, KV_HEADS))
    return tile(cos_h), tile(s1_h), tile(s2_h)


def _pad_lanes(a, width):
    return jnp.pad(a, ((0, 0), (0, width - a.shape[1])))


def _layer(h2d, bsz, seq, p):
    (attn_norm_w, w_in, conv_w, conv_b, dt_bias, a_log, d_skip, ssd_norm_w, cmp_w1_k, cmp_w2_k, cmp_w1_v,
     cmp_w2_v, cmp_pe_k, cmp_pe_v, w_out, ffn_norm_w, w_gate, w_up, w_down) = p
    o_xbc, o_dt, o_q, o_kv, o_gate = 1024, 2560, 2576, 3600, 5136
    w_perm = jnp.concatenate([
        w_in[:, o_xbc:o_dt], w_in[:, o_kv:o_gate], w_in[:, :o_xbc], w_in[:, o_q:o_kv],
        _pad_lanes(w_in[:, o_dt:o_q], LANES), _pad_lanes(w_in[:, o_gate:], LANES)], axis=1).astype(BF16)
    proj = _in_proj(h2d, attn_norm_w[None, :], w_perm)

    y_ssd = _ssd(proj, bsz, seq, conv_w, conv_b[None, :], _pad_lanes(dt_bias[None, :], LANES),
                 _pad_lanes(a_log[None, :], LANES), jnp.repeat(d_skip, SSD_HEAD_DIM)[None, :],
                 ssd_norm_w[None, :])

    cos_t, s1_t, s2_t = _rope_tables(seq)
    ksa, vsa, kwa, vwa = _kv_prep(proj, seq, cos_t, s1_t, s2_t)

    ng = seq // CMP_STRIDE

    def groups(col):
        a = proj[:, col:col + KV_WIDTH].reshape(bsz, ng, CMP_STRIDE, KV_HEADS, HEAD_DIM)
        return a.transpose(0, 3, 1, 2, 4).reshape(bsz, KV_HEADS, ng, CMP_STRIDE * HEAD_DIM)

    pe2 = lambda pe: pe.reshape(2, CMP_STRIDE * HEAD_DIM)
    kblk, vblk = _compress(groups(COL_KV), groups(COL_KV + KV_WIDTH), pe2(cmp_pe_k), pe2(cmp_pe_v),
                           cmp_w1_k.astype(BF16), cmp_w1_v.astype(BF16), cmp_w2_k.T.astype(BF16),
                           _pad_lanes(cmp_w2_v, LANES).astype(BF16))
    qaug, ocmp = _nsa_cmp(proj, kblk, vblk, bsz, seq, cos_t, s1_t, s2_t)
    y_att = _nsa_attn(qaug, ocmp, ksa, vsa, kwa, vwa, proj, bsz, seq)

    h1 = _out_proj(h2d, y_ssd, y_att, w_out.astype(BF16))
    return h1, (ffn_norm_w, w_gate.astype(BF16), w_up.astype(BF16), w_down.astype(BF16))


def kernel(x, attn_norm_w, w_in, conv_w, conv_b, dt_bias, a_log, d_skip, ssd_norm_w, cmp_w1_k, cmp_w2_k,
           cmp_w1_v, cmp_w2_v, cmp_pe_k, cmp_pe_v, w_out, ffn_norm_w, w_gate, w_up, w_down, final_norm_w):
    bsz, seq, _ = x.shape
    depth = w_in.shape[0]
    assert depth == 1, "the final rmsnorm is fused into the last layer's ffn kernel"
    h = x.reshape(bsz * seq, D_MODEL)
    l = 0
    params = (attn_norm_w[l], w_in[l], conv_w[l], conv_b[l], dt_bias[l], a_log[l], d_skip[l], ssd_norm_w[l],
              cmp_w1_k[l], cmp_w2_k[l], cmp_w1_v[l], cmp_w2_v[l], cmp_pe_k[l], cmp_pe_v[l], w_out[l],
              ffn_norm_w[l], w_gate[l], w_up[l], w_down[l])
    h1, (fnw, wg_b, wu_b, wd_b) = _layer(h, bsz, seq, params)
    out = _ffn(h1, fnw[None, :], final_norm_w[None, :], wg_b, wu_b, wd_b)
    return out.reshape(bsz, seq, D_MODEL)
```

```python
import functools

import numpy as np
import jax
import jax.numpy as jnp
from jax import lax
from jax.experimental import pallas as pl
from jax.experimental.pallas import tpu as pltpu

F32 = jnp.float32
BF16 = jnp.bfloat16
HI = lax.Precision.HIGHEST

D_MODEL = 2048
SSD_WIDTH = 1024
ATT_WIDTH = 1024
SSD_HEAD_DIM = 64
SSD_HEADS = 16
SSD_GROUPS = 2
SSD_STATE = 128
SSD_CHUNK = 128
CONV_WIDTH = 4
CONV_CH = SSD_WIDTH + 2 * SSD_GROUPS * SSD_STATE
HEAD_DIM = 64
ATT_HEADS = 16
KV_HEADS = 4
ATT_GROUP = 4
KV_WIDTH = KV_HEADS * HEAD_DIM
CMP_BLOCK = 32
CMP_STRIDE = 16
CMP_HIDDEN = 256
SEL_BLOCK = 64
N_SELECT = 16
WINDOW = 512
N_BRANCH = 3
ROPE_THETA = 500000.0
ROPE_DIM = 16
D_FF = 5632
NORM_EPS = 1e-6
NEG_INF = -1e30
FORCE_SCORE = 1e4
SCALE = HEAD_DIM ** -0.5
BLOCK_BIAS = -(2.0 ** 100)

LANES = 128
VMEM_LIMIT = 56 * 1024 * 1024

NP = 5376
COL_XBC = 0
COL_KV = 1536
COL_Z = 3072
COL_Q = 4096
COL_DT = 5120
COL_GATE = 5248

TQ = 256
TK = 512
ROWS = ATT_GROUP * TQ


def _silu(x):
    return x * jax.nn.sigmoid(x)


def _split_bf16(x, terms):
    out = []
    for _ in range(terms - 1):
        t = x.astype(BF16)
        out.append(t)
        x = x - t.astype(F32)
    out.append(x.astype(BF16))
    return out


def _cparams(sem):
    return pltpu.CompilerParams(dimension_semantics=sem, vmem_limit_bytes=VMEM_LIMIT)


def _inproj_kernel(x_ref, nw_ref, w_ref, o_ref, u_ref):
    @pl.when(pl.program_id(1) == 0)
    def _():
        x = x_ref[...]
        ms = jnp.mean(x * x, axis=-1, keepdims=True)
        u_ref[...] = ((x * lax.rsqrt(ms + NORM_EPS)) * nw_ref[...]).astype(BF16)

    o_ref[...] = jnp.dot(u_ref[...], w_ref[...], preferred_element_type=F32)


def _in_proj(x2, norm_w, w_perm):
    t = x2.shape[0]
    tm, tn = 1024, 768
    return pl.pallas_call(
        _inproj_kernel,
        grid=(t // tm, NP // tn),
        in_specs=[
            pl.BlockSpec((tm, D_MODEL), lambda i, j: (i, 0)),
            pl.BlockSpec((1, D_MODEL), lambda i, j: (0, 0)),
            pl.BlockSpec((D_MODEL, tn), lambda i, j: (0, j)),
        ],
        out_specs=pl.BlockSpec((tm, tn), lambda i, j: (i, j)),
        out_shape=jax.ShapeDtypeStruct((t, NP), F32),
        scratch_shapes=[pltpu.VMEM((tm, D_MODEL), BF16)],
        compiler_params=_cparams(("parallel", "arbitrary")),
        name="in_proj",
    )(x2, norm_w, w_perm)


def _ssd_kernel(xbc_ref, z_ref, dt_ref, cw_ref, cb_ref, dtb_ref, alog_ref, dskip_ref, nw_ref,
                ltri_ref, ex_ref, o_ref, ext_ref, state_ref, y_ref):
    L = SSD_CHUNK
    c = pl.program_id(1)

    @pl.when(c == 0)
    def _():
        ext_ref[0:8, :] = jnp.zeros((8, CONV_CH), F32)
        state_ref[...] = jnp.zeros(state_ref.shape, F32)

    ext_ref[8:8 + L, :] = xbc_ref[...]
    w = cw_ref[...]
    y = (ext_ref[5:5 + L, :] * w[0:1, :] + ext_ref[6:6 + L, :] * w[1:2, :]
         + ext_ref[7:7 + L, :] * w[2:3, :] + ext_ref[8:8 + L, :] * w[3:4, :]) + cb_ref[...]
    tail = ext_ref[L:L + 8, :]
    ext_ref[0:8, :] = tail
    act = _silu(y)
    xs = act[:, :SSD_WIDTH]
    bm = act[:, SSD_WIDTH:SSD_WIDTH + SSD_GROUPS * SSD_STATE]
    cm = act[:, SSD_WIDTH + SSD_GROUPS * SSD_STATE:]

    v = dt_ref[...] + dtb_ref[...]
    dt = jnp.maximum(v, 0.0) + jnp.log1p(jnp.exp(-jnp.abs(v)))
    a = -jnp.exp(alog_ref[...])
    adt = a * dt
    ltri = ltri_ref[...]
    acum = sum(jnp.dot(ltri, t, preferred_element_type=F32) for t in _split_bf16(adt, 3))
    acum_t = acum.T
    last = acum[L - 1:L, :]
    stacked = jnp.concatenate([dt, jnp.exp(acum), jnp.exp(last - acum)], axis=0)
    ex = ex_ref[...]
    expanded = sum(jnp.dot(t, ex, preferred_element_type=F32) for t in _split_bf16(stacked, 2))
    dt_e = expanded[0:L]
    expa_e = expanded[L:2 * L]
    dst_e = expanded[2 * L:3 * L]

    xdt = xs * dt_e
    xds_b = (xdt * dst_e).astype(BF16)
    row = lax.broadcasted_iota(jnp.int32, (L, L), 0)
    col = lax.broadcasted_iota(jnp.int32, (L, L), 1)
    causal = row >= col
    lane = lax.broadcasted_iota(jnp.int32, (L, LANES), 1)
    hg = SSD_HEADS // SSD_GROUPS
    gw = hg * SSD_HEAD_DIM
    for g in range(SSD_GROUPS):
        bg = bm[:, g * SSD_STATE:(g + 1) * SSD_STATE]
        cg_b = cm[:, g * SSD_STATE:(g + 1) * SSD_STATE].astype(BF16)
        bg_b = bg.astype(BF16)
        cb = lax.dot_general(cg_b, bg_b, (((1,), (1,)), ((), ())), preferred_element_type=F32)
        hprev = state_ref[g]
        yoff = jnp.dot(cg_b, hprev.astype(BF16), preferred_element_type=F32) * expa_e[:, g * gw:(g + 1) * gw]
        snew = jnp.dot(bg.T.astype(BF16), xds_b[:, g * gw:(g + 1) * gw], preferred_element_type=F32)
        state_ref[g] = hprev * expa_e[L - 1:L, g * gw:(g + 1) * gw] + snew
        for k in range(hg // 2):
            pair = g * (hg // 2) + k
            ms = []
            for hh in (2 * pair, 2 * pair + 1):
                seg = acum[:, hh:hh + 1] - acum_t[hh:hh + 1, :]
                decay = jnp.exp(jnp.where(causal, seg, -jnp.inf))
                ms.append((cb * decay).astype(BF16))
            mpair = jnp.concatenate(ms, axis=1)
            xp = xdt[:, pair * LANES:(pair + 1) * LANES]
            xblk = jnp.concatenate([jnp.where(lane < SSD_HEAD_DIM, xp, 0.0),
                                    jnp.where(lane >= SSD_HEAD_DIM, xp, 0.0)], axis=0).astype(BF16)
            yd = jnp.dot(mpair, xblk, preferred_element_type=F32)
            y_ref[:, pair * LANES:(pair + 1) * LANES] = yd + yoff[:, k * LANES:(k + 1) * LANES]

    yy = y_ref[...] + dskip_ref[...] * xs
    z = z_ref[...]
    yz = yy * _silu(z)
    ms2 = jnp.mean(yz * yz, axis=-1, keepdims=True)
    o_ref[...] = ((yz * lax.rsqrt(ms2 + NORM_EPS)) * nw_ref[...]).astype(o_ref.dtype)


def _ssd(proj, bsz, seq, conv_w, conv_b, dtb_pad, alog_pad, dskip_e, norm_w):
    nc = seq // SSD_CHUNK
    L = SSD_CHUNK
    ltri = jnp.asarray(np.tril(np.ones((L, L), np.float32)), BF16)
    ex = np.zeros((LANES, SSD_WIDTH), np.float32)
    for h in range(SSD_HEADS):
        ex[h, h * SSD_HEAD_DIM:(h + 1) * SSD_HEAD_DIM] = 1.0
    ex = jnp.asarray(ex, BF16)
    const = lambda shape: pl.BlockSpec(shape, lambda b, c: (0,) * len(shape))
    return pl.pallas_call(
        _ssd_kernel,
        grid=(bsz, nc),
        in_specs=[
            pl.BlockSpec((L, CONV_CH), lambda b, c: (b * nc + c, COL_XBC // CONV_CH)),
            pl.BlockSpec((L, SSD_WIDTH), lambda b, c: (b * nc + c, COL_Z // SSD_WIDTH)),
            pl.BlockSpec((L, LANES), lambda b, c: (b * nc + c, COL_DT // LANES)),
            const((CONV_WIDTH, CONV_CH)), const((1, CONV_CH)), const((1, LANES)), const((1, LANES)),
            const((1, SSD_WIDTH)), const((1, SSD_WIDTH)), const((L, L)), const((LANES, SSD_WIDTH)),
        ],
        out_specs=pl.BlockSpec((L, SSD_WIDTH), lambda b, c: (b * nc + c, 0)),
        out_shape=jax.ShapeDtypeStruct((bsz * seq, SSD_WIDTH), BF16),
        scratch_shapes=[
            pltpu.VMEM((L + 8, CONV_CH), F32),
            pltpu.VMEM((SSD_GROUPS, SSD_STATE, (SSD_HEADS // SSD_GROUPS) * SSD_HEAD_DIM), F32),
            pltpu.VMEM((L, SSD_WIDTH), F32),
        ],
        compiler_params=_cparams(("parallel", "arbitrary")),
        name="ssd",
    )(proj, proj, proj, conv_w, conv_b, dtb_pad, alog_pad, dskip_e, norm_w, ltri, ex)


def _rope(x, cos, s1, s2):
    n = x.shape[-1]
    half = ROPE_DIM // 2
    return x * cos + pltpu.roll(x, n - half, 1) * s1 + pltpu.roll(x, half, 1) * s2


def _kvprep_kernel(ks_ref, vs_ref, kw_ref, vw_ref, cos_ref, s1_ref, s2_ref, sel_ref,
                   kso_ref, vso_ref, kwo_ref, vwo_ref, *, tiles_per_seq):
    tr = ks_ref.shape[0]
    wide = KV_HEADS * LANES
    s0 = (pl.program_id(0) % tiles_per_seq) * tr
    cos, s1, s2 = cos_ref[...], s1_ref[...], s2_ref[...]
    sel = sel_ref[...]
    lane = lax.broadcasted_iota(jnp.int32, (tr, wide), 1) % LANES
    blk = (s0 + lax.broadcasted_iota(jnp.int32, (tr, wide), 0)) // SEL_BLOCK
    ebias = jnp.where(lane - SEL_BLOCK == blk, BLOCK_BIAS, 0.0)
    ones = jnp.where(lane == HEAD_DIM, 1.0, 0.0)

    def place(x):
        return jnp.dot(x.astype(BF16), sel, preferred_element_type=F32)

    kso_ref[...] = (place(_rope(ks_ref[...], cos, s1, s2)) + ebias).astype(BF16)
    kwo_ref[...] = place(_rope(kw_ref[...], cos, s1, s2)).astype(BF16)
    vso_ref[...] = (place(vs_ref[...]) + ones).astype(BF16)
    vwo_ref[...] = (place(vw_ref[...]) + ones).astype(BF16)


def _head_place_matrix():
    m = np.zeros((KV_WIDTH, KV_HEADS * LANES), np.float32)
    for h in range(KV_HEADS):
        for d in range(HEAD_DIM):
            m[h * HEAD_DIM + d, h * LANES + d] = 1.0
    return m


def _kv_prep(proj, seq, cos_t, s1_t, s2_t):
    t = proj.shape[0]
    tr = 512
    tps = seq // tr
    sel = jnp.asarray(_head_place_matrix(), BF16)
    kvb = COL_KV // KV_WIDTH
    seg = lambda k: pl.BlockSpec((tr, KV_WIDTH), lambda i: (i, kvb + k))
    tab = pl.BlockSpec((tr, KV_WIDTH), lambda i: (i % tps, 0))
    wide = KV_HEADS * LANES
    out = pl.BlockSpec((tr, wide), lambda i: (i, 0))
    shp = jax.ShapeDtypeStruct((t, wide), BF16)
    return pl.pallas_call(
        functools.partial(_kvprep_kernel, tiles_per_seq=tps),
        grid=(t // tr,),
        in_specs=[seg(2), seg(3), seg(4), seg(5), tab, tab, tab,
                  pl.BlockSpec((KV_WIDTH, wide), lambda i: (0, 0))],
        out_specs=[out, out, out, out],
        out_shape=[shp, shp, shp, shp],
        compiler_params=_cparams(("parallel",)),
        name="kv_prep",
    )(proj, proj, proj, proj, cos_t, s1_t, s2_t, sel)


def _compress_kernel(gk_ref, gv_ref, pek_ref, pev_ref, w1k_ref, w1v_ref, w2kt_ref, w2v_ref,
                     kblk_ref, vblk_ref, shift_ref):
    ng = gk_ref.shape[2]
    half = CMP_BLOCK // 2 * HEAD_DIM

    def hidden(g_ref, pe_ref, w1_ref):
        g = g_ref[0, 0]
        top = jnp.dot((g + pe_ref[0:1, :]).astype(BF16), w1_ref[0:half, :], preferred_element_type=F32)
        bot = jnp.dot((g + pe_ref[1:2, :]).astype(BF16), w1_ref[half:2 * half, :], preferred_element_type=F32)
        shift_ref[0:ng, :] = bot
        shift_ref[ng:ng + 8, :] = jnp.zeros((8, CMP_HIDDEN), F32)
        return _silu(top + shift_ref[1:ng + 1, :]).astype(BF16)

    hk = hidden(gk_ref, pek_ref, w1k_ref)
    kt = lax.dot_general(w2kt_ref[...], hk, (((1,), (1,)), ((), ())), preferred_element_type=F32)
    kblk_ref[...] = jnp.zeros(kblk_ref.shape, BF16)
    for g in range(ATT_GROUP):
        kblk_ref[0, 0, g * HEAD_DIM:(g + 1) * HEAD_DIM, g * ng:(g + 1) * ng] = kt.astype(BF16)

    hv = hidden(gv_ref, pev_ref, w1v_ref)
    vc = jnp.dot(hv, w2v_ref[...], preferred_element_type=F32)
    vblk_ref[...] = jnp.zeros(vblk_ref.shape, BF16)
    for g in range(ATT_GROUP):
        vblk_ref[0, 0, g * ng:(g + 1) * ng, g * LANES:(g + 1) * LANES] = vc.astype(BF16)


def _compress(gk, gv, pek, pev, w1k, w1v, w2kt, w2v):
    bsz, _, ng, width = gk.shape
    gspec = pl.BlockSpec((1, 1, ng, width), lambda b, h: (b, h, 0, 0))
    const = lambda shape: pl.BlockSpec(shape, lambda b, h: (0,) * len(shape))
    return pl.pallas_call(
        _compress_kernel,
        grid=(bsz, KV_HEADS),
        in_specs=[gspec, gspec, const((2, width)), const((2, width)),
                  const((2 * width, CMP_HIDDEN)), const((2 * width, CMP_HIDDEN)),
                  const((HEAD_DIM, CMP_HIDDEN)), const((CMP_HIDDEN, LANES))],
        out_specs=[pl.BlockSpec((1, 1, KV_WIDTH, ATT_GROUP * ng), lambda b, h: (b, h, 0, 0)),
                   pl.BlockSpec((1, 1, ATT_GROUP * ng, ATT_GROUP * LANES), lambda b, h: (b, h, 0, 0))],
        out_shape=[jax.ShapeDtypeStruct((bsz, KV_HEADS, KV_WIDTH, ATT_GROUP * ng), BF16),
                   jax.ShapeDtypeStruct((bsz, KV_HEADS, ATT_GROUP * ng, ATT_GROUP * LANES), BF16)],
        scratch_shapes=[pltpu.VMEM((ng + 8, CMP_HIDDEN), F32)],
        compiler_params=_cparams(("parallel", "parallel")),
        name="compress",
    )(gk, gv, pek, pev, w1k, w1v, w2kt, w2v)


TQC = TQ
SUBLANES = 8


def _nsacmp_kernel(q_ref, kblk_ref, vblk_ref, cos_ref, s1_ref, s2_ref, ovt_ref, selq_ref, placen_ref, eye_ref,
                   qaug_ref, ocmp_ref):
    ncp = kblk_ref.shape[3] // ATT_GROUP
    nsel = ovt_ref.shape[0]
    nt = (((1,), (1,)), ((), ()))
    s0 = pl.program_id(2) * TQC
    q = q_ref[...]
    s_all = jnp.dot(q.astype(BF16), kblk_ref[0, 0], preferred_element_type=F32) * SCALE
    tq_pos = s0 + lax.broadcasted_iota(jnp.int32, (TQC, ncp), 0)
    cmp_end = lax.broadcasted_iota(jnp.int32, (TQC, ncp), 1) * CMP_STRIDE + (CMP_BLOCK - 1)
    mask = cmp_end <= tq_pos
    maskf = mask.astype(F32)
    ps = []
    for g in range(ATT_GROUP):
        s = jnp.where(mask, s_all[:, g * ncp:(g + 1) * ncp], NEG_INF)
        m = jnp.max(s, axis=-1, keepdims=True)
        e = jnp.exp(s - m)
        ps.append(e / jnp.sum(e, axis=-1, keepdims=True) * maskf)
    p_all = jnp.concatenate(ps, axis=1)

    def stack_rows(wide):
        return jnp.concatenate([wide[u * TQ:(u + 1) * TQ, g * LANES:(g + 1) * LANES]
                                for u in range(TQC // TQ) for g in range(ATT_GROUP)], axis=0)

    ow = jnp.dot(p_all.astype(BF16), vblk_ref[0, 0], preferred_element_type=F32)
    ocmp_ref[0, 0] = stack_rows(ow)

    psum = (ps[0] + ps[1]) + (ps[2] + ps[3])
    p_hi = psum.astype(BF16)
    p_lo = (psum - p_hi.astype(F32)).astype(BF16)
    ovt = ovt_ref[...]
    imp = (lax.dot_general(ovt, p_hi, nt, preferred_element_type=F32)
           + lax.dot_general(ovt, p_lo, nt, preferred_element_type=F32))
    j = lax.broadcasted_iota(jnp.int32, (nsel, TQC), 0)
    cur = (s0 + lax.broadcasted_iota(jnp.int32, (nsel, TQC), 1)) // SEL_BLOCK
    imp = jnp.where((j == 0) | (j == cur) | (j == cur - 1), FORCE_SCORE, imp)
    imp = jnp.where(j <= cur, imp, -1.0)
    nblk = nsel // SUBLANES
    blocks = [imp[k * SUBLANES:(k + 1) * SUBLANES, :] for k in range(nblk)]
    ranks = [jnp.zeros((SUBLANES, TQC), F32) for _ in range(nblk)]
    sub = lax.broadcasted_iota(jnp.int32, (SUBLANES, TQC), 0)
    for jp in range(nsel):
        r = jnp.broadcast_to(imp[jp:jp + 1, :], (SUBLANES, TQC))
        for k in range(nblk):
            if k > jp // SUBLANES:
                inc = jnp.where(r >= blocks[k], 1.0, 0.0)
            elif k < jp // SUBLANES:
                inc = jnp.where(r > blocks[k], 1.0, 0.0)
            else:
                tie = jnp.where(sub > jp % SUBLANES, 1.0, 0.0)
                inc = jnp.where(r > blocks[k], 1.0, jnp.where(r == blocks[k], tie, 0.0))
            ranks[k] = ranks[k] + inc
    rank = jnp.concatenate(ranks, axis=0)
    notsel_t = jnp.where((rank < float(N_SELECT)) & (imp >= 0.0), 0.0, 1.0).astype(BF16)
    notsel = lax.dot_general(eye_ref[...], notsel_t, nt, preferred_element_type=F32)

    q_rot = (_rope(q, cos_ref[...], s1_ref[...], s2_ref[...]) * SCALE).astype(BF16)
    qw = (jnp.dot(q_rot, selq_ref[...], preferred_element_type=F32)
          + jnp.dot(notsel.astype(BF16), placen_ref[...], preferred_element_type=F32))
    qaug_ref[0, 0] = stack_rows(qw).astype(BF16)


def _selection_overlap_t(n_cmp_pad, n_cmp, n_sel):
    cs = np.arange(n_cmp)[:, None] * CMP_STRIDE
    ce = cs + CMP_BLOCK
    ss = np.arange(n_sel)[None, :] * SEL_BLOCK
    se = ss + SEL_BLOCK
    ov = np.clip(np.minimum(ce, se) - np.maximum(cs, ss), 0, None) / CMP_BLOCK
    full = np.zeros((n_cmp_pad, n_sel), np.float32)
    full[:n_cmp] = ov
    return full.T.copy()


def _nsa_cmp(proj, kblk, vblk, bsz, seq, cos_t, s1_t, s2_t):
    nqc = seq // TQC
    ncp = kblk.shape[3] // ATT_GROUP
    n_cmp = (seq - CMP_BLOCK) // CMP_STRIDE + 1
    nsel = seq // SEL_BLOCK
    ovt = jnp.asarray(_selection_overlap_t(ncp, n_cmp, nsel), BF16)
    selq = jnp.asarray(_head_place_matrix(), BF16)
    placen = np.zeros((nsel, ATT_GROUP * LANES), np.float32)
    for g in range(ATT_GROUP):
        for jb in range(nsel):
            placen[jb, g * LANES + HEAD_DIM + jb] = 1.0
    placen = jnp.asarray(placen, BF16)
    eye = jnp.asarray(np.eye(TQC, dtype=np.float32), BF16)
    const = lambda shape: pl.BlockSpec(shape, lambda b, h, t: (0,) * len(shape))
    tab = pl.BlockSpec((TQC, KV_WIDTH), lambda b, h, t: (t, 0))
    qb = COL_Q // KV_WIDTH
    rows = (TQC // TQ) * ROWS
    stacked = pl.BlockSpec((1, 1, rows, LANES), lambda b, h, t: (b, h, t, 0))
    return pl.pallas_call(
        _nsacmp_kernel,
        grid=(bsz, KV_HEADS, nqc),
        in_specs=[
            pl.BlockSpec((TQC, KV_WIDTH), lambda b, h, t: (b * nqc + t, qb + h)),
            pl.BlockSpec((1, 1, KV_WIDTH, ATT_GROUP * ncp), lambda b, h, t: (b, h, 0, 0)),
            pl.BlockSpec((1, 1, ATT_GROUP * ncp, ATT_GROUP * LANES), lambda b, h, t: (b, h, 0, 0)),
            tab, tab, tab,
            const((nsel, ncp)), const((KV_WIDTH, ATT_GROUP * LANES)),
            const((nsel, ATT_GROUP * LANES)), const((TQC, TQC)),
        ],
        out_specs=[stacked, stacked],
        out_shape=[jax.ShapeDtypeStruct((bsz, KV_HEADS, nqc * rows, LANES), BF16),
                   jax.ShapeDtypeStruct((bsz, KV_HEADS, nqc * rows, LANES), F32)],
        compiler_params=_cparams(("parallel", "parallel", "parallel")),
        name="nsa_cmp",
    )(proj, kblk, vblk, cos_t, s1_t, s2_t, ovt, selq, placen, eye)


def _nsaattn_kernel(qaug_ref, ocmp_ref, ks_ref, vs_ref, kw_ref, vw_ref, gate_ref, gsel_ref, db_ref, wb_ref, o_ref,
                    m_ref, acc_ref):
    qt = pl.program_id(2)
    s0 = qt * TQ
    q = qaug_ref[0, 0]
    nt = (((1,), (1,)), ((), ()))

    def per_head(bias):
        return jnp.concatenate([bias] * ATT_GROUP, axis=0)

    m_ref[...] = jnp.full(m_ref.shape, NEG_INF, F32)
    acc_ref[...] = jnp.zeros(acc_ref.shape, F32)

    def tile(kb, causal):
        start = pl.multiple_of(kb * TK, TK)
        k = ks_ref[pl.ds(start, TK), :]
        v = vs_ref[pl.ds(start, TK), :]
        s = lax.dot_general(q, k, nt, preferred_element_type=F32)
        if causal:
            s = s + per_head(db_ref[0])
        m_prev = m_ref[...]
        m_next = jnp.maximum(m_prev, jnp.max(s, axis=-1, keepdims=True))
        p = jnp.exp(s - pltpu.repeat(m_next, TK // LANES, axis=1))
        acc_ref[...] = acc_ref[...] * jnp.exp(m_prev - m_next) + jnp.dot(
            p.astype(BF16), v, preferred_element_type=F32)
        m_ref[...] = m_next

    nfull = (qt * TQ) // TK

    def body(kb, carry):
        tile(kb, False)
        return carry

    lax.fori_loop(0, nfull, body, 0)
    tile(nfull, True)
    acc = acc_ref[...]
    o_sel = acc / acc[:, HEAD_DIM:HEAD_DIM + 1]

    wk = WINDOW + TQ
    wstart = pl.multiple_of(jnp.maximum(s0 - WINDOW, 0), TQ)
    kwin = kw_ref[pl.ds(wstart, wk), :]
    vwin = vw_ref[pl.ds(wstart, wk), :]
    sw = lax.dot_general(q, kwin, nt, preferred_element_type=F32) + per_head(wb_ref[0])
    pw = jnp.exp(sw - jnp.max(sw, axis=-1, keepdims=True))
    ow = jnp.dot(pw.astype(BF16), vwin, preferred_element_type=F32)
    o_win = ow / ow[:, HEAD_DIM:HEAD_DIM + 1]

    sig = jax.nn.sigmoid(gate_ref[...])
    sg = jnp.dot(sig, gsel_ref[0], precision=HI, preferred_element_type=F32)
    ocmp = ocmp_ref[0, 0]
    comb = []
    for g in range(ATT_GROUP):
        r = slice(g * TQ, (g + 1) * TQ)
        c = g * N_BRANCH
        comb.append(sg[:, c:c + 1] * ocmp[r] + sg[:, c + 1:c + 2] * o_sel[r] + sg[:, c + 2:c + 3] * o_win[r])
    lane = lax.broadcasted_iota(jnp.int32, (TQ, LANES), 1)
    pairs = [jnp.where(lane < HEAD_DIM, comb[2 * k], pltpu.roll(comb[2 * k + 1], HEAD_DIM, 1))
             for k in range(ATT_GROUP // 2)]
    o_ref[...] = jnp.concatenate(pairs, axis=1).astype(o_ref.dtype)


def _nsa_attn(qaug, ocmp, ksa, vsa, kwa, vwa, proj, bsz, seq):
    nqt = seq // TQ
    gsel = np.zeros((KV_HEADS, LANES, LANES), np.float32)
    for h in range(KV_HEADS):
        for c in range(ATT_GROUP * N_BRANCH):
            gsel[h, h * ATT_GROUP * N_BRANCH + c, c] = 1.0
    gsel = jnp.asarray(gsel)
    r = np.arange(TQ)[:, None]
    diag_cases = TK // TQ
    db = np.stack([np.where(np.arange(TK)[None, :] <= c * TQ + r, 0.0, NEG_INF) for c in range(diag_cases)])
    win_cases = WINDOW // TQ + 1
    wk = WINDOW + TQ
    dist = lambda c: c * TQ + r - np.arange(wk)[None, :]
    wb = np.stack([np.where((dist(c) >= 0) & (dist(c) < WINDOW), 0.0, NEG_INF) for c in range(win_cases)])
    db = jnp.asarray(db, F32)
    wb = jnp.asarray(wb, F32)
    stacked = pl.BlockSpec((1, 1, ROWS, LANES), lambda b, h, t: (b, h, t, 0))
    kv = pl.BlockSpec((seq, LANES), lambda b, h, t: (b, h))
    return pl.pallas_call(
        _nsaattn_kernel,
        grid=(bsz, KV_HEADS, nqt),
        in_specs=[stacked, stacked, kv, kv, kv, kv,
                  pl.BlockSpec((TQ, LANES), lambda b, h, t: (b * nqt + t, COL_GATE // LANES)),
                  pl.BlockSpec((1, LANES, LANES), lambda b, h, t: (h, 0, 0)),
                  pl.BlockSpec((1, TQ, TK), lambda b, h, t: (t % diag_cases, 0, 0)),
                  pl.BlockSpec((1, TQ, wk), lambda b, h, t: (jnp.minimum(t, win_cases - 1), 0, 0))],
        out_specs=pl.BlockSpec((TQ, KV_WIDTH), lambda b, h, t: (b * nqt + t, h)),
        out_shape=jax.ShapeDtypeStruct((bsz * seq, ATT_WIDTH), BF16),
        scratch_shapes=[pltpu.VMEM((ROWS, LANES), F32), pltpu.VMEM((ROWS, LANES), F32)],
        compiler_params=_cparams(("parallel", "parallel", "arbitrary")),
        name="nsa_attn",
    )(qaug, ocmp, ksa, vsa, kwa, vwa, proj, gsel, db, wb)


def _outproj_kernel(x_ref, ys_ref, ya_ref, w1_ref, w2_ref, o_ref):
    o_ref[...] = (x_ref[...] + jnp.dot(ys_ref[...], w1_ref[...], preferred_element_type=F32)
                  + jnp.dot(ya_ref[...], w2_ref[...], preferred_element_type=F32))


def _out_proj(x2, ys, ya, wo_b):
    t = x2.shape[0]
    tm = 512
    return pl.pallas_call(
        _outproj_kernel,
        grid=(t // tm,),
        in_specs=[
            pl.BlockSpec((tm, D_MODEL), lambda i: (i, 0)),
            pl.BlockSpec((tm, SSD_WIDTH), lambda i: (i, 0)),
            pl.BlockSpec((tm, ATT_WIDTH), lambda i: (i, 0)),
            pl.BlockSpec((SSD_WIDTH, D_MODEL), lambda i: (0, 0)),
            pl.BlockSpec((ATT_WIDTH, D_MODEL), lambda i: (1, 0)),
        ],
        out_specs=pl.BlockSpec((tm, D_MODEL), lambda i: (i, 0)),
        out_shape=jax.ShapeDtypeStruct((t, D_MODEL), F32),
        compiler_params=_cparams(("parallel",)),
        name="out_proj",
    )(x2, ys, ya, wo_b, wo_b)


def _ffn_kernel(h_ref, nw_ref, fw_ref, wg_ref, wu_ref, wd_ref, o_ref, v_ref, acc_ref):
    j = pl.program_id(1)

    @pl.when(j == 0)
    def _():
        h = h_ref[...]
        ms = jnp.mean(h * h, axis=-1, keepdims=True)
        v_ref[...] = ((h * lax.rsqrt(ms + NORM_EPS)) * nw_ref[...]).astype(BF16)
        acc_ref[...] = jnp.zeros(acc_ref.shape, F32)

    v = v_ref[...]
    gate = jnp.dot(v, wg_ref[...], preferred_element_type=F32)
    up = jnp.dot(v, wu_ref[...], preferred_element_type=F32)
    acc_ref[...] += jnp.dot((_silu(gate) * up).astype(BF16), wd_ref[...], preferred_element_type=F32)

    @pl.when(j == pl.num_programs(1) - 1)
    def _():
        h2 = h_ref[...] + acc_ref[...]
        ms = jnp.mean(h2 * h2, axis=-1, keepdims=True)
        o_ref[...] = (h2 * lax.rsqrt(ms + NORM_EPS)) * fw_ref[...]


def _ffn(h1, ffn_nw, final_w, wg_b, wu_b, wd_b):
    t = h1.shape[0]
    tm, tf = 512, 512
    return pl.pallas_call(
        _ffn_kernel,
        grid=(t // tm, D_FF // tf),
        in_specs=[
            pl.BlockSpec((tm, D_MODEL), lambda i, j: (i, 0)),
            pl.BlockSpec((1, D_MODEL), lambda i, j: (0, 0)),
            pl.BlockSpec((1, D_MODEL), lambda i, j: (0, 0)),
            pl.BlockSpec((D_MODEL, tf), lambda i, j: (0, j)),
            pl.BlockSpec((D_MODEL, tf), lambda i, j: (0, j)),
            pl.BlockSpec((tf, D_MODEL), lambda i, j: (j, 0)),
        ],
        out_specs=pl.BlockSpec((tm, D_MODEL), lambda i, j: (i, 0)),
        out_shape=jax.ShapeDtypeStruct((t, D_MODEL), F32),
        scratch_shapes=[pltpu.VMEM((tm, D_MODEL), BF16), pltpu.VMEM((tm, D_MODEL), F32)],
        compiler_params=_cparams(("parallel", "arbitrary")),
        name="ffn",
    )(h1, ffn_nw, final_w, wg_b, wu_b, wd_b)


def _rope_tables(seq):
    inv = 1.0 / (ROPE_THETA ** (jnp.arange(0, ROPE_DIM, 2, dtype=F32) / ROPE_DIM))
    ang = jnp.arange(seq, dtype=F32)[:, None] * inv[None, :]
    cos, sin = jnp.cos(ang), jnp.sin(ang)
    half = ROPE_DIM // 2
    rest_one = jnp.ones((seq, HEAD_DIM - ROPE_DIM), F32)
    rest_zero = jnp.zeros((seq, HEAD_DIM - ROPE_DIM), F32)
    zero_h = jnp.zeros((seq, half), F32)
    cos_h = jnp.concatenate([cos, cos, rest_one], axis=1)
    s1_h = jnp.concatenate([-sin, zero_h, rest_zero], axis=1)
    s2_h = jnp.concatenate([zero_h, sin, rest_zero], axis=1)
    tile = lambda a: jnp.tile(a, (1, KV_HEADS))
    return tile(cos_h), tile(s1_h), tile(s2_h)


def _pad_lanes(a, width):
    return jnp.pad(a, ((0, 0), (0, width - a.shape[1])))


def _layer(h2d, bsz, seq, p):
    (attn_norm_w, w_in, conv_w, conv_b, dt_bias, a_log, d_skip, ssd_norm_w, cmp_w1_k, cmp_w2_k, cmp_w1_v,
     cmp_w2_v, cmp_pe_k, cmp_pe_v, w_out, ffn_norm_w, w_gate, w_up, w_down) = p
    o_xbc, o_dt, o_q, o_kv, o_gate = 1024, 2560, 2576, 3600, 5136
    w_perm = jnp.concatenate([
        w_in[:, o_xbc:o_dt], w_in[:, o_kv:o_gate], w_in[:, :o_xbc], w_in[:, o_q:o_kv],
        _pad_lanes(w_in[:, o_dt:o_q], LANES), _pad_lanes(w_in[:, o_gate:], LANES)], axis=1).astype(BF16)
    proj = _in_proj(h2d, attn_norm_w[None, :], w_perm)

    y_ssd = _ssd(proj, bsz, seq, conv_w, conv_b[None, :], _pad_lanes(dt_bias[None, :], LANES),
                 _pad_lanes(a_log[None, :], LANES), jnp.repeat(d_skip, SSD_HEAD_DIM)[None, :],
                 ssd_norm_w[None, :])

    cos_t, s1_t, s2_t = _rope_tables(seq)
    ksa, vsa, kwa, vwa = _kv_prep(proj, seq, cos_t, s1_t, s2_t)

    ng = seq // CMP_STRIDE

    def groups(col):
        a = proj[:, col:col + KV_WIDTH].reshape(bsz, ng, CMP_STRIDE, KV_HEADS, HEAD_DIM)
        return a.transpose(0, 3, 1, 2, 4).reshape(bsz, KV_HEADS, ng, CMP_STRIDE * HEAD_DIM)

    pe2 = lambda pe: pe.reshape(2, CMP_STRIDE * HEAD_DIM)
    kblk, vblk = _compress(groups(COL_KV), groups(COL_KV + KV_WIDTH), pe2(cmp_pe_k), pe2(cmp_pe_v),
                           cmp_w1_k.astype(BF16), cmp_w1_v.astype(BF16), cmp_w2_k.T.astype(BF16),
                           _pad_lanes(cmp_w2_v, LANES).astype(BF16))
    qaug, ocmp = _nsa_cmp(proj, kblk, vblk, bsz, seq, cos_t, s1_t, s2_t)
    y_att = _nsa_attn(qaug, ocmp, ksa, vsa, kwa, vwa, proj, bsz, seq)

    h1 = _out_proj(h2d, y_ssd, y_att, w_out.astype(BF16))
    return h1, (ffn_norm_w, w_gate.astype(BF16), w_up.astype(BF16), w_down.astype(BF16))


def kernel(x, attn_norm_w, w_in, conv_w, conv_b, dt_bias, a_log, d_skip, ssd_norm_w, cmp_w1_k, cmp_w2_k,
           cmp_w1_v, cmp_w2_v, cmp_pe_k, cmp_pe_v, w_out, ffn_norm_w, w_gate, w_up, w_down, final_norm_w):
    bsz, seq, _ = x.shape
    depth = w_in.shape[0]
    assert depth == 1, "the final rmsnorm is fused into the last layer's ffn kernel"
    h = x.reshape(bsz * seq, D_MODEL)
    l = 0
    params = (attn_norm_w[l], w_in[l], conv_w[l], conv_b[l], dt_bias[l], a_log[l], d_skip[l], ssd_norm_w[l],
              cmp_w1_k[l], cmp_w2_k[l], cmp_w1_v[l], cmp_w2_v[l], cmp_pe_k[l], cmp_pe_v[l], w_out[l],
              ffn_norm_w[l], w_gate[l], w_up[l], w_down[l])
    h1, (fnw, wg_b, wu_b, wd_b) = _layer(h, bsz, seq, params)
    out = _ffn(h1, fnw[None, :], final_norm_w[None, :], wg_b, wu_b, wd_b)
    return out.reshape(bsz, seq, D_MODEL)
```

```python
import functools

import numpy as np
import jax
import jax.numpy as jnp
from jax import lax
from jax.experimental import pallas as pl
from jax.experimental.pallas import tpu as pltpu

F32 = jnp.float32
BF16 = jnp.bfloat16
HI = lax.Precision.HIGHEST

D_MODEL = 2048
SSD_WIDTH = 1024
ATT_WIDTH = 1024
SSD_HEAD_DIM = 64
SSD_HEADS = 16
SSD_GROUPS = 2
SSD_STATE = 128
SSD_CHUNK = 128
CONV_WIDTH = 4
CONV_CH = SSD_WIDTH + 2 * SSD_GROUPS * SSD_STATE
HEAD_DIM = 64
ATT_HEADS = 16
KV_HEADS = 4
ATT_GROUP = 4
KV_WIDTH = KV_HEADS * HEAD_DIM
CMP_BLOCK = 32
CMP_STRIDE = 16
CMP_HIDDEN = 256
SEL_BLOCK = 64
N_SELECT = 16
WINDOW = 512
N_BRANCH = 3
ROPE_THETA = 500000.0
ROPE_DIM = 16
D_FF = 5632
NORM_EPS = 1e-6
NEG_INF = -1e30
FORCE_SCORE = 1e4
SCALE = HEAD_DIM ** -0.5
BLOCK_BIAS = -(2.0 ** 100)

LANES = 128
VMEM_LIMIT = 56 * 1024 * 1024

NP = 5376
COL_XBC = 0
COL_KV = 1536
COL_Z = 3072
COL_Q = 4096
COL_DT = 5120
COL_GATE = 5248

TQ = 256
TK = 512
ROWS = ATT_GROUP * TQ


def _silu(x):
    return x * jax.nn.sigmoid(x)


def _split_bf16(x, terms):
    out = []
    for _ in range(terms - 1):
        t = x.astype(BF16)
        out.append(t)
        x = x - t.astype(F32)
    out.append(x.astype(BF16))
    return out


def _cparams(sem):
    return pltpu.CompilerParams(dimension_semantics=sem, vmem_limit_bytes=VMEM_LIMIT)


def _inproj_kernel(x_ref, nw_ref, w_ref, o_ref, u_ref):
    @pl.when(pl.program_id(1) == 0)
    def _():
        x = x_ref[...]
        ms = jnp.mean(x * x, axis=-1, keepdims=True)
        u_ref[...] = ((x * lax.rsqrt(ms + NORM_EPS)) * nw_ref[...]).astype(BF16)

    o_ref[...] = jnp.dot(u_ref[...], w_ref[...], preferred_element_type=F32)


def _in_proj(x2, norm_w, w_perm):
    t = x2.shape[0]
    tm, tn = 1024, 768
    return pl.pallas_call(
        _inproj_kernel,
        grid=(t // tm, NP // tn),
        in_specs=[
            pl.BlockSpec((tm, D_MODEL), lambda i, j: (i, 0)),
            pl.BlockSpec((1, D_MODEL), lambda i, j: (0, 0)),
            pl.BlockSpec((D_MODEL, tn), lambda i, j: (0, j)),
        ],
        out_specs=pl.BlockSpec((tm, tn), lambda i, j: (i, j)),
        out_shape=jax.ShapeDtypeStruct((t, NP), F32),
        scratch_shapes=[pltpu.VMEM((tm, D_MODEL), BF16)],
        compiler_params=_cparams(("parallel", "arbitrary")),
        name="in_proj",
    )(x2, norm_w, w_perm)


def _ssd_kernel(xbc_ref, z_ref, dt_ref, cw_ref, cb_ref, dtb_ref, alog_ref, dskip_ref, nw_ref,
                ltri_ref, ex_ref, o_ref, ext_ref, state_ref, y_ref):
    L = SSD_CHUNK
    c = pl.program_id(1)

    @pl.when(c == 0)
    def _():
        ext_ref[0:8, :] = jnp.zeros((8, CONV_CH), F32)
        state_ref[...] = jnp.zeros(state_ref.shape, F32)

    ext_ref[8:8 + L, :] = xbc_ref[...]
    w = cw_ref[...]
    y = (ext_ref[5:5 + L, :] * w[0:1, :] + ext_ref[6:6 + L, :] * w[1:2, :]
         + ext_ref[7:7 + L, :] * w[2:3, :] + ext_ref[8:8 + L, :] * w[3:4, :]) + cb_ref[...]
    tail = ext_ref[L:L + 8, :]
    ext_ref[0:8, :] = tail
    act = _silu(y)
    xs = act[:, :SSD_WIDTH]
    bm = act[:, SSD_WIDTH:SSD_WIDTH + SSD_GROUPS * SSD_STATE]
    cm = act[:, SSD_WIDTH + SSD_GROUPS * SSD_STATE:]

    v = dt_ref[...] + dtb_ref[...]
    dt = jnp.maximum(v, 0.0) + jnp.log1p(jnp.exp(-jnp.abs(v)))
    a = -jnp.exp(alog_ref[...])
    adt = a * dt
    ltri = ltri_ref[...]
    acum = sum(jnp.dot(ltri, t, preferred_element_type=F32) for t in _split_bf16(adt, 3))
    acum_t = acum.T
    last = acum[L - 1:L, :]
    stacked = jnp.concatenate([dt, jnp.exp(acum), jnp.exp(last - acum)], axis=0)
    ex = ex_ref[...]
    expanded = sum(jnp.dot(t, ex, preferred_element_type=F32) for t in _split_bf16(stacked, 2))
    dt_e = expanded[0:L]
    expa_e = expanded[L:2 * L]
    dst_e = expanded[2 * L:3 * L]

    xdt = xs * dt_e
    xds_b = (xdt * dst_e).astype(BF16)
    row = lax.broadcasted_iota(jnp.int32, (L, L), 0)
    col = lax.broadcasted_iota(jnp.int32, (L, L), 1)
    causal = row >= col
    lane = lax.broadcasted_iota(jnp.int32, (L, LANES), 1)
    hg = SSD_HEADS // SSD_GROUPS
    gw = hg * SSD_HEAD_DIM
    for g in range(SSD_GROUPS):
        bg = bm[:, g * SSD_STATE:(g + 1) * SSD_STATE]
        cg_b = cm[:, g * SSD_STATE:(g + 1) * SSD_STATE].astype(BF16)
        bg_b = bg.astype(BF16)
        cb = lax.dot_general(cg_b, bg_b, (((1,), (1,)), ((), ())), preferred_element_type=F32)
        hprev = state_ref[g]
        yoff = jnp.dot(cg_b, hprev.astype(BF16), preferred_element_type=F32) * expa_e[:, g * gw:(g + 1) * gw]
        snew = jnp.dot(bg.T.astype(BF16), xds_b[:, g * gw:(g + 1) * gw], preferred_element_type=F32)
        state_ref[g] = hprev * expa_e[L - 1:L, g * gw:(g + 1) * gw] + snew
        for k in range(hg // 2):
            pair = g * (hg // 2) + k
            ms = []
            for hh in (2 * pair, 2 * pair + 1):
                seg = acum[:, hh:hh + 1] - acum_t[hh:hh + 1, :]
                decay = jnp.exp(jnp.where(causal, seg, -jnp.inf))
                ms.append((cb * decay).astype(BF16))
            mpair = jnp.concatenate(ms, axis=1)
            xp = xdt[:, pair * LANES:(pair + 1) * LANES]
            xblk = jnp.concatenate([jnp.where(lane < SSD_HEAD_DIM, xp, 0.0),
                                    jnp.where(lane >= SSD_HEAD_DIM, xp, 0.0)], axis=0).astype(BF16)
            yd = jnp.dot(mpair, xblk, preferred_element_type=F32)
            y_ref[:, pair * LANES:(pair + 1) * LANES] = yd + yoff[:, k * LANES:(k + 1) * LANES]

    yy = y_ref[...] + dskip_ref[...] * xs
    z = z_ref[...]
    yz = yy * _silu(z)
    ms2 = jnp.mean(yz * yz, axis=-1, keepdims=True)
    o_ref[...] = ((yz * lax.rsqrt(ms2 + NORM_EPS)) * nw_ref[...]).astype(o_ref.dtype)


def _ssd(proj, bsz, seq, conv_w, conv_b, dtb_pad, alog_pad, dskip_e, norm_w):
    nc = seq // SSD_CHUNK
    L = SSD_CHUNK
    ltri = jnp.asarray(np.tril(np.ones((L, L), np.float32)), BF16)
    ex = np.zeros((LANES, SSD_WIDTH), np.float32)
    for h in range(SSD_HEADS):
        ex[h, h * SSD_HEAD_DIM:(h + 1) * SSD_HEAD_DIM] = 1.0
    ex = jnp.asarray(ex, BF16)
    const = lambda shape: pl.BlockSpec(shape, lambda b, c: (0,) * len(shape))
    return pl.pallas_call(
        _ssd_kernel,
        grid=(bsz, nc),
        in_specs=[
            pl.BlockSpec((L, CONV_CH), lambda b, c: (b * nc + c, COL_XBC // CONV_CH)),
            pl.BlockSpec((L, SSD_WIDTH), lambda b, c: (b * nc + c, COL_Z // SSD_WIDTH)),
            pl.BlockSpec((L, LANES), lambda b, c: (b * nc + c, COL_DT // LANES)),
            const((CONV_WIDTH, CONV_CH)), const((1, CONV_CH)), const((1, LANES)), const((1, LANES)),
            const((1, SSD_WIDTH)), const((1, SSD_WIDTH)), const((L, L)), const((LANES, SSD_WIDTH)),
        ],
        out_specs=pl.BlockSpec((L, SSD_WIDTH), lambda b, c: (b * nc + c, 0)),
        out_shape=jax.ShapeDtypeStruct((bsz * seq, SSD_WIDTH), BF16),
        scratch_shapes=[
            pltpu.VMEM((L + 8, CONV_CH), F32),
            pltpu.VMEM((SSD_GROUPS, SSD_STATE, (SSD_HEADS // SSD_GROUPS) * SSD_HEAD_DIM), F32),
            pltpu.VMEM((L, SSD_WIDTH), F32),
        ],
        compiler_params=_cparams(("parallel", "arbitrary")),
        name="ssd",
    )(proj, proj, proj, conv_w, conv_b, dtb_pad, alog_pad, dskip_e, norm_w, ltri, ex)


def _rope(x, cos, s1, s2):
    n = x.shape[-1]
    half = ROPE_DIM // 2
    return x * cos + pltpu.roll(x, n - half, 1) * s1 + pltpu.roll(x, half, 1) * s2


def _merge_heads(parts, upper=False):
    lane = lax.broadcasted_iota(jnp.int32, parts[0].shape, 1)
    out = []
    for k in range(len(parts) // 2):
        a, b = parts[2 * k], parts[2 * k + 1]
        if upper:
            out.append(jnp.where(lane < HEAD_DIM, pltpu.roll(a, HEAD_DIM, 1), b))
        else:
            out.append(jnp.where(lane < HEAD_DIM, a, pltpu.roll(b, HEAD_DIM, 1)))
    return jnp.concatenate(out, axis=1)


def _kvprep_kernel(ks_ref, vs_ref, kw_ref, vw_ref, cos_ref, s1_ref, s2_ref, sel_ref,
                   kso_ref, vso_ref, kwo_ref, vwo_ref, *, tiles_per_seq):
    tr = ks_ref.shape[0]
    wide = KV_HEADS * LANES
    s0 = (pl.program_id(0) % tiles_per_seq) * tr
    cos, s1, s2 = cos_ref[...], s1_ref[...], s2_ref[...]
    sel = sel_ref[...]
    lane = lax.broadcasted_iota(jnp.int32, (tr, wide), 1) % LANES
    blk = (s0 + lax.broadcasted_iota(jnp.int32, (tr, wide), 0)) // SEL_BLOCK
    ebias = jnp.where(lane - SEL_BLOCK == blk, BLOCK_BIAS, 0.0)
    ones = jnp.where(lane >= HEAD_DIM, 1.0, 0.0)

    def place(x):
        return jnp.dot(x.astype(BF16), sel, preferred_element_type=F32)

    kso_ref[...] = (place(_rope(ks_ref[...], cos, s1, s2)) + ebias).astype(BF16)
    kwo_ref[...] = place(_rope(kw_ref[...], cos, s1, s2)).astype(BF16)
    vso_ref[...] = (place(vs_ref[...]) + ones).astype(BF16)
    vwo_ref[...] = (place(vw_ref[...]) + ones).astype(BF16)


def _head_place_matrix():
    m = np.zeros((KV_WIDTH, KV_HEADS * LANES), np.float32)
    for h in range(KV_HEADS):
        for d in range(HEAD_DIM):
            m[h * HEAD_DIM + d, h * LANES + d] = 1.0
    return m


def _kv_prep(proj, seq, cos_t, s1_t, s2_t):
    t = proj.shape[0]
    tr = 512
    tps = seq // tr
    sel = jnp.asarray(_head_place_matrix(), BF16)
    kvb = COL_KV // KV_WIDTH
    seg = lambda k: pl.BlockSpec((tr, KV_WIDTH), lambda i: (i, kvb + k))
    tab = pl.BlockSpec((tr, KV_WIDTH), lambda i: (i % tps, 0))
    wide = KV_HEADS * LANES
    out = pl.BlockSpec((tr, wide), lambda i: (i, 0))
    shp = jax.ShapeDtypeStruct((t, wide), BF16)
    return pl.pallas_call(
        functools.partial(_kvprep_kernel, tiles_per_seq=tps),
        grid=(t // tr,),
        in_specs=[seg(2), seg(3), seg(4), seg(5), tab, tab, tab,
                  pl.BlockSpec((KV_WIDTH, wide), lambda i: (0, 0))],
        out_specs=[out, out, out, out],
        out_shape=[shp, shp, shp, shp],
        compiler_params=_cparams(("parallel",)),
        name="kv_prep",
    )(proj, proj, proj, proj, cos_t, s1_t, s2_t, sel)


def _compress_kernel(gk_ref, gv_ref, pek_ref, pev_ref, w1k_ref, w1v_ref, w2kt_ref, w2v_ref,
                     kblk_ref, vblk_ref, shift_ref):
    ng = gk_ref.shape[0] // CMP_STRIDE
    heads_per_block = LANES // HEAD_DIM
    mine = (lax.broadcasted_iota(jnp.int32, (ng, LANES), 1) // HEAD_DIM) == pl.program_id(1) % heads_per_block

    def hidden(x_ref, pe_ref, w1_ref):
        top = jnp.zeros((ng, CMP_HIDDEN), F32)
        bot = jnp.zeros((ng, CMP_HIDDEN), F32)
        for r in range(CMP_STRIDE):
            x = x_ref[pl.ds(r, ng, stride=CMP_STRIDE), :]
            xt = jnp.where(mine, x + pe_ref[r:r + 1, :], 0.0).astype(BF16)
            xb = jnp.where(mine, x + pe_ref[CMP_STRIDE + r:CMP_STRIDE + r + 1, :], 0.0).astype(BF16)
            top = top + jnp.dot(xt, w1_ref[r], preferred_element_type=F32)
            bot = bot + jnp.dot(xb, w1_ref[CMP_STRIDE + r], preferred_element_type=F32)
        shift_ref[0:ng, :] = bot
        shift_ref[ng:ng + 8, :] = jnp.zeros((8, CMP_HIDDEN), F32)
        return _silu(top + shift_ref[1:ng + 1, :]).astype(BF16)

    hk = hidden(gk_ref, pek_ref, w1k_ref)
    kt = lax.dot_general(w2kt_ref[...], hk, (((1,), (1,)), ((), ())), preferred_element_type=F32)
    kblk_ref[...] = jnp.zeros(kblk_ref.shape, BF16)
    for g in range(ATT_GROUP):
        kblk_ref[0, 0, g * HEAD_DIM:(g + 1) * HEAD_DIM, g * ng:(g + 1) * ng] = kt.astype(BF16)

    hv = hidden(gv_ref, pev_ref, w1v_ref)
    vc = jnp.dot(hv, w2v_ref[...], preferred_element_type=F32)
    vblk_ref[...] = jnp.zeros(vblk_ref.shape, BF16)
    for g in range(ATT_GROUP):
        vblk_ref[0, 0, g * ng:(g + 1) * ng, g * LANES:(g + 1) * LANES] = vc.astype(BF16)


def _compress(proj, bsz, seq, pek, pev, w1k, w1v, w2kt, w2v):
    ng = seq // CMP_STRIDE
    hpb = LANES // HEAD_DIM
    kcb = COL_KV // LANES
    vcb = kcb + KV_WIDTH // LANES
    const = lambda shape: pl.BlockSpec(shape, lambda b, h: (0,) * len(shape))
    return pl.pallas_call(
        _compress_kernel,
        grid=(bsz, KV_HEADS),
        in_specs=[pl.BlockSpec((seq, LANES), lambda b, h: (b, kcb + h // hpb)),
                  pl.BlockSpec((seq, LANES), lambda b, h: (b, vcb + h // hpb)),
                  const((CMP_BLOCK, LANES)), const((CMP_BLOCK, LANES)),
                  const((CMP_BLOCK, LANES, CMP_HIDDEN)), const((CMP_BLOCK, LANES, CMP_HIDDEN)),
                  const((HEAD_DIM, CMP_HIDDEN)), const((CMP_HIDDEN, LANES))],
        out_specs=[pl.BlockSpec((1, 1, KV_WIDTH, ATT_GROUP * ng), lambda b, h: (b, h, 0, 0)),
                   pl.BlockSpec((1, 1, ATT_GROUP * ng, ATT_GROUP * LANES), lambda b, h: (b, h, 0, 0))],
        out_shape=[jax.ShapeDtypeStruct((bsz, KV_HEADS, KV_WIDTH, ATT_GROUP * ng), BF16),
                   jax.ShapeDtypeStruct((bsz, KV_HEADS, ATT_GROUP * ng, ATT_GROUP * LANES), BF16)],
        scratch_shapes=[pltpu.VMEM((ng + 8, CMP_HIDDEN), F32)],
        compiler_params=_cparams(("parallel", "parallel")),
        name="compress",
    )(proj, proj, pek, pev, w1k, w1v, w2kt, w2v)


TQC = TQ
SUBLANES = 8


def _nsacmp_kernel(q_ref, kblk_ref, vblk_ref, cos_ref, s1_ref, s2_ref, ovt_ref, selq_ref, placen_ref, eye_ref,
                   qaug_ref, ocmp_ref):
    ncp = kblk_ref.shape[3] // ATT_GROUP
    nsel = ovt_ref.shape[0]
    nt = (((1,), (1,)), ((), ()))
    s0 = pl.program_id(2) * TQC
    q = q_ref[...]
    s_all = jnp.dot(q.astype(BF16), kblk_ref[0, 0], preferred_element_type=F32) * SCALE
    tq_pos = s0 + lax.broadcasted_iota(jnp.int32, (TQC, ncp), 0)
    cmp_end = lax.broadcasted_iota(jnp.int32, (TQC, ncp), 1) * CMP_STRIDE + (CMP_BLOCK - 1)
    mask = cmp_end <= tq_pos
    maskf = mask.astype(F32)
    ps = []
    for g in range(ATT_GROUP):
        s = jnp.where(mask, s_all[:, g * ncp:(g + 1) * ncp], NEG_INF)
        m = jnp.max(s, axis=-1, keepdims=True)
        e = jnp.exp(s - m)
        ps.append(e / jnp.sum(e, axis=-1, keepdims=True) * maskf)
    p_all = jnp.concatenate(ps, axis=1)

    def stack_rows(wide):
        return jnp.concatenate([wide[u * TQ:(u + 1) * TQ, g * LANES:(g + 1) * LANES]
                                for u in range(TQC // TQ) for g in range(ATT_GROUP)], axis=0)

    ow = jnp.dot(p_all.astype(BF16), vblk_ref[0, 0], preferred_element_type=F32)
    ocmp_ref[...] = _merge_heads([ow[:, g * LANES:(g + 1) * LANES] for g in range(ATT_GROUP)])

    psum = (ps[0] + ps[1]) + (ps[2] + ps[3])
    p_hi = psum.astype(BF16)
    p_lo = (psum - p_hi.astype(F32)).astype(BF16)
    ovt = ovt_ref[...]
    imp = (lax.dot_general(ovt, p_hi, nt, preferred_element_type=F32)
           + lax.dot_general(ovt, p_lo, nt, preferred_element_type=F32))
    j = lax.broadcasted_iota(jnp.int32, (nsel, TQC), 0)
    cur = (s0 + lax.broadcasted_iota(jnp.int32, (nsel, TQC), 1)) // SEL_BLOCK
    imp = jnp.where((j == 0) | (j == cur) | (j == cur - 1), FORCE_SCORE, imp)
    imp = jnp.where(j <= cur, imp, -1.0)
    nblk = nsel // SUBLANES
    blocks = [imp[k * SUBLANES:(k + 1) * SUBLANES, :] for k in range(nblk)]
    ranks = [jnp.zeros((SUBLANES, TQC), F32) for _ in range(nblk)]
    sub = lax.broadcasted_iota(jnp.int32, (SUBLANES, TQC), 0)
    for jp in range(nsel):
        r = jnp.broadcast_to(imp[jp:jp + 1, :], (SUBLANES, TQC))
        for k in range(nblk):
            if k > jp // SUBLANES:
                inc = jnp.where(r >= blocks[k], 1.0, 0.0)
            elif k < jp // SUBLANES:
                inc = jnp.where(r > blocks[k], 1.0, 0.0)
            else:
                tie = jnp.where(sub > jp % SUBLANES, 1.0, 0.0)
                inc = jnp.where(r > blocks[k], 1.0, jnp.where(r == blocks[k], tie, 0.0))
            ranks[k] = ranks[k] + inc
    rank = jnp.concatenate(ranks, axis=0)
    notsel_t = jnp.where((rank < float(N_SELECT)) & (imp >= 0.0), 0.0, 1.0).astype(BF16)
    notsel = lax.dot_general(eye_ref[...], notsel_t, nt, preferred_element_type=F32)

    q_rot = (_rope(q, cos_ref[...], s1_ref[...], s2_ref[...]) * SCALE).astype(BF16)
    qw = (jnp.dot(q_rot, selq_ref[...], preferred_element_type=F32)
          + jnp.dot(notsel.astype(BF16), placen_ref[...], preferred_element_type=F32))
    qaug_ref[0, 0] = stack_rows(qw).astype(BF16)


def _selection_overlap_t(n_cmp_pad, n_cmp, n_sel):
    cs = np.arange(n_cmp)[:, None] * CMP_STRIDE
    ce = cs + CMP_BLOCK
    ss = np.arange(n_sel)[None, :] * SEL_BLOCK
    se = ss + SEL_BLOCK
    ov = np.clip(np.minimum(ce, se) - np.maximum(cs, ss), 0, None) / CMP_BLOCK
    full = np.zeros((n_cmp_pad, n_sel), np.float32)
    full[:n_cmp] = ov
    return full.T.copy()


def _nsa_cmp(proj, kblk, vblk, bsz, seq, cos_t, s1_t, s2_t):
    nqc = seq // TQC
    ncp = kblk.shape[3] // ATT_GROUP
    n_cmp = (seq - CMP_BLOCK) // CMP_STRIDE + 1
    nsel = seq // SEL_BLOCK
    ovt = jnp.asarray(_selection_overlap_t(ncp, n_cmp, nsel), BF16)
    selq = jnp.asarray(_head_place_matrix(), BF16)
    placen = np.zeros((nsel, ATT_GROUP * LANES), np.float32)
    for g in range(ATT_GROUP):
        for jb in range(nsel):
            placen[jb, g * LANES + HEAD_DIM + jb] = 1.0
    placen = jnp.asarray(placen, BF16)
    eye = jnp.asarray(np.eye(TQC, dtype=np.float32), BF16)
    const = lambda shape: pl.BlockSpec(shape, lambda b, h, t: (0,) * len(shape))
    tab = pl.BlockSpec((TQC, KV_WIDTH), lambda b, h, t: (t, 0))
    qb = COL_Q // KV_WIDTH
    rows = (TQC // TQ) * ROWS
    stacked = pl.BlockSpec((1, 1, rows, LANES), lambda b, h, t: (b, h, t, 0))
    return pl.pallas_call(
        _nsacmp_kernel,
        grid=(bsz, KV_HEADS, nqc),
        in_specs=[
            pl.BlockSpec((TQC, KV_WIDTH), lambda b, h, t: (b * nqc + t, qb + h)),
            pl.BlockSpec((1, 1, KV_WIDTH, ATT_GROUP * ncp), lambda b, h, t: (b, h, 0, 0)),
            pl.BlockSpec((1, 1, ATT_GROUP * ncp, ATT_GROUP * LANES), lambda b, h, t: (b, h, 0, 0)),
            tab, tab, tab,
            const((nsel, ncp)), const((KV_WIDTH, ATT_GROUP * LANES)),
            const((nsel, ATT_GROUP * LANES)), const((TQC, TQC)),
        ],
        out_specs=[stacked, pl.BlockSpec((TQC, KV_WIDTH), lambda b, h, t: (b * nqc + t, h))],
        out_shape=[jax.ShapeDtypeStruct((bsz, KV_HEADS, nqc * rows, LANES), BF16),
                   jax.ShapeDtypeStruct((bsz * seq, ATT_WIDTH), F32)],
        compiler_params=_cparams(("parallel", "parallel", "parallel")),
        name="nsa_cmp",
    )(proj, kblk, vblk, cos_t, s1_t, s2_t, ovt, selq, placen, eye)


def _nsaattn_kernel(qaug_ref, ocmp_ref, ks_ref, vs_ref, kw_ref, vw_ref, gate_ref, gexp_ref, db_ref, wb_ref, o_ref,
                    m_ref, acc_ref):
    qt = pl.program_id(2)
    s0 = qt * TQ
    q = qaug_ref[0, 0]
    nt = (((1,), (1,)), ((), ()))

    def per_head(bias):
        return jnp.concatenate([bias] * ATT_GROUP, axis=0)

    m_ref[...] = jnp.full(m_ref.shape, NEG_INF, F32)
    acc_ref[...] = jnp.zeros(acc_ref.shape, F32)

    def tile(kb, causal):
        start = pl.multiple_of(kb * TK, TK)
        k = ks_ref[pl.ds(start, TK), :]
        v = vs_ref[pl.ds(start, TK), :]
        s = lax.dot_general(q, k, nt, preferred_element_type=F32)
        if causal:
            s = s + per_head(db_ref[0])
        m_prev = m_ref[...]
        m_next = jnp.maximum(m_prev, jnp.max(s, axis=-1, keepdims=True))
        p = jnp.exp(s - jnp.concatenate([m_next] * (TK // LANES), axis=1))
        acc_ref[...] = acc_ref[...] * jnp.exp(m_prev - m_next) + jnp.dot(
            p.astype(BF16), v, preferred_element_type=F32)
        m_ref[...] = m_next

    nfull = (qt * TQ) // TK

    def body(kb, carry):
        tile(kb, False)
        return carry

    lax.fori_loop(0, nfull, body, 0)
    tile(nfull, True)
    acc = acc_ref[...]

    wk = WINDOW + TQ
    wstart = pl.multiple_of(jnp.maximum(s0 - WINDOW, 0), TQ)
    kwin = kw_ref[pl.ds(wstart, wk), :]
    vwin = vw_ref[pl.ds(wstart, wk), :]
    sw = lax.dot_general(q, kwin, nt, preferred_element_type=F32) + per_head(wb_ref[0])
    pw = jnp.exp(sw - jnp.max(sw, axis=-1, keepdims=True))
    ow = jnp.dot(pw.astype(BF16), vwin, preferred_element_type=F32)

    def normalised(x):
        parts = [x[g * TQ:(g + 1) * TQ] for g in range(ATT_GROUP)]
        return _merge_heads(parts) / _merge_heads(parts, upper=True)

    sig = jax.nn.sigmoid(gate_ref[...])
    gexp = gexp_ref[0]
    gmap = sum(jnp.dot(t, gexp, preferred_element_type=F32) for t in _split_bf16(sig, 2))
    y = (gmap[:, 0:KV_WIDTH] * ocmp_ref[...] + gmap[:, KV_WIDTH:2 * KV_WIDTH] * normalised(acc)
         + gmap[:, 2 * KV_WIDTH:3 * KV_WIDTH] * normalised(ow))
    o_ref[...] = y.astype(o_ref.dtype)


def _nsa_attn(qaug, ocmp, ksa, vsa, kwa, vwa, proj, bsz, seq):
    nqt = seq // TQ
    gexp = np.zeros((KV_HEADS, LANES, N_BRANCH * KV_WIDTH), np.float32)
    for h in range(KV_HEADS):
        for g in range(ATT_GROUP):
            for br in range(N_BRANCH):
                src = (h * ATT_GROUP + g) * N_BRANCH + br
                gexp[h, src, br * KV_WIDTH + g * HEAD_DIM:br * KV_WIDTH + (g + 1) * HEAD_DIM] = 1.0
    gexp = jnp.asarray(gexp, BF16)
    r = np.arange(TQ)[:, None]
    diag_cases = TK // TQ
    db = np.stack([np.where(np.arange(TK)[None, :] <= c * TQ + r, 0.0, NEG_INF) for c in range(diag_cases)])
    win_cases = WINDOW // TQ + 1
    wk = WINDOW + TQ
    dist = lambda c: c * TQ + r - np.arange(wk)[None, :]
    wb = np.stack([np.where((dist(c) >= 0) & (dist(c) < WINDOW), 0.0, NEG_INF) for c in range(win_cases)])
    db = jnp.asarray(db, F32)
    wb = jnp.asarray(wb, F32)
    stacked = pl.BlockSpec((1, 1, ROWS, LANES), lambda b, h, t: (b, h, t, 0))
    kv = pl.BlockSpec((seq, LANES), lambda b, h, t: (b, h))
    return pl.pallas_call(
        _nsaattn_kernel,
        grid=(bsz, KV_HEADS, nqt),
        in_specs=[stacked, pl.BlockSpec((TQ, KV_WIDTH), lambda b, h, t: (b * nqt + t, h)), kv, kv, kv, kv,
                  pl.BlockSpec((TQ, LANES), lambda b, h, t: (b * nqt + t, COL_GATE // LANES)),
                  pl.BlockSpec((1, LANES, N_BRANCH * KV_WIDTH), lambda b, h, t: (h, 0, 0)),
                  pl.BlockSpec((1, TQ, TK), lambda b, h, t: (t % diag_cases, 0, 0)),
                  pl.BlockSpec((1, TQ, wk), lambda b, h, t: (jnp.minimum(t, win_cases - 1), 0, 0))],
        out_specs=pl.BlockSpec((TQ, KV_WIDTH), lambda b, h, t: (b * nqt + t, h)),
        out_shape=jax.ShapeDtypeStruct((bsz * seq, ATT_WIDTH), BF16),
        scratch_shapes=[pltpu.VMEM((ROWS, LANES), F32), pltpu.VMEM((ROWS, LANES), F32)],
        compiler_params=_cparams(("parallel", "parallel", "arbitrary")),
        name="nsa_attn",
    )(qaug, ocmp, ksa, vsa, kwa, vwa, proj, gexp, db, wb)


def _outproj_kernel(x_ref, ys_ref, ya_ref, w1_ref, w2_ref, o_ref):
    o_ref[...] = (x_ref[...] + jnp.dot(ys_ref[...], w1_ref[...], preferred_element_type=F32)
                  + jnp.dot(ya_ref[...], w2_ref[...], preferred_element_type=F32))


def _out_proj(x2, ys, ya, wo_b):
    t = x2.shape[0]
    tm = 512
    return pl.pallas_call(
        _outproj_kernel,
        grid=(t // tm,),
        in_specs=[
            pl.BlockSpec((tm, D_MODEL), lambda i: (i, 0)),
            pl.BlockSpec((tm, SSD_WIDTH), lambda i: (i, 0)),
            pl.BlockSpec((tm, ATT_WIDTH), lambda i: (i, 0)),
            pl.BlockSpec((SSD_WIDTH, D_MODEL), lambda i: (0, 0)),
            pl.BlockSpec((ATT_WIDTH, D_MODEL), lambda i: (1, 0)),
        ],
        out_specs=pl.BlockSpec((tm, D_MODEL), lambda i: (i, 0)),
        out_shape=jax.ShapeDtypeStruct((t, D_MODEL), F32),
        compiler_params=_cparams(("parallel",)),
        name="out_proj",
    )(x2, ys, ya, wo_b, wo_b)


def _ffn_kernel(h_ref, nw_ref, fw_ref, wg_ref, wu_ref, wd_ref, o_ref, v_ref, acc_ref):
    j = pl.program_id(1)

    @pl.when(j == 0)
    def _():
        h = h_ref[...]
        ms = jnp.mean(h * h, axis=-1, keepdims=True)
        v_ref[...] = ((h * lax.rsqrt(ms + NORM_EPS)) * nw_ref[...]).astype(BF16)
        acc_ref[...] = jnp.zeros(acc_ref.shape, F32)

    v = v_ref[...]
    gate = jnp.dot(v, wg_ref[...], preferred_element_type=F32)
    up = jnp.dot(v, wu_ref[...], preferred_element_type=F32)
    acc_ref[...] += jnp.dot((_silu(gate) * up).astype(BF16), wd_ref[...], preferred_element_type=F32)

    @pl.when(j == pl.num_programs(1) - 1)
    def _():
        h2 = h_ref[...] + acc_ref[...]
        ms = jnp.mean(h2 * h2, axis=-1, keepdims=True)
        o_ref[...] = (h2 * lax.rsqrt(ms + NORM_EPS)) * fw_ref[...]


def _ffn(h1, ffn_nw, final_w, wg_b, wu_b, wd_b):
    t = h1.shape[0]
    tm, tf = 512, 512
    return pl.pallas_call(
        _ffn_kernel,
        grid=(t // tm, D_FF // tf),
        in_specs=[
            pl.BlockSpec((tm, D_MODEL), lambda i, j: (i, 0)),
            pl.BlockSpec((1, D_MODEL), lambda i, j: (0, 0)),
            pl.BlockSpec((1, D_MODEL), lambda i, j: (0, 0)),
            pl.BlockSpec((D_MODEL, tf), lambda i, j: (0, j)),
            pl.BlockSpec((D_MODEL, tf), lambda i, j: (0, j)),
            pl.BlockSpec((tf, D_MODEL), lambda i, j: (j, 0)),
        ],
        out_specs=pl.BlockSpec((tm, D_MODEL), lambda i, j: (i, 0)),
        out_shape=jax.ShapeDtypeStruct((t, D_MODEL), F32),
        scratch_shapes=[pltpu.VMEM((tm, D_MODEL), BF16), pltpu.VMEM((tm, D_MODEL), F32)],
        compiler_params=_cparams(("parallel", "arbitrary")),
        name="ffn",
    )(h1, ffn_nw, final_w, wg_b, wu_b, wd_b)


def _rope_tables(seq):
    inv = 1.0 / (ROPE_THETA ** (jnp.arange(0, ROPE_DIM, 2, dtype=F32) / ROPE_DIM))
    ang = jnp.arange(seq, dtype=F32)[:, None] * inv[None, :]
    cos, sin = jnp.cos(ang), jnp.sin(ang)
    half = ROPE_DIM // 2
    rest_one = jnp.ones((seq, HEAD_DIM - ROPE_DIM), F32)
    rest_zero = jnp.zeros((seq, HEAD_DIM - ROPE_DIM), F32)
    zero_h = jnp.zeros((seq, half), F32)
    cos_h = jnp.concatenate([cos, cos, rest_one], axis=1)
    s1_h = jnp.concatenate([-sin, zero_h, rest_zero], axis=1)
    s2_h = jnp.concatenate([zero_h, sin, rest_zero], axis=1)
    tile = lambda a: jnp.tile(a, (1, KV_HEADS))
    return tile(cos_h), tile(s1_h), tile(s2_h)


def _pad_lanes(a, width):
    return jnp.pad(a, ((0, 0), (0, width - a.shape[1])))


def _layer(h2d, bsz, seq, p):
    (attn_norm_w, w_in, conv_w, conv_b, dt_bias, a_log, d_skip, ssd_norm_w, cmp_w1_k, cmp_w2_k, cmp_w1_v,
     cmp_w2_v, cmp_pe_k, cmp_pe_v, w_out, ffn_norm_w, w_gate, w_up, w_down) = p
    o_xbc, o_dt, o_q, o_kv, o_gate = 1024, 2560, 2576, 3600, 5136
    w_perm = jnp.concatenate([
        w_in[:, o_xbc:o_dt], w_in[:, o_kv:o_gate], w_in[:, :o_xbc], w_in[:, o_q:o_kv],
        _pad_lanes(w_in[:, o_dt:o_q], LANES), _pad_lanes(w_in[:, o_gate:], LANES)], axis=1).astype(BF16)
    proj = _in_proj(h2d, attn_norm_w[None, :], w_perm)

    y_ssd = _ssd(proj, bsz, seq, conv_w, conv_b[None, :], _pad_lanes(dt_bias[None, :], LANES),
                 _pad_lanes(a_log[None, :], LANES), jnp.repeat(d_skip, SSD_HEAD_DIM)[None, :],
                 ssd_norm_w[None, :])

    cos_t, s1_t, s2_t = _rope_tables(seq)
    ksa, vsa, kwa, vwa = _kv_prep(proj, seq, cos_t, s1_t, s2_t)

    hpb = LANES // HEAD_DIM
    pe_rep = lambda pe: jnp.tile(pe, (1, hpb))
    w1_rep = lambda w1: jnp.tile(w1.astype(BF16).reshape(CMP_BLOCK, HEAD_DIM, CMP_HIDDEN), (1, hpb, 1))
    kblk, vblk = _compress(proj, bsz, seq, pe_rep(cmp_pe_k), pe_rep(cmp_pe_v),
                           w1_rep(cmp_w1_k), w1_rep(cmp_w1_v), cmp_w2_k.T.astype(BF16),
                           _pad_lanes(cmp_w2_v, LANES).astype(BF16))
    qaug, ocmp = _nsa_cmp(proj, kblk, vblk, bsz, seq, cos_t, s1_t, s2_t)
    y_att = _nsa_attn(qaug, ocmp, ksa, vsa, kwa, vwa, proj, bsz, seq)

    h1 = _out_proj(h2d, y_ssd, y_att, w_out.astype(BF16))
    return h1, (ffn_norm_w, w_gate.astype(BF16), w_up.astype(BF16), w_down.astype(BF16))


def kernel(x, attn_norm_w, w_in, conv_w, conv_b, dt_bias, a_log, d_skip, ssd_norm_w, cmp_w1_k, cmp_w2_k,
           cmp_w1_v, cmp_w2_v, cmp_pe_k, cmp_pe_v, w_out, ffn_norm_w, w_gate, w_up, w_down, final_norm_w):
    bsz, seq, _ = x.shape
    depth = w_in.shape[0]
    assert depth == 1, "the final rmsnorm is fused into the last layer's ffn kernel"
    h = x.reshape(bsz * seq, D_MODEL)
    l = 0
    params = (attn_norm_w[l], w_in[l], conv_w[l], conv_b[l], dt_bias[l], a_log[l], d_skip[l], ssd_norm_w[l],
              cmp_w1_k[l], cmp_w2_k[l], cmp_w1_v[l], cmp_w2_v[l], cmp_pe_k[l], cmp_pe_v[l], w_out[l],
              ffn_norm_w[l], w_gate[l], w_up[l], w_down[l])
    h1, (fnw, wg_b, wu_b, wd_b) = _layer(h, bsz, seq, params)
    out = _ffn(h1, fnw[None, :], final_norm_w[None, :], wg_b, wu_b, wd_b)
    return out.reshape(bsz, seq, D_MODEL)
```

```python
import functools

import numpy as np
import jax
import jax.numpy as jnp
from jax import lax
from jax.experimental import pallas as pl
from jax.experimental.pallas import tpu as pltpu

F32 = jnp.float32
BF16 = jnp.bfloat16
HI = lax.Precision.HIGHEST

D_MODEL = 2048
SSD_WIDTH = 1024
ATT_WIDTH = 1024
SSD_HEAD_DIM = 64
SSD_HEADS = 16
SSD_GROUPS = 2
SSD_STATE = 128
SSD_CHUNK = 128
CONV_WIDTH = 4
CONV_CH = SSD_WIDTH + 2 * SSD_GROUPS * SSD_STATE
HEAD_DIM = 64
ATT_HEADS = 16
KV_HEADS = 4
ATT_GROUP = 4
KV_WIDTH = KV_HEADS * HEAD_DIM
CMP_BLOCK = 32
CMP_STRIDE = 16
CMP_HIDDEN = 256
SEL_BLOCK = 64
N_SELECT = 16
WINDOW = 512
N_BRANCH = 3
ROPE_THETA = 500000.0
ROPE_DIM = 16
D_FF = 5632
NORM_EPS = 1e-6
NEG_INF = -1e30
FORCE_SCORE = 1e4
SCALE = HEAD_DIM ** -0.5
BLOCK_BIAS = -(2.0 ** 100)

LANES = 128
VMEM_LIMIT = 56 * 1024 * 1024

NP = 5376
COL_XBC = 0
COL_KV = 1536
COL_Z = 3072
COL_Q = 4096
COL_DT = 5120
COL_GATE = 5248

TQ = 256
TK = 512
ROWS = ATT_GROUP * TQ


def _silu(x):
    return x * jax.nn.sigmoid(x)


def _split_bf16(x, terms):
    out = []
    for _ in range(terms - 1):
        t = x.astype(BF16)
        out.append(t)
        x = x - t.astype(F32)
    out.append(x.astype(BF16))
    return out


def _cparams(sem):
    return pltpu.CompilerParams(dimension_semantics=sem, vmem_limit_bytes=VMEM_LIMIT)


def _inproj_kernel(x_ref, nw_ref, w_ref, o_ref, u_ref):
    @pl.when(pl.program_id(1) == 0)
    def _():
        x = x_ref[...]
        ms = jnp.mean(x * x, axis=-1, keepdims=True)
        u_ref[...] = ((x * lax.rsqrt(ms + NORM_EPS)) * nw_ref[...]).astype(BF16)

    o_ref[...] = jnp.dot(u_ref[...], w_ref[...].astype(BF16), preferred_element_type=F32)


def _in_proj(x2, norm_w, w_perm):
    t = x2.shape[0]
    tm, tn = 1024, 768
    return pl.pallas_call(
        _inproj_kernel,
        grid=(t // tm, NP // tn),
        in_specs=[
            pl.BlockSpec((tm, D_MODEL), lambda i, j: (i, 0)),
            pl.BlockSpec((1, D_MODEL), lambda i, j: (0, 0)),
            pl.BlockSpec((D_MODEL, tn), lambda i, j: (0, j)),
        ],
        out_specs=pl.BlockSpec((tm, tn), lambda i, j: (i, j)),
        out_shape=jax.ShapeDtypeStruct((t, NP), F32),
        scratch_shapes=[pltpu.VMEM((tm, D_MODEL), BF16)],
        compiler_params=_cparams(("parallel", "arbitrary")),
        name="in_proj",
    )(x2, norm_w, w_perm)


def _ssd_kernel(xbc_ref, z_ref, dt_ref, cw_ref, cb_ref, dtb_ref, alog_ref, dskip_ref, nw_ref,
                ltri_ref, ex_ref, o_ref, ext_ref, state_ref, y_ref):
    L = SSD_CHUNK
    c = pl.program_id(1)

    @pl.when(c == 0)
    def _():
        ext_ref[0:8, :] = jnp.zeros((8, CONV_CH), F32)
        state_ref[...] = jnp.zeros(state_ref.shape, F32)

    ext_ref[8:8 + L, :] = xbc_ref[...]
    w = cw_ref[...]
    y = (ext_ref[5:5 + L, :] * w[0:1, :] + ext_ref[6:6 + L, :] * w[1:2, :]
         + ext_ref[7:7 + L, :] * w[2:3, :] + ext_ref[8:8 + L, :] * w[3:4, :]) + cb_ref[...]
    tail = ext_ref[L:L + 8, :]
    ext_ref[0:8, :] = tail
    act = _silu(y)
    xs = act[:, :SSD_WIDTH]
    bm = act[:, SSD_WIDTH:SSD_WIDTH + SSD_GROUPS * SSD_STATE]
    cm = act[:, SSD_WIDTH + SSD_GROUPS * SSD_STATE:]

    v = dt_ref[...] + dtb_ref[...]
    dt = jnp.maximum(v, 0.0) + jnp.log1p(jnp.exp(-jnp.abs(v)))
    a = -jnp.exp(alog_ref[...])
    adt = a * dt
    ltri = ltri_ref[...]
    acum = sum(jnp.dot(ltri, t, preferred_element_type=F32) for t in _split_bf16(adt, 3))
    acum_t = acum.T
    last = acum[L - 1:L, :]
    stacked = jnp.concatenate([dt, jnp.exp(acum), jnp.exp(last - acum)], axis=0)
    ex = ex_ref[...]
    expanded = sum(jnp.dot(t, ex, preferred_element_type=F32) for t in _split_bf16(stacked, 2))
    dt_e = expanded[0:L]
    expa_e = expanded[L:2 * L]
    dst_e = expanded[2 * L:3 * L]

    xdt = xs * dt_e
    xds_b = (xdt * dst_e).astype(BF16)
    row = lax.broadcasted_iota(jnp.int32, (L, L), 0)
    col = lax.broadcasted_iota(jnp.int32, (L, L), 1)
    causal = row >= col
    lane = lax.broadcasted_iota(jnp.int32, (L, LANES), 1)
    hg = SSD_HEADS // SSD_GROUPS
    gw = hg * SSD_HEAD_DIM
    for g in range(SSD_GROUPS):
        bg = bm[:, g * SSD_STATE:(g + 1) * SSD_STATE]
        cg_b = cm[:, g * SSD_STATE:(g + 1) * SSD_STATE].astype(BF16)
        bg_b = bg.astype(BF16)
        cb = lax.dot_general(cg_b, bg_b, (((1,), (1,)), ((), ())), preferred_element_type=F32)
        hprev = state_ref[g]
        yoff = jnp.dot(cg_b, hprev.astype(BF16), preferred_element_type=F32) * expa_e[:, g * gw:(g + 1) * gw]
        snew = jnp.dot(bg.T.astype(BF16), xds_b[:, g * gw:(g + 1) * gw], preferred_element_type=F32)
        state_ref[g] = hprev * expa_e[L - 1:L, g * gw:(g + 1) * gw] + snew
        for k in range(hg // 2):
            pair = g * (hg // 2) + k
            ms = []
            for hh in (2 * pair, 2 * pair + 1):
                seg = acum[:, hh:hh + 1] - acum_t[hh:hh + 1, :]
                decay = jnp.exp(jnp.where(causal, seg, -jnp.inf))
                ms.append((cb * decay).astype(BF16))
            mpair = jnp.concatenate(ms, axis=1)
            xp = xdt[:, pair * LANES:(pair + 1) * LANES]
            xblk = jnp.concatenate([jnp.where(lane < SSD_HEAD_DIM, xp, 0.0),
                                    jnp.where(lane >= SSD_HEAD_DIM, xp, 0.0)], axis=0).astype(BF16)
            yd = jnp.dot(mpair, xblk, preferred_element_type=F32)
            y_ref[:, pair * LANES:(pair + 1) * LANES] = yd + yoff[:, k * LANES:(k + 1) * LANES]

    yy = y_ref[...] + dskip_ref[...] * xs
    z = z_ref[...]
    yz = yy * _silu(z)
    ms2 = jnp.mean(yz * yz, axis=-1, keepdims=True)
    o_ref[...] = ((yz * lax.rsqrt(ms2 + NORM_EPS)) * nw_ref[...]).astype(o_ref.dtype)


def _ssd(proj, bsz, seq, conv_w, conv_b, dtb_pad, alog_pad, dskip_e, norm_w):
    nc = seq // SSD_CHUNK
    L = SSD_CHUNK
    ltri = jnp.asarray(np.tril(np.ones((L, L), np.float32)), BF16)
    ex = np.zeros((LANES, SSD_WIDTH), np.float32)
    for h in range(SSD_HEADS):
        ex[h, h * SSD_HEAD_DIM:(h + 1) * SSD_HEAD_DIM] = 1.0
    ex = jnp.asarray(ex, BF16)
    const = lambda shape: pl.BlockSpec(shape, lambda b, c: (0,) * len(shape))
    return pl.pallas_call(
        _ssd_kernel,
        grid=(bsz, nc),
        in_specs=[
            pl.BlockSpec((L, CONV_CH), lambda b, c: (b * nc + c, COL_XBC // CONV_CH)),
            pl.BlockSpec((L, SSD_WIDTH), lambda b, c: (b * nc + c, COL_Z // SSD_WIDTH)),
            pl.BlockSpec((L, LANES), lambda b, c: (b * nc + c, COL_DT // LANES)),
            const((CONV_WIDTH, CONV_CH)), const((1, CONV_CH)), const((1, LANES)), const((1, LANES)),
            const((1, SSD_WIDTH)), const((1, SSD_WIDTH)), const((L, L)), const((LANES, SSD_WIDTH)),
        ],
        out_specs=pl.BlockSpec((L, SSD_WIDTH), lambda b, c: (b * nc + c, 0)),
        out_shape=jax.ShapeDtypeStruct((bsz * seq, SSD_WIDTH), BF16),
        scratch_shapes=[
            pltpu.VMEM((L + 8, CONV_CH), F32),
            pltpu.VMEM((SSD_GROUPS, SSD_STATE, (SSD_HEADS // SSD_GROUPS) * SSD_HEAD_DIM), F32),
            pltpu.VMEM((L, SSD_WIDTH), F32),
        ],
        compiler_params=_cparams(("parallel", "arbitrary")),
        name="ssd",
    )(proj, proj, proj, conv_w, conv_b, dtb_pad, alog_pad, dskip_e, norm_w, ltri, ex)


def _rope(x, cos, s1, s2):
    n = x.shape[-1]
    half = ROPE_DIM // 2
    return x * cos + pltpu.roll(x, n - half, 1) * s1 + pltpu.roll(x, half, 1) * s2


def _merge_heads(parts, upper=False):
    lane = lax.broadcasted_iota(jnp.int32, parts[0].shape, 1)
    out = []
    for k in range(len(parts) // 2):
        a, b = parts[2 * k], parts[2 * k + 1]
        if upper:
            out.append(jnp.where(lane < HEAD_DIM, pltpu.roll(a, HEAD_DIM, 1), b))
        else:
            out.append(jnp.where(lane < HEAD_DIM, a, pltpu.roll(b, HEAD_DIM, 1)))
    return jnp.concatenate(out, axis=1)


def _kvprep_kernel(ks_ref, vs_ref, kw_ref, vw_ref, cos_ref, s1_ref, s2_ref, sel_ref,
                   kso_ref, vso_ref, kwo_ref, vwo_ref, *, tiles_per_seq):
    tr = ks_ref.shape[0]
    wide = KV_HEADS * LANES
    s0 = (pl.program_id(0) % tiles_per_seq) * tr
    cos, s1, s2 = cos_ref[...], s1_ref[...], s2_ref[...]
    sel = sel_ref[...]
    lane = lax.broadcasted_iota(jnp.int32, (tr, wide), 1) % LANES
    blk = (s0 + lax.broadcasted_iota(jnp.int32, (tr, wide), 0)) // SEL_BLOCK
    ebias = jnp.where(lane - SEL_BLOCK == blk, BLOCK_BIAS, 0.0)
    ones = jnp.where(lane >= HEAD_DIM, 1.0, 0.0)

    def place(x):
        return jnp.dot(x.astype(BF16), sel, preferred_element_type=F32)

    kso_ref[...] = (place(_rope(ks_ref[...], cos, s1, s2)) + ebias).astype(BF16)
    kwo_ref[...] = place(_rope(kw_ref[...], cos, s1, s2)).astype(BF16)
    vso_ref[...] = (place(vs_ref[...]) + ones).astype(BF16)
    vwo_ref[...] = (place(vw_ref[...]) + ones).astype(BF16)


def _head_place_matrix():
    m = np.zeros((KV_WIDTH, KV_HEADS * LANES), np.float32)
    for h in range(KV_HEADS):
        for d in range(HEAD_DIM):
            m[h * HEAD_DIM + d, h * LANES + d] = 1.0
    return m


def _kv_prep(proj, seq, cos_t, s1_t, s2_t):
    t = proj.shape[0]
    tr = 512
    tps = seq // tr
    sel = jnp.asarray(_head_place_matrix(), BF16)
    kvb = COL_KV // KV_WIDTH
    seg = lambda k: pl.BlockSpec((tr, KV_WIDTH), lambda i: (i, kvb + k))
    tab = pl.BlockSpec((tr, KV_WIDTH), lambda i: (i % tps, 0))
    wide = KV_HEADS * LANES
    out = pl.BlockSpec((tr, wide), lambda i: (i, 0))
    shp = jax.ShapeDtypeStruct((t, wide), BF16)
    return pl.pallas_call(
        functools.partial(_kvprep_kernel, tiles_per_seq=tps),
        grid=(t // tr,),
        in_specs=[seg(2), seg(3), seg(4), seg(5), tab, tab, tab,
                  pl.BlockSpec((KV_WIDTH, wide), lambda i: (0, 0))],
        out_specs=[out, out, out, out],
        out_shape=[shp, shp, shp, shp],
        compiler_params=_cparams(("parallel",)),
        name="kv_prep",
    )(proj, proj, proj, proj, cos_t, s1_t, s2_t, sel)


def _compress_kernel(gk_ref, gv_ref, pek_ref, pev_ref, w1k_ref, w1v_ref, w2kt_ref, w2v_ref,
                     kblk_ref, vblk_ref, shift_ref):
    ng = gk_ref.shape[0] // CMP_STRIDE
    heads_per_block = LANES // HEAD_DIM
    mine = (lax.broadcasted_iota(jnp.int32, (ng, LANES), 1) // HEAD_DIM) == pl.program_id(1) % heads_per_block

    def hidden(x_ref, pe_ref, w1_ref):
        top = jnp.zeros((ng, CMP_HIDDEN), F32)
        bot = jnp.zeros((ng, CMP_HIDDEN), F32)
        for r in range(CMP_STRIDE):
            x = x_ref[pl.ds(r, ng, stride=CMP_STRIDE), :]
            xt = jnp.where(mine, x + pe_ref[r:r + 1, :], 0.0).astype(BF16)
            xb = jnp.where(mine, x + pe_ref[CMP_STRIDE + r:CMP_STRIDE + r + 1, :], 0.0).astype(BF16)
            top = top + jnp.dot(xt, w1_ref[r], preferred_element_type=F32)
            bot = bot + jnp.dot(xb, w1_ref[CMP_STRIDE + r], preferred_element_type=F32)
        shift_ref[0:ng, :] = bot
        shift_ref[ng:ng + 8, :] = jnp.zeros((8, CMP_HIDDEN), F32)
        return _silu(top + shift_ref[1:ng + 1, :]).astype(BF16)

    hk = hidden(gk_ref, pek_ref, w1k_ref)
    kt = lax.dot_general(w2kt_ref[...], hk, (((1,), (1,)), ((), ())), preferred_element_type=F32)
    kblk_ref[...] = jnp.zeros(kblk_ref.shape, BF16)
    for g in range(ATT_GROUP):
        kblk_ref[0, 0, g * HEAD_DIM:(g + 1) * HEAD_DIM, g * ng:(g + 1) * ng] = kt.astype(BF16)

    hv = hidden(gv_ref, pev_ref, w1v_ref)
    vc = jnp.dot(hv, w2v_ref[...], preferred_element_type=F32)
    vblk_ref[...] = jnp.zeros(vblk_ref.shape, BF16)
    for g in range(ATT_GROUP):
        vblk_ref[0, 0, g * ng:(g + 1) * ng, g * LANES:(g + 1) * LANES] = vc.astype(BF16)


def _compress(proj, bsz, seq, pek, pev, w1k, w1v, w2kt, w2v):
    ng = seq // CMP_STRIDE
    hpb = LANES // HEAD_DIM
    kcb = COL_KV // LANES
    vcb = kcb + KV_WIDTH // LANES
    const = lambda shape: pl.BlockSpec(shape, lambda b, h: (0,) * len(shape))
    return pl.pallas_call(
        _compress_kernel,
        grid=(bsz, KV_HEADS),
        in_specs=[pl.BlockSpec((seq, LANES), lambda b, h: (b, kcb + h // hpb)),
                  pl.BlockSpec((seq, LANES), lambda b, h: (b, vcb + h // hpb)),
                  const((CMP_BLOCK, LANES)), const((CMP_BLOCK, LANES)),
                  const((CMP_BLOCK, LANES, CMP_HIDDEN)), const((CMP_BLOCK, LANES, CMP_HIDDEN)),
                  const((HEAD_DIM, CMP_HIDDEN)), const((CMP_HIDDEN, LANES))],
        out_specs=[pl.BlockSpec((1, 1, KV_WIDTH, ATT_GROUP * ng), lambda b, h: (b, h, 0, 0)),
                   pl.BlockSpec((1, 1, ATT_GROUP * ng, ATT_GROUP * LANES), lambda b, h: (b, h, 0, 0))],
        out_shape=[jax.ShapeDtypeStruct((bsz, KV_HEADS, KV_WIDTH, ATT_GROUP * ng), BF16),
                   jax.ShapeDtypeStruct((bsz, KV_HEADS, ATT_GROUP * ng, ATT_GROUP * LANES), BF16)],
        scratch_shapes=[pltpu.VMEM((ng + 8, CMP_HIDDEN), F32)],
        compiler_params=_cparams(("parallel", "parallel")),
        name="compress",
    )(proj, proj, pek, pev, w1k, w1v, w2kt, w2v)


TQC = TQ
SUBLANES = 8


def _nsacmp_kernel(q_ref, kblk_ref, vblk_ref, cos_ref, s1_ref, s2_ref, ovt_ref, selq_ref, placen_ref, eye_ref,
                   qaug_ref, ocmp_ref):
    ncp = kblk_ref.shape[3] // ATT_GROUP
    nsel = ovt_ref.shape[0]
    nt = (((1,), (1,)), ((), ()))
    s0 = pl.program_id(2) * TQC
    q = q_ref[...]
    s_all = jnp.dot(q.astype(BF16), kblk_ref[0, 0], preferred_element_type=F32) * SCALE
    tq_pos = s0 + lax.broadcasted_iota(jnp.int32, (TQC, ncp), 0)
    cmp_end = lax.broadcasted_iota(jnp.int32, (TQC, ncp), 1) * CMP_STRIDE + (CMP_BLOCK - 1)
    mask = cmp_end <= tq_pos
    maskf = mask.astype(F32)
    ps = []
    for g in range(ATT_GROUP):
        s = jnp.where(mask, s_all[:, g * ncp:(g + 1) * ncp], NEG_INF)
        m = jnp.max(s, axis=-1, keepdims=True)
        e = jnp.exp(s - m)
        ps.append(e / jnp.sum(e, axis=-1, keepdims=True) * maskf)
    p_all = jnp.concatenate(ps, axis=1)

    def stack_rows(wide):
        return jnp.concatenate([wide[u * TQ:(u + 1) * TQ, g * LANES:(g + 1) * LANES]
                                for u in range(TQC // TQ) for g in range(ATT_GROUP)], axis=0)

    ow = jnp.dot(p_all.astype(BF16), vblk_ref[0, 0], preferred_element_type=F32)
    ocmp_ref[...] = _merge_heads([ow[:, g * LANES:(g + 1) * LANES] for g in range(ATT_GROUP)])

    psum = (ps[0] + ps[1]) + (ps[2] + ps[3])
    p_hi = psum.astype(BF16)
    p_lo = (psum - p_hi.astype(F32)).astype(BF16)
    ovt = ovt_ref[...]
    imp = (lax.dot_general(ovt, p_hi, nt, preferred_element_type=F32)
           + lax.dot_general(ovt, p_lo, nt, preferred_element_type=F32))
    j = lax.broadcasted_iota(jnp.int32, (nsel, TQC), 0)
    cur = (s0 + lax.broadcasted_iota(jnp.int32, (nsel, TQC), 1)) // SEL_BLOCK
    imp = jnp.where((j == 0) | (j == cur) | (j == cur - 1), FORCE_SCORE, imp)
    imp = jnp.where(j <= cur, imp, -1.0)
    nblk = nsel // SUBLANES
    blocks = [imp[k * SUBLANES:(k + 1) * SUBLANES, :] for k in range(nblk)]
    ranks = [jnp.zeros((SUBLANES, TQC), F32) for _ in range(nblk)]
    sub = lax.broadcasted_iota(jnp.int32, (SUBLANES, TQC), 0)
    for jp in range(nsel):
        r = jnp.broadcast_to(imp[jp:jp + 1, :], (SUBLANES, TQC))
        for k in range(nblk):
            if k > jp // SUBLANES:
                inc = jnp.where(r >= blocks[k], 1.0, 0.0)
            elif k < jp // SUBLANES:
                inc = jnp.where(r > blocks[k], 1.0, 0.0)
            else:
                tie = jnp.where(sub > jp % SUBLANES, 1.0, 0.0)
                inc = jnp.where(r > blocks[k], 1.0, jnp.where(r == blocks[k], tie, 0.0))
            ranks[k] = ranks[k] + inc
    rank = jnp.concatenate(ranks, axis=0)
    notsel_t = jnp.where((rank < float(N_SELECT)) & (imp >= 0.0), 0.0, 1.0).astype(BF16)
    notsel = lax.dot_general(eye_ref[...], notsel_t, nt, preferred_element_type=F32)

    q_rot = (_rope(q, cos_ref[...], s1_ref[...], s2_ref[...]) * SCALE).astype(BF16)
    qw = (jnp.dot(q_rot, selq_ref[...], preferred_element_type=F32)
          + jnp.dot(notsel.astype(BF16), placen_ref[...], preferred_element_type=F32))
    qaug_ref[0, 0] = stack_rows(qw).astype(BF16)


def _selection_overlap_t(n_cmp_pad, n_cmp, n_sel):
    cs = np.arange(n_cmp)[:, None] * CMP_STRIDE
    ce = cs + CMP_BLOCK
    ss = np.arange(n_sel)[None, :] * SEL_BLOCK
    se = ss + SEL_BLOCK
    ov = np.clip(np.minimum(ce, se) - np.maximum(cs, ss), 0, None) / CMP_BLOCK
    full = np.zeros((n_cmp_pad, n_sel), np.float32)
    full[:n_cmp] = ov
    return full.T.copy()


def _nsa_cmp(proj, kblk, vblk, bsz, seq, cos_t, s1_t, s2_t):
    nqc = seq // TQC
    ncp = kblk.shape[3] // ATT_GROUP
    n_cmp = (seq - CMP_BLOCK) // CMP_STRIDE + 1
    nsel = seq // SEL_BLOCK
    ovt = jnp.asarray(_selection_overlap_t(ncp, n_cmp, nsel), BF16)
    selq = jnp.asarray(_head_place_matrix(), BF16)
    placen = np.zeros((nsel, ATT_GROUP * LANES), np.float32)
    for g in range(ATT_GROUP):
        for jb in range(nsel):
            placen[jb, g * LANES + HEAD_DIM + jb] = 1.0
    placen = jnp.asarray(placen, BF16)
    eye = jnp.asarray(np.eye(TQC, dtype=np.float32), BF16)
    const = lambda shape: pl.BlockSpec(shape, lambda b, h, t: (0,) * len(shape))
    tab = pl.BlockSpec((TQC, KV_WIDTH), lambda b, h, t: (t, 0))
    qb = COL_Q // KV_WIDTH
    rows = (TQC // TQ) * ROWS
    stacked = pl.BlockSpec((1, 1, rows, LANES), lambda b, h, t: (b, h, t, 0))
    return pl.pallas_call(
        _nsacmp_kernel,
        grid=(bsz, KV_HEADS, nqc),
        in_specs=[
            pl.BlockSpec((TQC, KV_WIDTH), lambda b, h, t: (b * nqc + t, qb + h)),
            pl.BlockSpec((1, 1, KV_WIDTH, ATT_GROUP * ncp), lambda b, h, t: (b, h, 0, 0)),
            pl.BlockSpec((1, 1, ATT_GROUP * ncp, ATT_GROUP * LANES), lambda b, h, t: (b, h, 0, 0)),
            tab, tab, tab,
            const((nsel, ncp)), const((KV_WIDTH, ATT_GROUP * LANES)),
            const((nsel, ATT_GROUP * LANES)), const((TQC, TQC)),
        ],
        out_specs=[stacked, pl.BlockSpec((TQC, KV_WIDTH), lambda b, h, t: (b * nqc + t, h))],
        out_shape=[jax.ShapeDtypeStruct((bsz, KV_HEADS, nqc * rows, LANES), BF16),
                   jax.ShapeDtypeStruct((bsz * seq, ATT_WIDTH), F32)],
        compiler_params=_cparams(("parallel", "parallel", "parallel")),
        name="nsa_cmp",
    )(proj, kblk, vblk, cos_t, s1_t, s2_t, ovt, selq, placen, eye)


def _nsaattn_kernel(qaug_ref, ocmp_ref, ks_ref, vs_ref, kw_ref, vw_ref, gate_ref, gexp_ref, db_ref, wb_ref, o_ref,
                    m_ref, acc_ref):
    qt = pl.program_id(2)
    s0 = qt * TQ
    q = qaug_ref[0, 0]
    nt = (((1,), (1,)), ((), ()))

    def per_head(bias):
        return jnp.concatenate([bias] * ATT_GROUP, axis=0)

    m_ref[...] = jnp.full(m_ref.shape, NEG_INF, F32)
    acc_ref[...] = jnp.zeros(acc_ref.shape, F32)

    def tile(kb, causal):
        start = pl.multiple_of(kb * TK, TK)
        k = ks_ref[pl.ds(start, TK), :]
        v = vs_ref[pl.ds(start, TK), :]
        s = lax.dot_general(q, k, nt, preferred_element_type=F32)
        if causal:
            s = s + per_head(db_ref[0])
        m_prev = m_ref[...]
        m_next = jnp.maximum(m_prev, jnp.max(s, axis=-1, keepdims=True))
        p = jnp.exp(s - jnp.concatenate([m_next] * (TK // LANES), axis=1))
        acc_ref[...] = acc_ref[...] * jnp.exp(m_prev - m_next) + jnp.dot(
            p.astype(BF16), v, preferred_element_type=F32)
        m_ref[...] = m_next

    nfull = (qt * TQ) // TK

    def body(kb, carry):
        tile(kb, False)
        return carry

    lax.fori_loop(0, nfull, body, 0)
    tile(nfull, True)
    acc = acc_ref[...]

    wk = WINDOW + TQ
    wstart = pl.multiple_of(jnp.maximum(s0 - WINDOW, 0), TQ)
    kwin = kw_ref[pl.ds(wstart, wk), :]
    vwin = vw_ref[pl.ds(wstart, wk), :]
    sw = lax.dot_general(q, kwin, nt, preferred_element_type=F32) + per_head(wb_ref[0])
    pw = jnp.exp(sw - jnp.max(sw, axis=-1, keepdims=True))
    ow = jnp.dot(pw.astype(BF16), vwin, preferred_element_type=F32)

    def normalised(x):
        parts = [x[g * TQ:(g + 1) * TQ] for g in range(ATT_GROUP)]
        return _merge_heads(parts) / _merge_heads(parts, upper=True)

    sig = jax.nn.sigmoid(gate_ref[...])
    gexp = gexp_ref[0]
    gmap = sum(jnp.dot(t, gexp, preferred_element_type=F32) for t in _split_bf16(sig, 2))
    y = (gmap[:, 0:KV_WIDTH] * ocmp_ref[...] + gmap[:, KV_WIDTH:2 * KV_WIDTH] * normalised(acc)
         + gmap[:, 2 * KV_WIDTH:3 * KV_WIDTH] * normalised(ow))
    o_ref[...] = y.astype(o_ref.dtype)


def _nsa_attn(qaug, ocmp, ksa, vsa, kwa, vwa, proj, bsz, seq):
    nqt = seq // TQ
    gexp = np.zeros((KV_HEADS, LANES, N_BRANCH * KV_WIDTH), np.float32)
    for h in range(KV_HEADS):
        for g in range(ATT_GROUP):
            for br in range(N_BRANCH):
                src = (h * ATT_GROUP + g) * N_BRANCH + br
                gexp[h, src, br * KV_WIDTH + g * HEAD_DIM:br * KV_WIDTH + (g + 1) * HEAD_DIM] = 1.0
    gexp = jnp.asarray(gexp, BF16)
    r = np.arange(TQ)[:, None]
    diag_cases = TK // TQ
    db = np.stack([np.where(np.arange(TK)[None, :] <= c * TQ + r, 0.0, NEG_INF) for c in range(diag_cases)])
    win_cases = WINDOW // TQ + 1
    wk = WINDOW + TQ
    dist = lambda c: c * TQ + r - np.arange(wk)[None, :]
    wb = np.stack([np.where((dist(c) >= 0) & (dist(c) < WINDOW), 0.0, NEG_INF) for c in range(win_cases)])
    db = jnp.asarray(db, F32)
    wb = jnp.asarray(wb, F32)
    stacked = pl.BlockSpec((1, 1, ROWS, LANES), lambda b, h, t: (b, h, t, 0))
    kv = pl.BlockSpec((seq, LANES), lambda b, h, t: (b, h))
    return pl.pallas_call(
        _nsaattn_kernel,
        grid=(bsz, KV_HEADS, nqt),
        in_specs=[stacked, pl.BlockSpec((TQ, KV_WIDTH), lambda b, h, t: (b * nqt + t, h)), kv, kv, kv, kv,
                  pl.BlockSpec((TQ, LANES), lambda b, h, t: (b * nqt + t, COL_GATE // LANES)),
                  pl.BlockSpec((1, LANES, N_BRANCH * KV_WIDTH), lambda b, h, t: (h, 0, 0)),
                  pl.BlockSpec((1, TQ, TK), lambda b, h, t: (t % diag_cases, 0, 0)),
                  pl.BlockSpec((1, TQ, wk), lambda b, h, t: (jnp.minimum(t, win_cases - 1), 0, 0))],
        out_specs=pl.BlockSpec((TQ, KV_WIDTH), lambda b, h, t: (b * nqt + t, h)),
        out_shape=jax.ShapeDtypeStruct((bsz * seq, ATT_WIDTH), BF16),
        scratch_shapes=[pltpu.VMEM((ROWS, LANES), F32), pltpu.VMEM((ROWS, LANES), F32)],
        compiler_params=_cparams(("parallel", "parallel", "arbitrary")),
        name="nsa_attn",
    )(qaug, ocmp, ksa, vsa, kwa, vwa, proj, gexp, db, wb)


def _outproj_kernel(x_ref, ys_ref, ya_ref, w1_ref, w2_ref, o_ref):
    o_ref[...] = (x_ref[...] + jnp.dot(ys_ref[...], w1_ref[...], preferred_element_type=F32)
                  + jnp.dot(ya_ref[...], w2_ref[...], preferred_element_type=F32))


def _out_proj(x2, ys, ya, wo_b):
    t = x2.shape[0]
    tm = 512
    return pl.pallas_call(
        _outproj_kernel,
        grid=(t // tm,),
        in_specs=[
            pl.BlockSpec((tm, D_MODEL), lambda i: (i, 0)),
            pl.BlockSpec((tm, SSD_WIDTH), lambda i: (i, 0)),
            pl.BlockSpec((tm, ATT_WIDTH), lambda i: (i, 0)),
            pl.BlockSpec((SSD_WIDTH, D_MODEL), lambda i: (0, 0)),
            pl.BlockSpec((ATT_WIDTH, D_MODEL), lambda i: (1, 0)),
        ],
        out_specs=pl.BlockSpec((tm, D_MODEL), lambda i: (i, 0)),
        out_shape=jax.ShapeDtypeStruct((t, D_MODEL), F32),
        compiler_params=_cparams(("parallel",)),
        name="out_proj",
    )(x2, ys, ya, wo_b, wo_b)


def _ffn_kernel(h_ref, nw_ref, fw_ref, wg_ref, wu_ref, wd_ref, o_ref, v_ref):
    j = pl.program_id(1)

    @pl.when(j == 0)
    def _():
        h = h_ref[...]
        ms = jnp.mean(h * h, axis=-1, keepdims=True)
        v_ref[...] = ((h * lax.rsqrt(ms + NORM_EPS)) * nw_ref[...]).astype(BF16)
        o_ref[...] = h

    v = v_ref[...]
    gate = jnp.dot(v, wg_ref[...].astype(BF16), preferred_element_type=F32)
    up = jnp.dot(v, wu_ref[...].astype(BF16), preferred_element_type=F32)
    o_ref[...] += jnp.dot((_silu(gate) * up).astype(BF16), wd_ref[...].astype(BF16), preferred_element_type=F32)

    @pl.when(j == pl.num_programs(1) - 1)
    def _():
        h2 = o_ref[...]
        ms = jnp.mean(h2 * h2, axis=-1, keepdims=True)
        o_ref[...] = (h2 * lax.rsqrt(ms + NORM_EPS)) * fw_ref[...]


def _ffn(h1, ffn_nw, final_w, wg_b, wu_b, wd_b):
    t = h1.shape[0]
    tm, tf = 1024, 256
    return pl.pallas_call(
        _ffn_kernel,
        grid=(t // tm, D_FF // tf),
        in_specs=[
            pl.BlockSpec((tm, D_MODEL), lambda i, j: (i, 0)),
            pl.BlockSpec((1, D_MODEL), lambda i, j: (0, 0)),
            pl.BlockSpec((1, D_MODEL), lambda i, j: (0, 0)),
            pl.BlockSpec((D_MODEL, tf), lambda i, j: (0, j)),
            pl.BlockSpec((D_MODEL, tf), lambda i, j: (0, j)),
            pl.BlockSpec((tf, D_MODEL), lambda i, j: (j, 0)),
        ],
        out_specs=pl.BlockSpec((tm, D_MODEL), lambda i, j: (i, 0)),
        out_shape=jax.ShapeDtypeStruct((t, D_MODEL), F32),
        scratch_shapes=[pltpu.VMEM((tm, D_MODEL), BF16)],
        compiler_params=_cparams(("parallel", "arbitrary")),
        name="ffn",
    )(h1, ffn_nw, final_w, wg_b, wu_b, wd_b)


def _rope_tables(seq):
    inv = 1.0 / (ROPE_THETA ** (jnp.arange(0, ROPE_DIM, 2, dtype=F32) / ROPE_DIM))
    ang = jnp.arange(seq, dtype=F32)[:, None] * inv[None, :]
    cos, sin = jnp.cos(ang), jnp.sin(ang)
    half = ROPE_DIM // 2
    rest_one = jnp.ones((seq, HEAD_DIM - ROPE_DIM), F32)
    rest_zero = jnp.zeros((seq, HEAD_DIM - ROPE_DIM), F32)
    zero_h = jnp.zeros((seq, half), F32)
    cos_h = jnp.concatenate([cos, cos, rest_one], axis=1)
    s1_h = jnp.concatenate([-sin, zero_h, rest_zero], axis=1)
    s2_h = jnp.concatenate([zero_h, sin, rest_zero], axis=1)
    tile = lambda a: jnp.tile(a, (1, KV_HEADS))
    return tile(cos_h), tile(s1_h), tile(s2_h)


def _pad_lanes(a, width):
    return jnp.pad(a, ((0, 0), (0, width - a.shape[1])))


def _layer(h2d, bsz, seq, p):
    (attn_norm_w, w_in, conv_w, conv_b, dt_bias, a_log, d_skip, ssd_norm_w, cmp_w1_k, cmp_w2_k, cmp_w1_v,
     cmp_w2_v, cmp_pe_k, cmp_pe_v, w_out, ffn_norm_w, w_gate, w_up, w_down) = p
    o_xbc, o_dt, o_q, o_kv, o_gate = 1024, 2560, 2576, 3600, 5136
    w_perm = jnp.concatenate([
        w_in[:, o_xbc:o_dt], w_in[:, o_kv:o_gate], w_in[:, :o_xbc], w_in[:, o_q:o_kv],
        _pad_lanes(w_in[:, o_dt:o_q], LANES), _pad_lanes(w_in[:, o_gate:], LANES)], axis=1)
    proj = _in_proj(h2d, attn_norm_w[None, :], w_perm)

    y_ssd = _ssd(proj, bsz, seq, conv_w, conv_b[None, :], _pad_lanes(dt_bias[None, :], LANES),
                 _pad_lanes(a_log[None, :], LANES), jnp.repeat(d_skip, SSD_HEAD_DIM)[None, :],
                 ssd_norm_w[None, :])

    cos_t, s1_t, s2_t = _rope_tables(seq)
    ksa, vsa, kwa, vwa = _kv_prep(proj, seq, cos_t, s1_t, s2_t)

    hpb = LANES // HEAD_DIM
    pe_rep = lambda pe: jnp.tile(pe, (1, hpb))
    w1_rep = lambda w1: jnp.tile(w1.astype(BF16).reshape(CMP_BLOCK, HEAD_DIM, CMP_HIDDEN), (1, hpb, 1))
    kblk, vblk = _compress(proj, bsz, seq, pe_rep(cmp_pe_k), pe_rep(cmp_pe_v),
                           w1_rep(cmp_w1_k), w1_rep(cmp_w1_v), cmp_w2_k.T.astype(BF16),
                           _pad_lanes(cmp_w2_v, LANES).astype(BF16))
    qaug, ocmp = _nsa_cmp(proj, kblk, vblk, bsz, seq, cos_t, s1_t, s2_t)
    y_att = _nsa_attn(qaug, ocmp, ksa, vsa, kwa, vwa, proj, bsz, seq)

    h1 = _out_proj(h2d, y_ssd, y_att, w_out.astype(BF16))
    return h1, (ffn_norm_w, w_gate, w_up, w_down)


def kernel(x, attn_norm_w, w_in, conv_w, conv_b, dt_bias, a_log, d_skip, ssd_norm_w, cmp_w1_k, cmp_w2_k,
           cmp_w1_v, cmp_w2_v, cmp_pe_k, cmp_pe_v, w_out, ffn_norm_w, w_gate, w_up, w_down, final_norm_w):
    bsz, seq, _ = x.shape
    depth = w_in.shape[0]
    assert depth == 1, "the final rmsnorm is fused into the last layer's ffn kernel"
    h = x.reshape(bsz * seq, D_MODEL)
    l = 0
    params = (attn_norm_w[l], w_in[l], conv_w[l], conv_b[l], dt_bias[l], a_log[l], d_skip[l], ssd_norm_w[l],
              cmp_w1_k[l], cmp_w2_k[l], cmp_w1_v[l], cmp_w2_v[l], cmp_pe_k[l], cmp_pe_v[l], w_out[l],
              ffn_norm_w[l], w_gate[l], w_up[l], w_down[l])
    h1, (fnw, wg_b, wu_b, wd_b) = _layer(h, bsz, seq, params)
    out = _ffn(h1, fnw[None, :], final_norm_w[None, :], wg_b, wu_b, wd_b)
    return out.reshape(bsz, seq, D_MODEL)
```

```python
import functools

import numpy as np
import jax
import jax.numpy as jnp
from jax import lax
from jax.experimental import pallas as pl
from jax.experimental.pallas import tpu as pltpu

F32 = jnp.float32
BF16 = jnp.bfloat16
HI = lax.Precision.HIGHEST

D_MODEL = 2048
SSD_WIDTH = 1024
ATT_WIDTH = 1024
SSD_HEAD_DIM = 64
SSD_HEADS = 16
SSD_GROUPS = 2
SSD_STATE = 128
SSD_CHUNK = 128
CONV_WIDTH = 4
CONV_CH = SSD_WIDTH + 2 * SSD_GROUPS * SSD_STATE
HEAD_DIM = 64
ATT_HEADS = 16
KV_HEADS = 4
ATT_GROUP = 4
KV_WIDTH = KV_HEADS * HEAD_DIM
CMP_BLOCK = 32
CMP_STRIDE = 16
CMP_HIDDEN = 256
SEL_BLOCK = 64
N_SELECT = 16
WINDOW = 512
N_BRANCH = 3
ROPE_THETA = 500000.0
ROPE_DIM = 16
D_FF = 5632
NORM_EPS = 1e-6
NEG_INF = -1e30
FORCE_SCORE = 1e4
SCALE = HEAD_DIM ** -0.5
BLOCK_BIAS = -(2.0 ** 100)

LANES = 128
VMEM_LIMIT = 56 * 1024 * 1024

NP = 5376
COL_XBC = 0
COL_KV = 1536
COL_Z = 3072
COL_Q = 4096
COL_DT = 5120
COL_GATE = 5248

TQ = 256
TK = 512
ROWS = ATT_GROUP * TQ


def _silu(x):
    return x * jax.nn.sigmoid(x)


def _split_bf16(x, terms):
    out = []
    for _ in range(terms - 1):
        t = x.astype(BF16)
        out.append(t)
        x = x - t.astype(F32)
    out.append(x.astype(BF16))
    return out


def _cparams(sem):
    return pltpu.CompilerParams(dimension_semantics=sem, vmem_limit_bytes=VMEM_LIMIT)


def _inproj_kernel(x_ref, nw_ref, w_ref, o_ref, u_ref):
    @pl.when(pl.program_id(1) == 0)
    def _():
        x = x_ref[...]
        ms = jnp.mean(x * x, axis=-1, keepdims=True)
        u_ref[...] = ((x * lax.rsqrt(ms + NORM_EPS)) * nw_ref[...]).astype(BF16)

    o_ref[...] = jnp.dot(u_ref[...], w_ref[...], preferred_element_type=F32)


def _in_proj(x2, norm_w, w_perm):
    t = x2.shape[0]
    tm, tn = 1024, 768
    return pl.pallas_call(
        _inproj_kernel,
        grid=(t // tm, NP // tn),
        in_specs=[
            pl.BlockSpec((tm, D_MODEL), lambda i, j: (i, 0)),
            pl.BlockSpec((1, D_MODEL), lambda i, j: (0, 0)),
            pl.BlockSpec((D_MODEL, tn), lambda i, j: (0, j)),
        ],
        out_specs=pl.BlockSpec((tm, tn), lambda i, j: (i, j)),
        out_shape=jax.ShapeDtypeStruct((t, NP), F32),
        scratch_shapes=[pltpu.VMEM((tm, D_MODEL), BF16)],
        compiler_params=_cparams(("parallel", "arbitrary")),
        name="in_proj",
    )(x2, norm_w, w_perm)


def _ssd_kernel(xbc_ref, z_ref, dt_ref, cw_ref, cb_ref, dtb_ref, alog_ref, dskip_ref, nw_ref,
                ltri_ref, ex_ref, o_ref, ext_ref, state_ref, y_ref):
    L = SSD_CHUNK
    c = pl.program_id(1)

    @pl.when(c == 0)
    def _():
        ext_ref[0:8, :] = jnp.zeros((8, CONV_CH), F32)
        state_ref[...] = jnp.zeros(state_ref.shape, F32)

    ext_ref[8:8 + L, :] = xbc_ref[...]
    w = cw_ref[...]
    y = (ext_ref[5:5 + L, :] * w[0:1, :] + ext_ref[6:6 + L, :] * w[1:2, :]
         + ext_ref[7:7 + L, :] * w[2:3, :] + ext_ref[8:8 + L, :] * w[3:4, :]) + cb_ref[...]
    tail = ext_ref[L:L + 8, :]
    ext_ref[0:8, :] = tail
    act = _silu(y)
    xs = act[:, :SSD_WIDTH]
    bm = act[:, SSD_WIDTH:SSD_WIDTH + SSD_GROUPS * SSD_STATE]
    cm = act[:, SSD_WIDTH + SSD_GROUPS * SSD_STATE:]

    v = dt_ref[...] + dtb_ref[...]
    dt = jnp.maximum(v, 0.0) + jnp.log1p(jnp.exp(-jnp.abs(v)))
    a = -jnp.exp(alog_ref[...])
    adt = a * dt
    ltri = ltri_ref[...]
    acum = sum(jnp.dot(ltri, t, preferred_element_type=F32) for t in _split_bf16(adt, 3))
    acum_t = acum.T
    last = acum[L - 1:L, :]
    stacked = jnp.concatenate([dt, jnp.exp(acum), jnp.exp(last - acum)], axis=0)
    ex = ex_ref[...]
    expanded = sum(jnp.dot(t, ex, preferred_element_type=F32) for t in _split_bf16(stacked, 2))
    dt_e = expanded[0:L]
    expa_e = expanded[L:2 * L]
    dst_e = expanded[2 * L:3 * L]

    xdt = xs * dt_e
    xds_b = (xdt * dst_e).astype(BF16)
    row = lax.broadcasted_iota(jnp.int32, (L, L), 0)
    col = lax.broadcasted_iota(jnp.int32, (L, L), 1)
    causal = row >= col
    lane = lax.broadcasted_iota(jnp.int32, (L, LANES), 1)
    hg = SSD_HEADS // SSD_GROUPS
    gw = hg * SSD_HEAD_DIM
    for g in range(SSD_GROUPS):
        bg = bm[:, g * SSD_STATE:(g + 1) * SSD_STATE]
        cg_b = cm[:, g * SSD_STATE:(g + 1) * SSD_STATE].astype(BF16)
        bg_b = bg.astype(BF16)
        cb = lax.dot_general(cg_b, bg_b, (((1,), (1,)), ((), ())), preferred_element_type=F32)
        hprev = state_ref[g]
        yoff = jnp.dot(cg_b, hprev.astype(BF16), preferred_element_type=F32) * expa_e[:, g * gw:(g + 1) * gw]
        snew = jnp.dot(bg.T.astype(BF16), xds_b[:, g * gw:(g + 1) * gw], preferred_element_type=F32)
        state_ref[g] = hprev * expa_e[L - 1:L, g * gw:(g + 1) * gw] + snew
        for k in range(hg // 2):
            pair = g * (hg // 2) + k
            ms = []
            for hh in (2 * pair, 2 * pair + 1):
                seg = acum[:, hh:hh + 1] - acum_t[hh:hh + 1, :]
                decay = jnp.exp(jnp.where(causal, seg, -jnp.inf))
                ms.append((cb * decay).astype(BF16))
            mpair = jnp.concatenate(ms, axis=1)
            xp = xdt[:, pair * LANES:(pair + 1) * LANES]
            xblk = jnp.concatenate([jnp.where(lane < SSD_HEAD_DIM, xp, 0.0),
                                    jnp.where(lane >= SSD_HEAD_DIM, xp, 0.0)], axis=0).astype(BF16)
            yd = jnp.dot(mpair, xblk, preferred_element_type=F32)
            y_ref[:, pair * LANES:(pair + 1) * LANES] = yd + yoff[:, k * LANES:(k + 1) * LANES]

    yy = y_ref[...] + dskip_ref[...] * xs
    z = z_ref[...]
    yz = yy * _silu(z)
    ms2 = jnp.mean(yz * yz, axis=-1, keepdims=True)
    o_ref[...] = ((yz * lax.rsqrt(ms2 + NORM_EPS)) * nw_ref[...]).astype(o_ref.dtype)


def _ssd(proj, bsz, seq, conv_w, conv_b, dtb_pad, alog_pad, dskip_e, norm_w):
    nc = seq // SSD_CHUNK
    L = SSD_CHUNK
    ltri = jnp.asarray(np.tril(np.ones((L, L), np.float32)), BF16)
    ex = np.zeros((LANES, SSD_WIDTH), np.float32)
    for h in range(SSD_HEADS):
        ex[h, h * SSD_HEAD_DIM:(h + 1) * SSD_HEAD_DIM] = 1.0
    ex = jnp.asarray(ex, BF16)
    const = lambda shape: pl.BlockSpec(shape, lambda b, c: (0,) * len(shape))
    return pl.pallas_call(
        _ssd_kernel,
        grid=(bsz, nc),
        in_specs=[
            pl.BlockSpec((L, CONV_CH), lambda b, c: (b * nc + c, COL_XBC // CONV_CH)),
            pl.BlockSpec((L, SSD_WIDTH), lambda b, c: (b * nc + c, COL_Z // SSD_WIDTH)),
            pl.BlockSpec((L, LANES), lambda b, c: (b * nc + c, COL_DT // LANES)),
            const((CONV_WIDTH, CONV_CH)), const((1, CONV_CH)), const((1, LANES)), const((1, LANES)),
            const((1, SSD_WIDTH)), const((1, SSD_WIDTH)), const((L, L)), const((LANES, SSD_WIDTH)),
        ],
        out_specs=pl.BlockSpec((L, SSD_WIDTH), lambda b, c: (b * nc + c, 0)),
        out_shape=jax.ShapeDtypeStruct((bsz * seq, SSD_WIDTH), BF16),
        scratch_shapes=[
            pltpu.VMEM((L + 8, CONV_CH), F32),
            pltpu.VMEM((SSD_GROUPS, SSD_STATE, (SSD_HEADS // SSD_GROUPS) * SSD_HEAD_DIM), F32),
            pltpu.VMEM((L, SSD_WIDTH), F32),
        ],
        compiler_params=_cparams(("parallel", "arbitrary")),
        name="ssd",
    )(proj, proj, proj, conv_w, conv_b, dtb_pad, alog_pad, dskip_e, norm_w, ltri, ex)


def _rope(x, cos, s1, s2):
    n = x.shape[-1]
    half = ROPE_DIM // 2
    return x * cos + pltpu.roll(x, n - half, 1) * s1 + pltpu.roll(x, half, 1) * s2


def _merge_heads(parts, upper=False):
    lane = lax.broadcasted_iota(jnp.int32, parts[0].shape, 1)
    out = []
    for k in range(len(parts) // 2):
        a, b = parts[2 * k], parts[2 * k + 1]
        if upper:
            out.append(jnp.where(lane < HEAD_DIM, pltpu.roll(a, HEAD_DIM, 1), b))
        else:
            out.append(jnp.where(lane < HEAD_DIM, a, pltpu.roll(b, HEAD_DIM, 1)))
    return jnp.concatenate(out, axis=1)


def _kvprep_kernel(ks_ref, vs_ref, kw_ref, vw_ref, cos_ref, s1_ref, s2_ref, sel_ref,
                   kso_ref, vso_ref, kwo_ref, vwo_ref, *, tiles_per_seq):
    tr = ks_ref.shape[0]
    wide = KV_HEADS * LANES
    s0 = (pl.program_id(0) % tiles_per_seq) * tr
    cos, s1, s2 = cos_ref[...], s1_ref[...], s2_ref[...]
    sel = sel_ref[...]
    lane = lax.broadcasted_iota(jnp.int32, (tr, wide), 1) % LANES
    blk = (s0 + lax.broadcasted_iota(jnp.int32, (tr, wide), 0)) // SEL_BLOCK
    ebias = jnp.where(lane - SEL_BLOCK == blk, BLOCK_BIAS, 0.0)
    ones = jnp.where(lane >= HEAD_DIM, 1.0, 0.0)

    def place(x):
        return jnp.dot(x.astype(BF16), sel, preferred_element_type=F32)

    kso_ref[...] = (place(_rope(ks_ref[...], cos, s1, s2)) + ebias).astype(BF16)
    kwo_ref[...] = place(_rope(kw_ref[...], cos, s1, s2)).astype(BF16)
    vso_ref[...] = (place(vs_ref[...]) + ones).astype(BF16)
    vwo_ref[...] = (place(vw_ref[...]) + ones).astype(BF16)


def _head_place_matrix():
    m = np.zeros((KV_WIDTH, KV_HEADS * LANES), np.float32)
    for h in range(KV_HEADS):
        for d in range(HEAD_DIM):
            m[h * HEAD_DIM + d, h * LANES + d] = 1.0
    return m


def _kv_prep(proj, seq, cos_t, s1_t, s2_t):
    t = proj.shape[0]
    tr = 512
    tps = seq // tr
    sel = jnp.asarray(_head_place_matrix(), BF16)
    kvb = COL_KV // KV_WIDTH
    seg = lambda k: pl.BlockSpec((tr, KV_WIDTH), lambda i: (i, kvb + k))
    tab = pl.BlockSpec((tr, KV_WIDTH), lambda i: (i % tps, 0))
    wide = KV_HEADS * LANES
    out = pl.BlockSpec((tr, wide), lambda i: (i, 0))
    shp = jax.ShapeDtypeStruct((t, wide), BF16)
    return pl.pallas_call(
        functools.partial(_kvprep_kernel, tiles_per_seq=tps),
        grid=(t // tr,),
        in_specs=[seg(2), seg(3), seg(4), seg(5), tab, tab, tab,
                  pl.BlockSpec((KV_WIDTH, wide), lambda i: (0, 0))],
        out_specs=[out, out, out, out],
        out_shape=[shp, shp, shp, shp],
        compiler_params=_cparams(("parallel",)),
        name="kv_prep",
    )(proj, proj, proj, proj, cos_t, s1_t, s2_t, sel)


def _compress_kernel(gk_ref, gv_ref, pek_ref, pev_ref, w1k_ref, w1v_ref, w2kt_ref, w2v_ref,
                     kblk_ref, vblk_ref, shift_ref):
    ng = gk_ref.shape[0] // CMP_STRIDE
    heads_per_block = LANES // HEAD_DIM
    mine = (lax.broadcasted_iota(jnp.int32, (ng, LANES), 1) // HEAD_DIM) == pl.program_id(1) % heads_per_block

    def hidden(x_ref, pe_ref, w1_ref):
        top = jnp.zeros((ng, CMP_HIDDEN), F32)
        bot = jnp.zeros((ng, CMP_HIDDEN), F32)
        for r in range(CMP_STRIDE):
            x = x_ref[pl.ds(r, ng, stride=CMP_STRIDE), :]
            xt = jnp.where(mine, x + pe_ref[r:r + 1, :], 0.0).astype(BF16)
            xb = jnp.where(mine, x + pe_ref[CMP_STRIDE + r:CMP_STRIDE + r + 1, :], 0.0).astype(BF16)
            top = top + jnp.dot(xt, w1_ref[r], preferred_element_type=F32)
            bot = bot + jnp.dot(xb, w1_ref[CMP_STRIDE + r], preferred_element_type=F32)
        shift_ref[0:ng, :] = bot
        shift_ref[ng:ng + 8, :] = jnp.zeros((8, CMP_HIDDEN), F32)
        return _silu(top + shift_ref[1:ng + 1, :]).astype(BF16)

    hk = hidden(gk_ref, pek_ref, w1k_ref)
    kt = lax.dot_general(w2kt_ref[...], hk, (((1,), (1,)), ((), ())), preferred_element_type=F32)
    kblk_ref[...] = jnp.zeros(kblk_ref.shape, BF16)
    for g in range(ATT_GROUP):
        kblk_ref[0, 0, g * HEAD_DIM:(g + 1) * HEAD_DIM, g * ng:(g + 1) * ng] = kt.astype(BF16)

    hv = hidden(gv_ref, pev_ref, w1v_ref)
    vc = jnp.dot(hv, w2v_ref[...], preferred_element_type=F32)
    vblk_ref[...] = jnp.zeros(vblk_ref.shape, BF16)
    for g in range(ATT_GROUP):
        vblk_ref[0, 0, g * ng:(g + 1) * ng, g * LANES:(g + 1) * LANES] = vc.astype(BF16)


def _compress(proj, bsz, seq, pek, pev, w1k, w1v, w2kt, w2v):
    ng = seq // CMP_STRIDE
    hpb = LANES // HEAD_DIM
    kcb = COL_KV // LANES
    vcb = kcb + KV_WIDTH // LANES
    const = lambda shape: pl.BlockSpec(shape, lambda b, h: (0,) * len(shape))
    return pl.pallas_call(
        _compress_kernel,
        grid=(bsz, KV_HEADS),
        in_specs=[pl.BlockSpec((seq, LANES), lambda b, h: (b, kcb + h // hpb)),
                  pl.BlockSpec((seq, LANES), lambda b, h: (b, vcb + h // hpb)),
                  const((CMP_BLOCK, LANES)), const((CMP_BLOCK, LANES)),
                  const((CMP_BLOCK, LANES, CMP_HIDDEN)), const((CMP_BLOCK, LANES, CMP_HIDDEN)),
                  const((HEAD_DIM, CMP_HIDDEN)), const((CMP_HIDDEN, LANES))],
        out_specs=[pl.BlockSpec((1, 1, KV_WIDTH, ATT_GROUP * ng), lambda b, h: (b, h, 0, 0)),
                   pl.BlockSpec((1, 1, ATT_GROUP * ng, ATT_GROUP * LANES), lambda b, h: (b, h, 0, 0))],
        out_shape=[jax.ShapeDtypeStruct((bsz, KV_HEADS, KV_WIDTH, ATT_GROUP * ng), BF16),
                   jax.ShapeDtypeStruct((bsz, KV_HEADS, ATT_GROUP * ng, ATT_GROUP * LANES), BF16)],
        scratch_shapes=[pltpu.VMEM((ng + 8, CMP_HIDDEN), F32)],
        compiler_params=_cparams(("parallel", "parallel")),
        name="compress",
    )(proj, proj, pek, pev, w1k, w1v, w2kt, w2v)


TQC = TQ
SUBLANES = 8


def _nsacmp_kernel(q_ref, kblk_ref, vblk_ref, cos_ref, s1_ref, s2_ref, ovt_ref, selq_ref, placen_ref, eye_ref,
                   qaug_ref, ocmp_ref):
    ncp = kblk_ref.shape[3] // ATT_GROUP
    nsel = ovt_ref.shape[0]
    nt = (((1,), (1,)), ((), ()))
    s0 = pl.program_id(2) * TQC
    q = q_ref[...]
    s_all = jnp.dot(q.astype(BF16), kblk_ref[0, 0], preferred_element_type=F32) * SCALE
    tq_pos = s0 + lax.broadcasted_iota(jnp.int32, (TQC, ncp), 0)
    cmp_end = lax.broadcasted_iota(jnp.int32, (TQC, ncp), 1) * CMP_STRIDE + (CMP_BLOCK - 1)
    mask = cmp_end <= tq_pos
    maskf = mask.astype(F32)
    ps = []
    for g in range(ATT_GROUP):
        s = jnp.where(mask, s_all[:, g * ncp:(g + 1) * ncp], NEG_INF)
        m = jnp.max(s, axis=-1, keepdims=True)
        e = jnp.exp(s - m)
        ps.append(e / jnp.sum(e, axis=-1, keepdims=True) * maskf)
    p_all = jnp.concatenate(ps, axis=1)

    def stack_rows(wide):
        return jnp.concatenate([wide[u * TQ:(u + 1) * TQ, g * LANES:(g + 1) * LANES]
                                for u in range(TQC // TQ) for g in range(ATT_GROUP)], axis=0)

    ow = jnp.dot(p_all.astype(BF16), vblk_ref[0, 0], preferred_element_type=F32)
    ocmp_ref[...] = _merge_heads([ow[:, g * LANES:(g + 1) * LANES] for g in range(ATT_GROUP)])

    psum = (ps[0] + ps[1]) + (ps[2] + ps[3])
    p_hi = psum.astype(BF16)
    p_lo = (psum - p_hi.astype(F32)).astype(BF16)
    ovt = ovt_ref[...]
    imp = (lax.dot_general(ovt, p_hi, nt, preferred_element_type=F32)
           + lax.dot_general(ovt, p_lo, nt, preferred_element_type=F32))
    j = lax.broadcasted_iota(jnp.int32, (nsel, TQC), 0)
    cur = (s0 + lax.broadcasted_iota(jnp.int32, (nsel, TQC), 1)) // SEL_BLOCK
    imp = jnp.where((j == 0) | (j == cur) | (j == cur - 1), FORCE_SCORE, imp)
    imp = jnp.where(j <= cur, imp, -1.0)
    nblk = nsel // SUBLANES
    blocks = [imp[k * SUBLANES:(k + 1) * SUBLANES, :] for k in range(nblk)]
    ranks = [jnp.zeros((SUBLANES, TQC), F32) for _ in range(nblk)]
    sub = lax.broadcasted_iota(jnp.int32, (SUBLANES, TQC), 0)
    for jp in range(nsel):
        r = jnp.broadcast_to(imp[jp:jp + 1, :], (SUBLANES, TQC))
        for k in range(nblk):
            if k > jp // SUBLANES:
                inc = jnp.where(r >= blocks[k], 1.0, 0.0)
            elif k < jp // SUBLANES:
                inc = jnp.where(r > blocks[k], 1.0, 0.0)
            else:
                tie = jnp.where(sub > jp % SUBLANES, 1.0, 0.0)
                inc = jnp.where(r > blocks[k], 1.0, jnp.where(r == blocks[k], tie, 0.0))
            ranks[k] = ranks[k] + inc
    rank = jnp.concatenate(ranks, axis=0)
    notsel_t = jnp.where((rank < float(N_SELECT)) & (imp >= 0.0), 0.0, 1.0).astype(BF16)
    notsel = lax.dot_general(eye_ref[...], notsel_t, nt, preferred_element_type=F32)

    q_rot = (_rope(q, cos_ref[...], s1_ref[...], s2_ref[...]) * SCALE).astype(BF16)
    qw = (jnp.dot(q_rot, selq_ref[...], preferred_element_type=F32)
          + jnp.dot(notsel.astype(BF16), placen_ref[...], preferred_element_type=F32))
    qaug_ref[0, 0] = stack_rows(qw).astype(BF16)


def _selection_overlap_t(n_cmp_pad, n_cmp, n_sel):
    cs = np.arange(n_cmp)[:, None] * CMP_STRIDE
    ce = cs + CMP_BLOCK
    ss = np.arange(n_sel)[None, :] * SEL_BLOCK
    se = ss + SEL_BLOCK
    ov = np.clip(np.minimum(ce, se) - np.maximum(cs, ss), 0, None) / CMP_BLOCK
    full = np.zeros((n_cmp_pad, n_sel), np.float32)
    full[:n_cmp] = ov
    return full.T.copy()


def _nsa_cmp(proj, kblk, vblk, bsz, seq, cos_t, s1_t, s2_t):
    nqc = seq // TQC
    ncp = kblk.shape[3] // ATT_GROUP
    n_cmp = (seq - CMP_BLOCK) // CMP_STRIDE + 1
    nsel = seq // SEL_BLOCK
    ovt = jnp.asarray(_selection_overlap_t(ncp, n_cmp, nsel), BF16)
    selq = jnp.asarray(_head_place_matrix(), BF16)
    placen = np.zeros((nsel, ATT_GROUP * LANES), np.float32)
    for g in range(ATT_GROUP):
        for jb in range(nsel):
            placen[jb, g * LANES + HEAD_DIM + jb] = 1.0
    placen = jnp.asarray(placen, BF16)
    eye = jnp.asarray(np.eye(TQC, dtype=np.float32), BF16)
    const = lambda shape: pl.BlockSpec(shape, lambda b, h, t: (0,) * len(shape))
    tab = pl.BlockSpec((TQC, KV_WIDTH), lambda b, h, t: (t, 0))
    qb = COL_Q // KV_WIDTH
    rows = (TQC // TQ) * ROWS
    stacked = pl.BlockSpec((1, 1, rows, LANES), lambda b, h, t: (b, h, t, 0))
    return pl.pallas_call(
        _nsacmp_kernel,
        grid=(bsz, KV_HEADS, nqc),
        in_specs=[
            pl.BlockSpec((TQC, KV_WIDTH), lambda b, h, t: (b * nqc + t, qb + h)),
            pl.BlockSpec((1, 1, KV_WIDTH, ATT_GROUP * ncp), lambda b, h, t: (b, h, 0, 0)),
            pl.BlockSpec((1, 1, ATT_GROUP * ncp, ATT_GROUP * LANES), lambda b, h, t: (b, h, 0, 0)),
            tab, tab, tab,
            const((nsel, ncp)), const((KV_WIDTH, ATT_GROUP * LANES)),
            const((nsel, ATT_GROUP * LANES)), const((TQC, TQC)),
        ],
        out_specs=[stacked, pl.BlockSpec((TQC, KV_WIDTH), lambda b, h, t: (b * nqc + t, h))],
        out_shape=[jax.ShapeDtypeStruct((bsz, KV_HEADS, nqc * rows, LANES), BF16),
                   jax.ShapeDtypeStruct((bsz * seq, ATT_WIDTH), F32)],
        compiler_params=_cparams(("parallel", "parallel", "parallel")),
        name="nsa_cmp",
    )(proj, kblk, vblk, cos_t, s1_t, s2_t, ovt, selq, placen, eye)


def _nsaattn_kernel(qaug_ref, ocmp_ref, ks_ref, vs_ref, kw_ref, vw_ref, gate_ref, gexp_ref, db_ref, wb_ref, o_ref,
                    m_ref, acc_ref):
    qt = pl.program_id(2)
    s0 = qt * TQ
    q = qaug_ref[0, 0]
    nt = (((1,), (1,)), ((), ()))

    def per_head(bias):
        return jnp.concatenate([bias] * ATT_GROUP, axis=0)

    m_ref[...] = jnp.full(m_ref.shape, NEG_INF, F32)
    acc_ref[...] = jnp.zeros(acc_ref.shape, F32)

    def tile(first_key, width, causal=False):
        start = pl.multiple_of(first_key, width)
        k = ks_ref[pl.ds(start, width), :]
        v = vs_ref[pl.ds(start, width), :]
        s = lax.dot_general(q, k, nt, preferred_element_type=F32)
        if causal:
            s = s + per_head(db_ref[0])
        m_prev = m_ref[...]
        m_next = jnp.maximum(m_prev, jnp.max(s, axis=-1, keepdims=True))
        p = jnp.exp(s - jnp.concatenate([m_next] * (width // LANES), axis=1))
        acc_ref[...] = acc_ref[...] * jnp.exp(m_prev - m_next) + jnp.dot(
            p.astype(BF16), v, preferred_element_type=F32)
        m_ref[...] = m_next

    nfull = (qt * TQ) // TK

    def body(i, carry):
        tile(i * (2 * TK), 2 * TK)
        return carry

    lax.fori_loop(0, nfull // 2, body, 0)

    @pl.when(nfull % 2 == 1)
    def _():
        tile((nfull - 1) * TK, TK)

    tile(nfull * TK, TK, causal=True)
    acc = acc_ref[...]

    wk = WINDOW + TQ
    wstart = pl.multiple_of(jnp.maximum(s0 - WINDOW, 0), TQ)
    kwin = kw_ref[pl.ds(wstart, wk), :]
    vwin = vw_ref[pl.ds(wstart, wk), :]
    sw = lax.dot_general(q, kwin, nt, preferred_element_type=F32) + per_head(wb_ref[0])
    pw = jnp.exp(sw - jnp.max(sw, axis=-1, keepdims=True))
    ow = jnp.dot(pw.astype(BF16), vwin, preferred_element_type=F32)

    def normalised(x):
        parts = [x[g * TQ:(g + 1) * TQ] for g in range(ATT_GROUP)]
        return _merge_heads(parts) / _merge_heads(parts, upper=True)

    sig = jax.nn.sigmoid(gate_ref[...])
    gexp = gexp_ref[0]
    gmap = sum(jnp.dot(t, gexp, preferred_element_type=F32) for t in _split_bf16(sig, 2))
    y = (gmap[:, 0:KV_WIDTH] * ocmp_ref[...] + gmap[:, KV_WIDTH:2 * KV_WIDTH] * normalised(acc)
         + gmap[:, 2 * KV_WIDTH:3 * KV_WIDTH] * normalised(ow))
    o_ref[...] = y.astype(o_ref.dtype)


def _nsa_attn(qaug, ocmp, ksa, vsa, kwa, vwa, proj, bsz, seq):
    nqt = seq // TQ
    gexp = np.zeros((KV_HEADS, LANES, N_BRANCH * KV_WIDTH), np.float32)
    for h in range(KV_HEADS):
        for g in range(ATT_GROUP):
            for br in range(N_BRANCH):
                src = (h * ATT_GROUP + g) * N_BRANCH + br
                gexp[h, src, br * KV_WIDTH + g * HEAD_DIM:br * KV_WIDTH + (g + 1) * HEAD_DIM] = 1.0
    gexp = jnp.asarray(gexp, BF16)
    r = np.arange(TQ)[:, None]
    diag_cases = TK // TQ
    db = np.stack([np.where(np.arange(TK)[None, :] <= c * TQ + r, 0.0, NEG_INF) for c in range(diag_cases)])
    win_cases = WINDOW // TQ + 1
    wk = WINDOW + TQ
    dist = lambda c: c * TQ + r - np.arange(wk)[None, :]
    wb = np.stack([np.where((dist(c) >= 0) & (dist(c) < WINDOW), 0.0, NEG_INF) for c in range(win_cases)])
    db = jnp.asarray(db, F32)
    wb = jnp.asarray(wb, F32)
    stacked = pl.BlockSpec((1, 1, ROWS, LANES), lambda b, h, t: (b, h, t, 0))
    kv = pl.BlockSpec((seq, LANES), lambda b, h, t: (b, h))
    return pl.pallas_call(
        _nsaattn_kernel,
        grid=(bsz, KV_HEADS, nqt),
        in_specs=[stacked, pl.BlockSpec((TQ, KV_WIDTH), lambda b, h, t: (b * nqt + t, h)), kv, kv, kv, kv,
                  pl.BlockSpec((TQ, LANES), lambda b, h, t: (b * nqt + t, COL_GATE // LANES)),
                  pl.BlockSpec((1, LANES, N_BRANCH * KV_WIDTH), lambda b, h, t: (h, 0, 0)),
                  pl.BlockSpec((1, TQ, TK), lambda b, h, t: (t % diag_cases, 0, 0)),
                  pl.BlockSpec((1, TQ, wk), lambda b, h, t: (jnp.minimum(t, win_cases - 1), 0, 0))],
        out_specs=pl.BlockSpec((TQ, KV_WIDTH), lambda b, h, t: (b * nqt + t, h)),
        out_shape=jax.ShapeDtypeStruct((bsz * seq, ATT_WIDTH), BF16),
        scratch_shapes=[pltpu.VMEM((ROWS, LANES), F32), pltpu.VMEM((ROWS, LANES), F32)],
        compiler_params=_cparams(("parallel", "parallel", "arbitrary")),
        name="nsa_attn",
    )(qaug, ocmp, ksa, vsa, kwa, vwa, proj, gexp, db, wb)


def _outproj_kernel(x_ref, ys_ref, ya_ref, w1_ref, w2_ref, o_ref):
    o_ref[...] = (x_ref[...] + jnp.dot(ys_ref[...], w1_ref[...], preferred_element_type=F32)
                  + jnp.dot(ya_ref[...], w2_ref[...], preferred_element_type=F32))


def _out_proj(x2, ys, ya, wo_b):
    t = x2.shape[0]
    tm = 512
    return pl.pallas_call(
        _outproj_kernel,
        grid=(t // tm,),
        in_specs=[
            pl.BlockSpec((tm, D_MODEL), lambda i: (i, 0)),
            pl.BlockSpec((tm, SSD_WIDTH), lambda i: (i, 0)),
            pl.BlockSpec((tm, ATT_WIDTH), lambda i: (i, 0)),
            pl.BlockSpec((SSD_WIDTH, D_MODEL), lambda i: (0, 0)),
            pl.BlockSpec((ATT_WIDTH, D_MODEL), lambda i: (1, 0)),
        ],
        out_specs=pl.BlockSpec((tm, D_MODEL), lambda i: (i, 0)),
        out_shape=jax.ShapeDtypeStruct((t, D_MODEL), F32),
        compiler_params=_cparams(("parallel",)),
        name="out_proj",
    )(x2, ys, ya, wo_b, wo_b)


def _ffn_kernel(h_ref, nw_ref, fw_ref, wg_ref, wu_ref, wd_ref, o_ref, v_ref):
    j = pl.program_id(1)

    @pl.when(j == 0)
    def _():
        h = h_ref[...]
        ms = jnp.mean(h * h, axis=-1, keepdims=True)
        v_ref[...] = ((h * lax.rsqrt(ms + NORM_EPS)) * nw_ref[...]).astype(BF16)
        o_ref[...] = h

    v = v_ref[...]
    gate = jnp.dot(v, wg_ref[...].astype(BF16), preferred_element_type=F32)
    up = jnp.dot(v, wu_ref[...].astype(BF16), preferred_element_type=F32)
    o_ref[...] += jnp.dot((_silu(gate) * up).astype(BF16), wd_ref[...].astype(BF16), preferred_element_type=F32)

    @pl.when(j == pl.num_programs(1) - 1)
    def _():
        h2 = o_ref[...]
        ms = jnp.mean(h2 * h2, axis=-1, keepdims=True)
        o_ref[...] = (h2 * lax.rsqrt(ms + NORM_EPS)) * fw_ref[...]


def _ffn(h1, ffn_nw, final_w, wg_b, wu_b, wd_b):
    t = h1.shape[0]
    tm, tf = 1024, 256
    return pl.pallas_call(
        _ffn_kernel,
        grid=(t // tm, D_FF // tf),
        in_specs=[
            pl.BlockSpec((tm, D_MODEL), lambda i, j: (i, 0)),
            pl.BlockSpec((1, D_MODEL), lambda i, j: (0, 0)),
            pl.BlockSpec((1, D_MODEL), lambda i, j: (0, 0)),
            pl.BlockSpec((D_MODEL, tf), lambda i, j: (0, j)),
            pl.BlockSpec((D_MODEL, tf), lambda i, j: (0, j)),
            pl.BlockSpec((tf, D_MODEL), lambda i, j: (j, 0)),
        ],
        out_specs=pl.BlockSpec((tm, D_MODEL), lambda i, j: (i, 0)),
        out_shape=jax.ShapeDtypeStruct((t, D_MODEL), F32),
        scratch_shapes=[pltpu.VMEM((tm, D_MODEL), BF16)],
        compiler_params=_cparams(("parallel", "arbitrary")),
        name="ffn",
    )(h1, ffn_nw, final_w, wg_b, wu_b, wd_b)


def _rope_tables(seq):
    inv = 1.0 / (ROPE_THETA ** (jnp.arange(0, ROPE_DIM, 2, dtype=F32) / ROPE_DIM))
    ang = jnp.arange(seq, dtype=F32)[:, None] * inv[None, :]
    cos, sin = jnp.cos(ang), jnp.sin(ang)
    half = ROPE_DIM // 2
    rest_one = jnp.ones((seq, HEAD_DIM - ROPE_DIM), F32)
    rest_zero = jnp.zeros((seq, HEAD_DIM - ROPE_DIM), F32)
    zero_h = jnp.zeros((seq, half), F32)
    cos_h = jnp.concatenate([cos, cos, rest_one], axis=1)
    s1_h = jnp.concatenate([-sin, zero_h, rest_zero], axis=1)
    s2_h = jnp.concatenate([zero_h, sin, rest_zero], axis=1)
    tile = lambda a: jnp.tile(a, (1, KV_HEADS))
    return tile(cos_h), tile(s1_h), tile(s2_h)


def _pad_lanes(a, width):
    return jnp.pad(a, ((0, 0), (0, width - a.shape[1])))


def _layer(h2d, bsz, seq, p):
    (attn_norm_w, w_in, conv_w, conv_b, dt_bias, a_log, d_skip, ssd_norm_w, cmp_w1_k, cmp_w2_k, cmp_w1_v,
     cmp_w2_v, cmp_pe_k, cmp_pe_v, w_out, ffn_norm_w, w_gate, w_up, w_down) = p
    o_xbc, o_dt, o_q, o_kv, o_gate = 1024, 2560, 2576, 3600, 5136
    w_perm = jnp.concatenate([
        w_in[:, o_xbc:o_dt], w_in[:, o_kv:o_gate], w_in[:, :o_xbc], w_in[:, o_q:o_kv],
        _pad_lanes(w_in[:, o_dt:o_q], LANES), _pad_lanes(w_in[:, o_gate:], LANES)], axis=1).astype(BF16)
    proj = _in_proj(h2d, attn_norm_w[None, :], w_perm)

    y_ssd = _ssd(proj, bsz, seq, conv_w, conv_b[None, :], _pad_lanes(dt_bias[None, :], LANES),
                 _pad_lanes(a_log[None, :], LANES), jnp.repeat(d_skip, SSD_HEAD_DIM)[None, :],
                 ssd_norm_w[None, :])

    cos_t, s1_t, s2_t = _rope_tables(seq)
    ksa, vsa, kwa, vwa = _kv_prep(proj, seq, cos_t, s1_t, s2_t)

    hpb = LANES // HEAD_DIM
    pe_rep = lambda pe: jnp.tile(pe, (1, hpb))
    w1_rep = lambda w1: jnp.tile(w1.astype(BF16).reshape(CMP_BLOCK, HEAD_DIM, CMP_HIDDEN), (1, hpb, 1))
    kblk, vblk = _compress(proj, bsz, seq, pe_rep(cmp_pe_k), pe_rep(cmp_pe_v),
                           w1_rep(cmp_w1_k), w1_rep(cmp_w1_v), cmp_w2_k.T.astype(BF16),
                           _pad_lanes(cmp_w2_v, LANES).astype(BF16))
    qaug, ocmp = _nsa_cmp(proj, kblk, vblk, bsz, seq, cos_t, s1_t, s2_t)
    y_att = _nsa_attn(qaug, ocmp, ksa, vsa, kwa, vwa, proj, bsz, seq)

    h1 = _out_proj(h2d, y_ssd, y_att, w_out.astype(BF16))
    return h1, (ffn_norm_w, w_gate, w_up, w_down)


def kernel(x, attn_norm_w, w_in, conv_w, conv_b, dt_bias, a_log, d_skip, ssd_norm_w, cmp_w1_k, cmp_w2_k,
           cmp_w1_v, cmp_w2_v, cmp_pe_k, cmp_pe_v, w_out, ffn_norm_w, w_gate, w_up, w_down, final_norm_w):
    bsz, seq, _ = x.shape
    depth = w_in.shape[0]
    assert depth == 1, "the final rmsnorm is fused into the last layer's ffn kernel"
    h = x.reshape(bsz * seq, D_MODEL)
    l = 0
    params = (attn_norm_w[l], w_in[l], conv_w[l], conv_b[l], dt_bias[l], a_log[l], d_skip[l], ssd_norm_w[l],
              cmp_w1_k[l], cmp_w2_k[l], cmp_w1_v[l], cmp_w2_v[l], cmp_pe_k[l], cmp_pe_v[l], w_out[l],
              ffn_norm_w[l], w_gate[l], w_up[l], w_down[l])
    h1, (fnw, wg_b, wu_b, wd_b) = _layer(h, bsz, seq, params)
    out = _ffn(h1, fnw[None, :], final_norm_w[None, :], wg_b, wu_b, wd_b)
    return out.reshape(bsz, seq, D_MODEL)
```

```python
import functools

import numpy as np
import jax
import jax.numpy as jnp
from jax import lax
from jax.experimental import pallas as pl
from jax.experimental.pallas import tpu as pltpu

F32 = jnp.float32
BF16 = jnp.bfloat16
HI = lax.Precision.HIGHEST

D_MODEL = 2048
SSD_WIDTH = 1024
ATT_WIDTH = 1024
SSD_HEAD_DIM = 64
SSD_HEADS = 16
SSD_GROUPS = 2
SSD_STATE = 128
SSD_CHUNK = 128
CONV_WIDTH = 4
CONV_CH = SSD_WIDTH + 2 * SSD_GROUPS * SSD_STATE
HEAD_DIM = 64
ATT_HEADS = 16
KV_HEADS = 4
ATT_GROUP = 4
KV_WIDTH = KV_HEADS * HEAD_DIM
CMP_BLOCK = 32
CMP_STRIDE = 16
CMP_HIDDEN = 256
SEL_BLOCK = 64
N_SELECT = 16
WINDOW = 512
N_BRANCH = 3
ROPE_THETA = 500000.0
ROPE_DIM = 16
D_FF = 5632
NORM_EPS = 1e-6
NEG_INF = -1e30
FORCE_SCORE = 1e4
SCALE = HEAD_DIM ** -0.5
BLOCK_BIAS = -(2.0 ** 100)

LANES = 128
VMEM_LIMIT = 56 * 1024 * 1024

NP = 5376
COL_XBC = 0
COL_KV = 1536
COL_Z = 3072
COL_Q = 4096
COL_DT = 5120
COL_GATE = 5248

TQ = 256
TK = 512
ROWS = ATT_GROUP * TQ


def _silu(x):
    return x * jax.nn.sigmoid(x)


def _split_bf16(x, terms):
    out = []
    for _ in range(terms - 1):
        t = x.astype(BF16)
        out.append(t)
        x = x - t.astype(F32)
    out.append(x.astype(BF16))
    return out


def _cparams(sem):
    return pltpu.CompilerParams(dimension_semantics=sem, vmem_limit_bytes=VMEM_LIMIT)


def _inproj_kernel(x_ref, nw_ref, w_ref, o_ref, u_ref):
    @pl.when(pl.program_id(1) == 0)
    def _():
        x = x_ref[...]
        ms = jnp.mean(x * x, axis=-1, keepdims=True)
        u_ref[...] = ((x * lax.rsqrt(ms + NORM_EPS)) * nw_ref[...]).astype(BF16)

    o_ref[...] = jnp.dot(u_ref[...], w_ref[...], preferred_element_type=F32)


def _in_proj(x2, norm_w, w_perm):
    t = x2.shape[0]
    tm, tn = 1024, 768
    return pl.pallas_call(
        _inproj_kernel,
        grid=(t // tm, NP // tn),
        in_specs=[
            pl.BlockSpec((tm, D_MODEL), lambda i, j: (i, 0)),
            pl.BlockSpec((1, D_MODEL), lambda i, j: (0, 0)),
            pl.BlockSpec((D_MODEL, tn), lambda i, j: (0, j)),
        ],
        out_specs=pl.BlockSpec((tm, tn), lambda i, j: (i, j)),
        out_shape=jax.ShapeDtypeStruct((t, NP), F32),
        scratch_shapes=[pltpu.VMEM((tm, D_MODEL), BF16)],
        compiler_params=_cparams(("parallel", "arbitrary")),
        name="in_proj",
    )(x2, norm_w, w_perm)


def _ssd_kernel(xbc_ref, z_ref, dt_ref, cw_ref, cb_ref, dtb_ref, alog_ref, dskip_ref, nw_ref,
                ltri_ref, ex_ref, o_ref, ext_ref, state_ref, y_ref):
    L = SSD_CHUNK
    c = pl.program_id(1)

    @pl.when(c == 0)
    def _():
        ext_ref[0:8, :] = jnp.zeros((8, CONV_CH), F32)
        state_ref[...] = jnp.zeros(state_ref.shape, F32)

    ext_ref[8:8 + L, :] = xbc_ref[...]
    w = cw_ref[...]
    y = (ext_ref[5:5 + L, :] * w[0:1, :] + ext_ref[6:6 + L, :] * w[1:2, :]
         + ext_ref[7:7 + L, :] * w[2:3, :] + ext_ref[8:8 + L, :] * w[3:4, :]) + cb_ref[...]
    tail = ext_ref[L:L + 8, :]
    ext_ref[0:8, :] = tail
    act = _silu(y)
    xs = act[:, :SSD_WIDTH]
    bm = act[:, SSD_WIDTH:SSD_WIDTH + SSD_GROUPS * SSD_STATE]
    cm = act[:, SSD_WIDTH + SSD_GROUPS * SSD_STATE:]

    v = dt_ref[...] + dtb_ref[...]
    dt = jnp.maximum(v, 0.0) + jnp.log1p(jnp.exp(-jnp.abs(v)))
    a = -jnp.exp(alog_ref[...])
    adt = a * dt
    ltri = ltri_ref[...]
    acum = sum(jnp.dot(ltri, t, preferred_element_type=F32) for t in _split_bf16(adt, 3))
    acum_t = acum.T
    last = acum[L - 1:L, :]
    stacked = jnp.concatenate([dt, jnp.exp(acum), jnp.exp(last - acum)], axis=0)
    ex = ex_ref[...]
    expanded = sum(jnp.dot(t, ex, preferred_element_type=F32) for t in _split_bf16(stacked, 2))
    dt_e = expanded[0:L]
    expa_e = expanded[L:2 * L]
    dst_e = expanded[2 * L:3 * L]

    xdt = xs * dt_e
    xds_b = (xdt * dst_e).astype(BF16)
    row = lax.broadcasted_iota(jnp.int32, (L, L), 0)
    col = lax.broadcasted_iota(jnp.int32, (L, L), 1)
    causal = row >= col
    lane = lax.broadcasted_iota(jnp.int32, (L, LANES), 1)
    hg = SSD_HEADS // SSD_GROUPS
    gw = hg * SSD_HEAD_DIM
    for g in range(SSD_GROUPS):
        bg = bm[:, g * SSD_STATE:(g + 1) * SSD_STATE]
        cg_b = cm[:, g * SSD_STATE:(g + 1) * SSD_STATE].astype(BF16)
        bg_b = bg.astype(BF16)
        cb = lax.dot_general(cg_b, bg_b, (((1,), (1,)), ((), ())), preferred_element_type=F32)
        hprev = state_ref[g]
        yoff = jnp.dot(cg_b, hprev.astype(BF16), preferred_element_type=F32) * expa_e[:, g * gw:(g + 1) * gw]
        snew = jnp.dot(bg.T.astype(BF16), xds_b[:, g * gw:(g + 1) * gw], preferred_element_type=F32)
        state_ref[g] = hprev * expa_e[L - 1:L, g * gw:(g + 1) * gw] + snew
        for k in range(hg // 2):
            pair = g * (hg // 2) + k
            ms = []
            for hh in (2 * pair, 2 * pair + 1):
                seg = acum[:, hh:hh + 1] - acum_t[hh:hh + 1, :]
                decay = jnp.exp(jnp.where(causal, seg, -jnp.inf))
                ms.append((cb * decay).astype(BF16))
            mpair = jnp.concatenate(ms, axis=1)
            xp = xdt[:, pair * LANES:(pair + 1) * LANES]
            xblk = jnp.concatenate([jnp.where(lane < SSD_HEAD_DIM, xp, 0.0),
                                    jnp.where(lane >= SSD_HEAD_DIM, xp, 0.0)], axis=0).astype(BF16)
            yd = jnp.dot(mpair, xblk, preferred_element_type=F32)
            y_ref[:, pair * LANES:(pair + 1) * LANES] = yd + yoff[:, k * LANES:(k + 1) * LANES]

    yy = y_ref[...] + dskip_ref[...] * xs
    z = z_ref[...]
    yz = yy * _silu(z)
    ms2 = jnp.mean(yz * yz, axis=-1, keepdims=True)
    o_ref[...] = ((yz * lax.rsqrt(ms2 + NORM_EPS)) * nw_ref[...]).astype(o_ref.dtype)


def _ssd(proj, bsz, seq, conv_w, conv_b, dtb_pad, alog_pad, dskip_e, norm_w):
    nc = seq // SSD_CHUNK
    L = SSD_CHUNK
    ltri = jnp.asarray(np.tril(np.ones((L, L), np.float32)), BF16)
    ex = np.zeros((LANES, SSD_WIDTH), np.float32)
    for h in range(SSD_HEADS):
        ex[h, h * SSD_HEAD_DIM:(h + 1) * SSD_HEAD_DIM] = 1.0
    ex = jnp.asarray(ex, BF16)
    const = lambda shape: pl.BlockSpec(shape, lambda b, c: (0,) * len(shape))
    return pl.pallas_call(
        _ssd_kernel,
        grid=(bsz, nc),
        in_specs=[
            pl.BlockSpec((L, CONV_CH), lambda b, c: (b * nc + c, COL_XBC // CONV_CH)),
            pl.BlockSpec((L, SSD_WIDTH), lambda b, c: (b * nc + c, COL_Z // SSD_WIDTH)),
            pl.BlockSpec((L, LANES), lambda b, c: (b * nc + c, COL_DT // LANES)),
            const((CONV_WIDTH, CONV_CH)), const((1, CONV_CH)), const((1, LANES)), const((1, LANES)),
            const((1, SSD_WIDTH)), const((1, SSD_WIDTH)), const((L, L)), const((LANES, SSD_WIDTH)),
        ],
        out_specs=pl.BlockSpec((L, SSD_WIDTH), lambda b, c: (b * nc + c, 0)),
        out_shape=jax.ShapeDtypeStruct((bsz * seq, SSD_WIDTH), BF16),
        scratch_shapes=[
            pltpu.VMEM((L + 8, CONV_CH), F32),
            pltpu.VMEM((SSD_GROUPS, SSD_STATE, (SSD_HEADS // SSD_GROUPS) * SSD_HEAD_DIM), F32),
            pltpu.VMEM((L, SSD_WIDTH), F32),
        ],
        compiler_params=_cparams(("parallel", "arbitrary")),
        name="ssd",
    )(proj, proj, proj, conv_w, conv_b, dtb_pad, alog_pad, dskip_e, norm_w, ltri, ex)


def _rope(x, cos, s1, s2):
    n = x.shape[-1]
    half = ROPE_DIM // 2
    return x * cos + pltpu.roll(x, n - half, 1) * s1 + pltpu.roll(x, half, 1) * s2


def _merge_heads(parts, upper=False):
    lane = lax.broadcasted_iota(jnp.int32, parts[0].shape, 1)
    out = []
    for k in range(len(parts) // 2):
        a, b = parts[2 * k], parts[2 * k + 1]
        if upper:
            out.append(jnp.where(lane < HEAD_DIM, pltpu.roll(a, HEAD_DIM, 1), b))
        else:
            out.append(jnp.where(lane < HEAD_DIM, a, pltpu.roll(b, HEAD_DIM, 1)))
    return jnp.concatenate(out, axis=1)


def _kvprep_kernel(ks_ref, vs_ref, kw_ref, vw_ref, cos_ref, s1_ref, s2_ref, sel_ref,
                   kso_ref, vso_ref, kwo_ref, vwo_ref, *, tiles_per_seq):
    tr = ks_ref.shape[0]
    wide = KV_HEADS * LANES
    s0 = (pl.program_id(0) % tiles_per_seq) * tr
    cos, s1, s2 = cos_ref[...], s1_ref[...], s2_ref[...]
    sel = sel_ref[...]
    lane = lax.broadcasted_iota(jnp.int32, (tr, wide), 1) % LANES
    blk = (s0 + lax.broadcasted_iota(jnp.int32, (tr, wide), 0)) // SEL_BLOCK
    ebias = jnp.where(lane - SEL_BLOCK == blk, BLOCK_BIAS, 0.0)
    ones = jnp.where(lane >= HEAD_DIM, 1.0, 0.0)

    def place(x):
        return jnp.dot(x.astype(BF16), sel, preferred_element_type=F32)

    kso_ref[...] = (place(_rope(ks_ref[...], cos, s1, s2)) + ebias).astype(BF16)
    kwo_ref[...] = place(_rope(kw_ref[...], cos, s1, s2)).astype(BF16)
    vso_ref[...] = (place(vs_ref[...]) + ones).astype(BF16)
    vwo_ref[...] = (place(vw_ref[...]) + ones).astype(BF16)


def _head_place_matrix():
    m = np.zeros((KV_WIDTH, KV_HEADS * LANES), np.float32)
    for h in range(KV_HEADS):
        for d in range(HEAD_DIM):
            m[h * HEAD_DIM + d, h * LANES + d] = 1.0
    return m


def _kv_prep(proj, seq, cos_t, s1_t, s2_t):
    t = proj.shape[0]
    tr = 512
    tps = seq // tr
    sel = jnp.asarray(_head_place_matrix(), BF16)
    kvb = COL_KV // KV_WIDTH
    seg = lambda k: pl.BlockSpec((tr, KV_WIDTH), lambda i: (i, kvb + k))
    tab = pl.BlockSpec((tr, KV_WIDTH), lambda i: (i % tps, 0))
    wide = KV_HEADS * LANES
    out = pl.BlockSpec((tr, wide), lambda i: (i, 0))
    shp = jax.ShapeDtypeStruct((t, wide), BF16)
    return pl.pallas_call(
        functools.partial(_kvprep_kernel, tiles_per_seq=tps),
        grid=(t // tr,),
        in_specs=[seg(2), seg(3), seg(4), seg(5), tab, tab, tab,
                  pl.BlockSpec((KV_WIDTH, wide), lambda i: (0, 0))],
        out_specs=[out, out, out, out],
        out_shape=[shp, shp, shp, shp],
        compiler_params=_cparams(("parallel",)),
        name="kv_prep",
    )(proj, proj, proj, proj, cos_t, s1_t, s2_t, sel)


def _compress_kernel(gk_ref, gv_ref, pek_ref, pev_ref, w1k_ref, w1v_ref, w2kt_ref, w2v_ref,
                     kblk_ref, vblk_ref, shift_ref):
    ng = gk_ref.shape[0] // CMP_STRIDE
    heads_per_block = LANES // HEAD_DIM
    mine = (lax.broadcasted_iota(jnp.int32, (ng, LANES), 1) // HEAD_DIM) == pl.program_id(1) % heads_per_block

    def hidden(x_ref, pe_ref, w1_ref):
        top = jnp.zeros((ng, CMP_HIDDEN), F32)
        bot = jnp.zeros((ng, CMP_HIDDEN), F32)
        for r in range(CMP_STRIDE):
            x = x_ref[pl.ds(r, ng, stride=CMP_STRIDE), :]
            xt = jnp.where(mine, x + pe_ref[r:r + 1, :], 0.0).astype(BF16)
            xb = jnp.where(mine, x + pe_ref[CMP_STRIDE + r:CMP_STRIDE + r + 1, :], 0.0).astype(BF16)
            top = top + jnp.dot(xt, w1_ref[r], preferred_element_type=F32)
            bot = bot + jnp.dot(xb, w1_ref[CMP_STRIDE + r], preferred_element_type=F32)
        shift_ref[0:ng, :] = bot
        shift_ref[ng:ng + 8, :] = jnp.zeros((8, CMP_HIDDEN), F32)
        return _silu(top + shift_ref[1:ng + 1, :]).astype(BF16)

    hk = hidden(gk_ref, pek_ref, w1k_ref)
    kt = lax.dot_general(w2kt_ref[...], hk, (((1,), (1,)), ((), ())), preferred_element_type=F32)
    kblk_ref[...] = jnp.zeros(kblk_ref.shape, BF16)
    for g in range(ATT_GROUP):
        kblk_ref[0, 0, g * HEAD_DIM:(g + 1) * HEAD_DIM, g * ng:(g + 1) * ng] = kt.astype(BF16)

    hv = hidden(gv_ref, pev_ref, w1v_ref)
    vc = jnp.dot(hv, w2v_ref[...], preferred_element_type=F32)
    vblk_ref[...] = jnp.zeros(vblk_ref.shape, BF16)
    for g in range(ATT_GROUP):
        vblk_ref[0, 0, g * ng:(g + 1) * ng, g * LANES:(g + 1) * LANES] = vc.astype(BF16)


def _compress(proj, bsz, seq, pek, pev, w1k, w1v, w2kt, w2v):
    ng = seq // CMP_STRIDE
    hpb = LANES // HEAD_DIM
    kcb = COL_KV // LANES
    vcb = kcb + KV_WIDTH // LANES
    const = lambda shape: pl.BlockSpec(shape, lambda b, h: (0,) * len(shape))
    return pl.pallas_call(
        _compress_kernel,
        grid=(bsz, KV_HEADS),
        in_specs=[pl.BlockSpec((seq, LANES), lambda b, h: (b, kcb + h // hpb)),
                  pl.BlockSpec((seq, LANES), lambda b, h: (b, vcb + h // hpb)),
                  const((CMP_BLOCK, LANES)), const((CMP_BLOCK, LANES)),
                  const((CMP_BLOCK, LANES, CMP_HIDDEN)), const((CMP_BLOCK, LANES, CMP_HIDDEN)),
                  const((HEAD_DIM, CMP_HIDDEN)), const((CMP_HIDDEN, LANES))],
        out_specs=[pl.BlockSpec((1, 1, KV_WIDTH, ATT_GROUP * ng), lambda b, h: (b, h, 0, 0)),
                   pl.BlockSpec((1, 1, ATT_GROUP * ng, ATT_GROUP * LANES), lambda b, h: (b, h, 0, 0))],
        out_shape=[jax.ShapeDtypeStruct((bsz, KV_HEADS, KV_WIDTH, ATT_GROUP * ng), BF16),
                   jax.ShapeDtypeStruct((bsz, KV_HEADS, ATT_GROUP * ng, ATT_GROUP * LANES), BF16)],
        scratch_shapes=[pltpu.VMEM((ng + 8, CMP_HIDDEN), F32)],
        compiler_params=_cparams(("parallel", "parallel")),
        name="compress",
    )(proj, proj, pek, pev, w1k, w1v, w2kt, w2v)


TQC = TQ
SUBLANES = 8


def _nsacmp_kernel(q_ref, kblk_ref, vblk_ref, cos_ref, s1_ref, s2_ref, ovt_ref, selq_ref, placen_ref, eye_ref,
                   qaug_ref, ocmp_ref):
    ncp = kblk_ref.shape[3] // ATT_GROUP
    nsel = ovt_ref.shape[0]
    nt = (((1,), (1,)), ((), ()))
    s0 = pl.program_id(2) * TQC
    q = q_ref[...]
    s_all = jnp.dot((q * SCALE).astype(BF16), kblk_ref[0, 0], preferred_element_type=F32)
    tq_pos = s0 + lax.broadcasted_iota(jnp.int32, (TQC, ncp), 0)
    cmp_end = lax.broadcasted_iota(jnp.int32, (TQC, ncp), 1) * CMP_STRIDE + (CMP_BLOCK - 1)
    mask = cmp_end <= tq_pos
    any_visible = (s0 + lax.broadcasted_iota(jnp.int32, (TQC, 1), 0) >= CMP_BLOCK - 1).astype(F32)
    ps = []
    for g in range(ATT_GROUP):
        s = jnp.where(mask, s_all[:, g * ncp:(g + 1) * ncp], NEG_INF)
        m = jnp.max(s, axis=-1, keepdims=True)
        e = jnp.exp(s - m)
        ps.append(e * (any_visible / jnp.sum(e, axis=-1, keepdims=True)))
    p_all = jnp.concatenate(ps, axis=1)

    def stack_rows(wide):
        return jnp.concatenate([wide[u * TQ:(u + 1) * TQ, g * LANES:(g + 1) * LANES]
                                for u in range(TQC // TQ) for g in range(ATT_GROUP)], axis=0)

    ow = jnp.dot(p_all.astype(BF16), vblk_ref[0, 0], preferred_element_type=F32)
    ocmp_ref[...] = _merge_heads([ow[:, g * LANES:(g + 1) * LANES] for g in range(ATT_GROUP)])

    psum = (ps[0] + ps[1]) + (ps[2] + ps[3])
    p_hi = psum.astype(BF16)
    p_lo = (psum - p_hi.astype(F32)).astype(BF16)
    ovt = ovt_ref[...]
    imp = (lax.dot_general(ovt, p_hi, nt, preferred_element_type=F32)
           + lax.dot_general(ovt, p_lo, nt, preferred_element_type=F32))
    j = lax.broadcasted_iota(jnp.int32, (nsel, TQC), 0)
    cur = (s0 + lax.broadcasted_iota(jnp.int32, (nsel, TQC), 1)) // SEL_BLOCK
    imp = jnp.where((j == 0) | (j == cur) | (j == cur - 1), FORCE_SCORE, imp)
    imp = jnp.where(j <= cur, imp, -1.0)
    nblk = nsel // SUBLANES
    blocks = [imp[k * SUBLANES:(k + 1) * SUBLANES, :] for k in range(nblk)]
    ranks = [jnp.zeros((SUBLANES, TQC), F32) for _ in range(nblk)]
    sub = lax.broadcasted_iota(jnp.int32, (SUBLANES, TQC), 0)
    for jp in range(nsel):
        r = jnp.broadcast_to(imp[jp:jp + 1, :], (SUBLANES, TQC))
        for k in range(nblk):
            if k > jp // SUBLANES:
                inc = jnp.where(r >= blocks[k], 1.0, 0.0)
            elif k < jp // SUBLANES:
                inc = jnp.where(r > blocks[k], 1.0, 0.0)
            else:
                tie = jnp.where(sub > jp % SUBLANES, 1.0, 0.0)
                inc = jnp.where(r > blocks[k], 1.0, jnp.where(r == blocks[k], tie, 0.0))
            ranks[k] = ranks[k] + inc
    rank = jnp.concatenate(ranks, axis=0)
    notsel_t = jnp.where((rank < float(N_SELECT)) & (imp >= 0.0), 0.0, 1.0).astype(BF16)
    notsel = lax.dot_general(eye_ref[...], notsel_t, nt, preferred_element_type=F32)

    q_rot = (_rope(q, cos_ref[...], s1_ref[...], s2_ref[...]) * SCALE).astype(BF16)
    qw = (jnp.dot(q_rot, selq_ref[...], preferred_element_type=F32)
          + jnp.dot(notsel.astype(BF16), placen_ref[...], preferred_element_type=F32))
    qaug_ref[0, 0] = stack_rows(qw).astype(BF16)


def _selection_overlap_t(n_cmp_pad, n_cmp, n_sel):
    cs = np.arange(n_cmp)[:, None] * CMP_STRIDE
    ce = cs + CMP_BLOCK
    ss = np.arange(n_sel)[None, :] * SEL_BLOCK
    se = ss + SEL_BLOCK
    ov = np.clip(np.minimum(ce, se) - np.maximum(cs, ss), 0, None) / CMP_BLOCK
    full = np.zeros((n_cmp_pad, n_sel), np.float32)
    full[:n_cmp] = ov
    return full.T.copy()


def _nsa_cmp(proj, kblk, vblk, bsz, seq, cos_t, s1_t, s2_t):
    nqc = seq // TQC
    ncp = kblk.shape[3] // ATT_GROUP
    n_cmp = (seq - CMP_BLOCK) // CMP_STRIDE + 1
    nsel = seq // SEL_BLOCK
    ovt = jnp.asarray(_selection_overlap_t(ncp, n_cmp, nsel), BF16)
    selq = jnp.asarray(_head_place_matrix(), BF16)
    placen = np.zeros((nsel, ATT_GROUP * LANES), np.float32)
    for g in range(ATT_GROUP):
        for jb in range(nsel):
            placen[jb, g * LANES + HEAD_DIM + jb] = 1.0
    placen = jnp.asarray(placen, BF16)
    eye = jnp.asarray(np.eye(TQC, dtype=np.float32), BF16)
    const = lambda shape: pl.BlockSpec(shape, lambda b, h, t: (0,) * len(shape))
    tab = pl.BlockSpec((TQC, KV_WIDTH), lambda b, h, t: (t, 0))
    qb = COL_Q // KV_WIDTH
    rows = (TQC // TQ) * ROWS
    stacked = pl.BlockSpec((1, 1, rows, LANES), lambda b, h, t: (b, h, t, 0))
    return pl.pallas_call(
        _nsacmp_kernel,
        grid=(bsz, KV_HEADS, nqc),
        in_specs=[
            pl.BlockSpec((TQC, KV_WIDTH), lambda b, h, t: (b * nqc + t, qb + h)),
            pl.BlockSpec((1, 1, KV_WIDTH, ATT_GROUP * ncp), lambda b, h, t: (b, h, 0, 0)),
            pl.BlockSpec((1, 1, ATT_GROUP * ncp, ATT_GROUP * LANES), lambda b, h, t: (b, h, 0, 0)),
            tab, tab, tab,
            const((nsel, ncp)), const((KV_WIDTH, ATT_GROUP * LANES)),
            const((nsel, ATT_GROUP * LANES)), const((TQC, TQC)),
        ],
        out_specs=[stacked, pl.BlockSpec((TQC, KV_WIDTH), lambda b, h, t: (b * nqc + t, h))],
        out_shape=[jax.ShapeDtypeStruct((bsz, KV_HEADS, nqc * rows, LANES), BF16),
                   jax.ShapeDtypeStruct((bsz * seq, ATT_WIDTH), F32)],
        compiler_params=_cparams(("parallel", "parallel", "parallel")),
        name="nsa_cmp",
    )(proj, kblk, vblk, cos_t, s1_t, s2_t, ovt, selq, placen, eye)


def _nsaattn_kernel(qaug_ref, ocmp_ref, ks_ref, vs_ref, kw_ref, vw_ref, gate_ref, gexp_ref, db_ref, wb_ref, o_ref,
                    m_ref, acc_ref):
    qt = pl.program_id(2)
    s0 = qt * TQ
    q = qaug_ref[0, 0]
    nt = (((1,), (1,)), ((), ()))

    def per_head(bias):
        return jnp.concatenate([bias] * ATT_GROUP, axis=0)

    m_ref[...] = jnp.full(m_ref.shape, NEG_INF, F32)
    acc_ref[...] = jnp.zeros(acc_ref.shape, F32)

    def tile(first_key, width, causal=False):
        start = pl.multiple_of(first_key, width)
        k = ks_ref[pl.ds(start, width), :]
        v = vs_ref[pl.ds(start, width), :]
        s = lax.dot_general(q, k, nt, preferred_element_type=F32)
        if causal:
            s = s + per_head(db_ref[0])
        m_prev = m_ref[...]
        m_next = jnp.maximum(m_prev, jnp.max(s, axis=-1, keepdims=True))
        p = jnp.exp(s - jnp.concatenate([m_next] * (width // LANES), axis=1))
        acc_ref[...] = acc_ref[...] * jnp.exp(m_prev - m_next) + jnp.dot(
            p.astype(BF16), v, preferred_element_type=F32)
        m_ref[...] = m_next

    nfull = (qt * TQ) // TK

    def body(i, carry):
        tile(i * (2 * TK), 2 * TK)
        return carry

    lax.fori_loop(0, nfull // 2, body, 0)

    @pl.when(nfull % 2 == 1)
    def _():
        tile((nfull - 1) * TK, TK)

    tile(nfull * TK, TK, causal=True)
    acc = acc_ref[...]

    wk = WINDOW + TQ
    wstart = pl.multiple_of(jnp.maximum(s0 - WINDOW, 0), TQ)
    kwin = kw_ref[pl.ds(wstart, wk), :]
    vwin = vw_ref[pl.ds(wstart, wk), :]
    sw = lax.dot_general(q, kwin, nt, preferred_element_type=F32) + per_head(wb_ref[0])
    pw = jnp.exp(sw - jnp.max(sw, axis=-1, keepdims=True))
    ow = jnp.dot(pw.astype(BF16), vwin, preferred_element_type=F32)

    def normalised(x):
        parts = [x[g * TQ:(g + 1) * TQ] for g in range(ATT_GROUP)]
        return _merge_heads(parts) / _merge_heads(parts, upper=True)

    sig = jax.nn.sigmoid(gate_ref[...])
    gexp = gexp_ref[0]
    gmap = sum(jnp.dot(t, gexp, preferred_element_type=F32) for t in _split_bf16(sig, 2))
    y = (gmap[:, 0:KV_WIDTH] * ocmp_ref[...] + gmap[:, KV_WIDTH:2 * KV_WIDTH] * normalised(acc)
         + gmap[:, 2 * KV_WIDTH:3 * KV_WIDTH] * normalised(ow))
    o_ref[...] = y.astype(o_ref.dtype)


def _nsa_attn(qaug, ocmp, ksa, vsa, kwa, vwa, proj, bsz, seq):
    nqt = seq // TQ
    gexp = np.zeros((KV_HEADS, LANES, N_BRANCH * KV_WIDTH), np.float32)
    for h in range(KV_HEADS):
        for g in range(ATT_GROUP):
            for br in range(N_BRANCH):
                src = (h * ATT_GROUP + g) * N_BRANCH + br
                gexp[h, src, br * KV_WIDTH + g * HEAD_DIM:br * KV_WIDTH + (g + 1) * HEAD_DIM] = 1.0
    gexp = jnp.asarray(gexp, BF16)
    r = np.arange(TQ)[:, None]
    diag_cases = TK // TQ
    db = np.stack([np.where(np.arange(TK)[None, :] <= c * TQ + r, 0.0, NEG_INF) for c in range(diag_cases)])
    win_cases = WINDOW // TQ + 1
    wk = WINDOW + TQ
    dist = lambda c: c * TQ + r - np.arange(wk)[None, :]
    wb = np.stack([np.where((dist(c) >= 0) & (dist(c) < WINDOW), 0.0, NEG_INF) for c in range(win_cases)])
    db = jnp.asarray(db, F32)
    wb = jnp.asarray(wb, F32)
    stacked = pl.BlockSpec((1, 1, ROWS, LANES), lambda b, h, t: (b, h, t, 0))
    kv = pl.BlockSpec((seq, LANES), lambda b, h, t: (b, h))
    return pl.pallas_call(
        _nsaattn_kernel,
        grid=(bsz, KV_HEADS, nqt),
        in_specs=[stacked, pl.BlockSpec((TQ, KV_WIDTH), lambda b, h, t: (b * nqt + t, h)), kv, kv, kv, kv,
                  pl.BlockSpec((TQ, LANES), lambda b, h, t: (b * nqt + t, COL_GATE // LANES)),
                  pl.BlockSpec((1, LANES, N_BRANCH * KV_WIDTH), lambda b, h, t: (h, 0, 0)),
                  pl.BlockSpec((1, TQ, TK), lambda b, h, t: (t % diag_cases, 0, 0)),
                  pl.BlockSpec((1, TQ, wk), lambda b, h, t: (jnp.minimum(t, win_cases - 1), 0, 0))],
        out_specs=pl.BlockSpec((TQ, KV_WIDTH), lambda b, h, t: (b * nqt + t, h)),
        out_shape=jax.ShapeDtypeStruct((bsz * seq, ATT_WIDTH), BF16),
        scratch_shapes=[pltpu.VMEM((ROWS, LANES), F32), pltpu.VMEM((ROWS, LANES), F32)],
        compiler_params=_cparams(("parallel", "parallel", "arbitrary")),
        name="nsa_attn",
    )(qaug, ocmp, ksa, vsa, kwa, vwa, proj, gexp, db, wb)


def _outproj_kernel(x_ref, ys_ref, ya_ref, w1_ref, w2_ref, o_ref):
    o_ref[...] = (x_ref[...] + jnp.dot(ys_ref[...], w1_ref[...], preferred_element_type=F32)
                  + jnp.dot(ya_ref[...], w2_ref[...], preferred_element_type=F32))


def _out_proj(x2, ys, ya, wo_b):
    t = x2.shape[0]
    tm = 512
    return pl.pallas_call(
        _outproj_kernel,
        grid=(t // tm,),
        in_specs=[
            pl.BlockSpec((tm, D_MODEL), lambda i: (i, 0)),
            pl.BlockSpec((tm, SSD_WIDTH), lambda i: (i, 0)),
            pl.BlockSpec((tm, ATT_WIDTH), lambda i: (i, 0)),
            pl.BlockSpec((SSD_WIDTH, D_MODEL), lambda i: (0, 0)),
            pl.BlockSpec((ATT_WIDTH, D_MODEL), lambda i: (1, 0)),
        ],
        out_specs=pl.BlockSpec((tm, D_MODEL), lambda i: (i, 0)),
        out_shape=jax.ShapeDtypeStruct((t, D_MODEL), F32),
        compiler_params=_cparams(("parallel",)),
        name="out_proj",
    )(x2, ys, ya, wo_b, wo_b)


def _ffn_kernel(h_ref, nw_ref, fw_ref, wg_ref, wu_ref, wd_ref, o_ref, v_ref):
    j = pl.program_id(1)

    @pl.when(j == 0)
    def _():
        h = h_ref[...]
        ms = jnp.mean(h * h, axis=-1, keepdims=True)
        v_ref[...] = ((h * lax.rsqrt(ms + NORM_EPS)) * nw_ref[...]).astype(BF16)
        o_ref[...] = h

    v = v_ref[...]
    gate = jnp.dot(v, wg_ref[...].astype(BF16), preferred_element_type=F32)
    up = jnp.dot(v, wu_ref[...].astype(BF16), preferred_element_type=F32)
    o_ref[...] += jnp.dot((_silu(gate) * up).astype(BF16), wd_ref[...].astype(BF16), preferred_element_type=F32)

    @pl.when(j == pl.num_programs(1) - 1)
    def _():
        h2 = o_ref[...]
        ms = jnp.mean(h2 * h2, axis=-1, keepdims=True)
        o_ref[...] = (h2 * lax.rsqrt(ms + NORM_EPS)) * fw_ref[...]


def _ffn(h1, ffn_nw, final_w, wg_b, wu_b, wd_b):
    t = h1.shape[0]
    tm, tf = 1024, 256
    return pl.pallas_call(
        _ffn_kernel,
        grid=(t // tm, D_FF // tf),
        in_specs=[
            pl.BlockSpec((tm, D_MODEL), lambda i, j: (i, 0)),
            pl.BlockSpec((1, D_MODEL), lambda i, j: (0, 0)),
            pl.BlockSpec((1, D_MODEL), lambda i, j: (0, 0)),
            pl.BlockSpec((D_MODEL, tf), lambda i, j: (0, j)),
            pl.BlockSpec((D_MODEL, tf), lambda i, j: (0, j)),
            pl.BlockSpec((tf, D_MODEL), lambda i, j: (j, 0)),
        ],
        out_specs=pl.BlockSpec((tm, D_MODEL), lambda i, j: (i, 0)),
        out_shape=jax.ShapeDtypeStruct((t, D_MODEL), F32),
        scratch_shapes=[pltpu.VMEM((tm, D_MODEL), BF16)],
        compiler_params=_cparams(("parallel", "arbitrary")),
        name="ffn",
    )(h1, ffn_nw, final_w, wg_b, wu_b, wd_b)


def _rope_tables(seq):
    f32 = np.float32
    inv = (f32(1.0) / (f32(ROPE_THETA) ** (np.arange(0, ROPE_DIM, 2, dtype=f32) / f32(ROPE_DIM)))).astype(f32)
    ang = (np.arange(seq, dtype=f32)[:, None] * inv[None, :]).astype(f32)
    cos, sin = np.cos(ang).astype(f32), np.sin(ang).astype(f32)
    half = ROPE_DIM // 2
    rest_one = np.ones((seq, HEAD_DIM - ROPE_DIM), f32)
    rest_zero = np.zeros((seq, HEAD_DIM - ROPE_DIM), f32)
    zero_h = np.zeros((seq, half), f32)
    cos_h = np.concatenate([cos, cos, rest_one], axis=1)
    s1_h = np.concatenate([-sin, zero_h, rest_zero], axis=1)
    s2_h = np.concatenate([zero_h, sin, rest_zero], axis=1)
    tile = lambda a: jnp.asarray(np.tile(a, (1, KV_HEADS)))
    return tile(cos_h), tile(s1_h), tile(s2_h)


def _pad_lanes(a, width):
    return jnp.pad(a, ((0, 0), (0, width - a.shape[1])))


def _layer(h2d, bsz, seq, p):
    (attn_norm_w, w_in, conv_w, conv_b, dt_bias, a_log, d_skip, ssd_norm_w, cmp_w1_k, cmp_w2_k, cmp_w1_v,
     cmp_w2_v, cmp_pe_k, cmp_pe_v, w_out, ffn_norm_w, w_gate, w_up, w_down) = p
    o_xbc, o_dt, o_q, o_kv, o_gate = 1024, 2560, 2576, 3600, 5136
    w_perm = jnp.concatenate([
        w_in[:, o_xbc:o_dt], w_in[:, o_kv:o_gate], w_in[:, :o_xbc], w_in[:, o_q:o_kv],
        _pad_lanes(w_in[:, o_dt:o_q], LANES), _pad_lanes(w_in[:, o_gate:], LANES)], axis=1).astype(BF16)
    proj = _in_proj(h2d, attn_norm_w[None, :], w_perm)

    y_ssd = _ssd(proj, bsz, seq, conv_w, conv_b[None, :], _pad_lanes(dt_bias[None, :], LANES),
                 _pad_lanes(a_log[None, :], LANES), jnp.repeat(d_skip, SSD_HEAD_DIM)[None, :],
                 ssd_norm_w[None, :])

    cos_t, s1_t, s2_t = _rope_tables(seq)
    ksa, vsa, kwa, vwa = _kv_prep(proj, seq, cos_t, s1_t, s2_t)

    hpb = LANES // HEAD_DIM
    pe_rep = lambda pe: jnp.tile(pe, (1, hpb))
    w1_rep = lambda w1: jnp.tile(w1.astype(BF16).reshape(CMP_BLOCK, HEAD_DIM, CMP_HIDDEN), (1, hpb, 1))
    kblk, vblk = _compress(proj, bsz, seq, pe_rep(cmp_pe_k), pe_rep(cmp_pe_v),
                           w1_rep(cmp_w1_k), w1_rep(cmp_w1_v), cmp_w2_k.T.astype(BF16),
                           _pad_lanes(cmp_w2_v, LANES).astype(BF16))
    qaug, ocmp = _nsa_cmp(proj, kblk, vblk, bsz, seq, cos_t, s1_t, s2_t)
    y_att = _nsa_attn(qaug, ocmp, ksa, vsa, kwa, vwa, proj, bsz, seq)

    h1 = _out_proj(h2d, y_ssd, y_att, w_out.astype(BF16))
    return h1, (ffn_norm_w, w_gate, w_up, w_down)


def kernel(x, attn_norm_w, w_in, conv_w, conv_b, dt_bias, a_log, d_skip, ssd_norm_w, cmp_w1_k, cmp_w2_k,
           cmp_w1_v, cmp_w2_v, cmp_pe_k, cmp_pe_v, w_out, ffn_norm_w, w_gate, w_up, w_down, final_norm_w):
    bsz, seq, _ = x.shape
    depth = w_in.shape[0]
    assert depth == 1, "the final rmsnorm is fused into the last layer's ffn kernel"
    h = x.reshape(bsz * seq, D_MODEL)
    l = 0
    params = (attn_norm_w[l], w_in[l], conv_w[l], conv_b[l], dt_bias[l], a_log[l], d_skip[l], ssd_norm_w[l],
              cmp_w1_k[l], cmp_w2_k[l], cmp_w1_v[l], cmp_w2_v[l], cmp_pe_k[l], cmp_pe_v[l], w_out[l],
              ffn_norm_w[l], w_gate[l], w_up[l], w_down[l])
    h1, (fnw, wg_b, wu_b, wd_b) = _layer(h, bsz, seq, params)
    out = _ffn(h1, fnw[None, :], final_norm_w[None, :], wg_b, wu_b, wd_b)
    return out.reshape(bsz, seq, D_MODEL)
```

```python
import functools

import numpy as np
import jax
import jax.numpy as jnp
from jax import lax
from jax.experimental import pallas as pl
from jax.experimental.pallas import tpu as pltpu

F32 = jnp.float32
BF16 = jnp.bfloat16
HI = lax.Precision.HIGHEST

D_MODEL = 2048
SSD_WIDTH = 1024
ATT_WIDTH = 1024
SSD_HEAD_DIM = 64
SSD_HEADS = 16
SSD_GROUPS = 2
SSD_STATE = 128
SSD_CHUNK = 128
CONV_WIDTH = 4
CONV_CH = SSD_WIDTH + 2 * SSD_GROUPS * SSD_STATE
HEAD_DIM = 64
ATT_HEADS = 16
KV_HEADS = 4
ATT_GROUP = 4
KV_WIDTH = KV_HEADS * HEAD_DIM
CMP_BLOCK = 32
CMP_STRIDE = 16
CMP_HIDDEN = 256
SEL_BLOCK = 64
N_SELECT = 16
WINDOW = 512
N_BRANCH = 3
ROPE_THETA = 500000.0
ROPE_DIM = 16
D_FF = 5632
NORM_EPS = 1e-6
NEG_INF = -1e30
FORCE_SCORE = 1e4
SCALE = HEAD_DIM ** -0.5
BLOCK_BIAS = -(2.0 ** 100)

LANES = 128
VMEM_LIMIT = 56 * 1024 * 1024

NP = 5376
COL_XBC = 0
COL_KV = 1536
COL_Z = 3072
COL_Q = 4096
COL_DT = 5120
COL_GATE = 5248

TQ = 256
TK = 512
ROWS = ATT_GROUP * TQ


def _silu(x):
    return x * jax.nn.sigmoid(x)


def _split_bf16(x, terms):
    out = []
    for _ in range(terms - 1):
        t = x.astype(BF16)
        out.append(t)
        x = x - t.astype(F32)
    out.append(x.astype(BF16))
    return out


def _cparams(sem):
    return pltpu.CompilerParams(dimension_semantics=sem, vmem_limit_bytes=VMEM_LIMIT)


def _inproj_kernel(x_ref, nw_ref, w_ref, o_ref, u_ref):
    @pl.when(pl.program_id(1) == 0)
    def _():
        x = x_ref[...]
        ms = jnp.mean(x * x, axis=-1, keepdims=True)
        u_ref[...] = ((x * lax.rsqrt(ms + NORM_EPS)) * nw_ref[...]).astype(BF16)

    o_ref[...] = jnp.dot(u_ref[...], w_ref[...], preferred_element_type=F32)


def _in_proj(x2, norm_w, w_perm):
    t = x2.shape[0]
    tm, tn = 1024, 768
    return pl.pallas_call(
        _inproj_kernel,
        grid=(t // tm, NP // tn),
        in_specs=[
            pl.BlockSpec((tm, D_MODEL), lambda i, j: (i, 0)),
            pl.BlockSpec((1, D_MODEL), lambda i, j: (0, 0)),
            pl.BlockSpec((D_MODEL, tn), lambda i, j: (0, j)),
        ],
        out_specs=pl.BlockSpec((tm, tn), lambda i, j: (i, j)),
        out_shape=jax.ShapeDtypeStruct((t, NP), F32),
        scratch_shapes=[pltpu.VMEM((tm, D_MODEL), BF16)],
        compiler_params=_cparams(("parallel", "arbitrary")),
        name="in_proj",
    )(x2, norm_w, w_perm)


def _ssd_kernel(xbc_ref, z_ref, dt_ref, cw_ref, cb_ref, dtb_ref, alog_ref, dskip_ref, nw_ref,
                ltri_ref, ex_ref, o_ref, ext_ref, state_ref, y_ref):
    L = SSD_CHUNK
    c = pl.program_id(1)

    @pl.when(c == 0)
    def _():
        ext_ref[0:8, :] = jnp.zeros((8, CONV_CH), F32)
        state_ref[...] = jnp.zeros(state_ref.shape, F32)

    ext_ref[8:8 + L, :] = xbc_ref[...]
    w = cw_ref[...]
    y = (ext_ref[5:5 + L, :] * w[0:1, :] + ext_ref[6:6 + L, :] * w[1:2, :]
         + ext_ref[7:7 + L, :] * w[2:3, :] + ext_ref[8:8 + L, :] * w[3:4, :]) + cb_ref[...]
    tail = ext_ref[L:L + 8, :]
    ext_ref[0:8, :] = tail
    act = _silu(y)
    xs = act[:, :SSD_WIDTH]
    bm = act[:, SSD_WIDTH:SSD_WIDTH + SSD_GROUPS * SSD_STATE]
    cm = act[:, SSD_WIDTH + SSD_GROUPS * SSD_STATE:]

    v = dt_ref[...] + dtb_ref[...]
    dt = jnp.maximum(v, 0.0) + jnp.log1p(jnp.exp(-jnp.abs(v)))
    a = -jnp.exp(alog_ref[...])
    adt = a * dt
    ltri = ltri_ref[...]
    acum = sum(jnp.dot(ltri, t, preferred_element_type=F32) for t in _split_bf16(adt, 3))
    acum_t = acum.T
    last = acum[L - 1:L, :]
    stacked = jnp.concatenate([dt, jnp.exp(acum), jnp.exp(last - acum)], axis=0)
    ex = ex_ref[...]
    expanded = sum(jnp.dot(t, ex, preferred_element_type=F32) for t in _split_bf16(stacked, 2))
    dt_e = expanded[0:L]
    expa_e = expanded[L:2 * L]
    dst_e = expanded[2 * L:3 * L]

    xdt = xs * dt_e
    xds_b = (xdt * dst_e).astype(BF16)
    row = lax.broadcasted_iota(jnp.int32, (L, L), 0)
    col = lax.broadcasted_iota(jnp.int32, (L, L), 1)
    causal = row >= col
    lane = lax.broadcasted_iota(jnp.int32, (L, LANES), 1)
    hg = SSD_HEADS // SSD_GROUPS
    gw = hg * SSD_HEAD_DIM
    for g in range(SSD_GROUPS):
        bg = bm[:, g * SSD_STATE:(g + 1) * SSD_STATE]
        cg_b = cm[:, g * SSD_STATE:(g + 1) * SSD_STATE].astype(BF16)
        bg_b = bg.astype(BF16)
        cb = lax.dot_general(cg_b, bg_b, (((1,), (1,)), ((), ())), preferred_element_type=F32)
        hprev = state_ref[g]
        yoff = jnp.dot(cg_b, hprev.astype(BF16), preferred_element_type=F32) * expa_e[:, g * gw:(g + 1) * gw]
        snew = jnp.dot(bg.T.astype(BF16), xds_b[:, g * gw:(g + 1) * gw], preferred_element_type=F32)
        state_ref[g] = hprev * expa_e[L - 1:L, g * gw:(g + 1) * gw] + snew
        for k in range(hg // 2):
            pair = g * (hg // 2) + k
            ms = []
            for hh in (2 * pair, 2 * pair + 1):
                seg = acum[:, hh:hh + 1] - acum_t[hh:hh + 1, :]
                decay = jnp.exp(jnp.where(causal, seg, -jnp.inf))
                ms.append((cb * decay).astype(BF16))
            mpair = jnp.concatenate(ms, axis=1)
            xp = xdt[:, pair * LANES:(pair + 1) * LANES]
            xblk = jnp.concatenate([jnp.where(lane < SSD_HEAD_DIM, xp, 0.0),
                                    jnp.where(lane >= SSD_HEAD_DIM, xp, 0.0)], axis=0).astype(BF16)
            yd = jnp.dot(mpair, xblk, preferred_element_type=F32)
            y_ref[:, pair * LANES:(pair + 1) * LANES] = yd + yoff[:, k * LANES:(k + 1) * LANES]

    yy = y_ref[...] + dskip_ref[...] * xs
    z = z_ref[...]
    yz = yy * _silu(z)
    ms2 = jnp.mean(yz * yz, axis=-1, keepdims=True)
    o_ref[...] = ((yz * lax.rsqrt(ms2 + NORM_EPS)) * nw_ref[...]).astype(o_ref.dtype)


def _ssd(proj, bsz, seq, conv_w, conv_b, dtb_pad, alog_pad, dskip_e, norm_w):
    nc = seq // SSD_CHUNK
    L = SSD_CHUNK
    ltri = jnp.asarray(np.tril(np.ones((L, L), np.float32)), BF16)
    ex = np.zeros((LANES, SSD_WIDTH), np.float32)
    for h in range(SSD_HEADS):
        ex[h, h * SSD_HEAD_DIM:(h + 1) * SSD_HEAD_DIM] = 1.0
    ex = jnp.asarray(ex, BF16)
    const = lambda shape: pl.BlockSpec(shape, lambda b, c: (0,) * len(shape))
    return pl.pallas_call(
        _ssd_kernel,
        grid=(bsz, nc),
        in_specs=[
            pl.BlockSpec((L, CONV_CH), lambda b, c: (b * nc + c, COL_XBC // CONV_CH)),
            pl.BlockSpec((L, SSD_WIDTH), lambda b, c: (b * nc + c, COL_Z // SSD_WIDTH)),
            pl.BlockSpec((L, LANES), lambda b, c: (b * nc + c, COL_DT // LANES)),
            const((CONV_WIDTH, CONV_CH)), const((1, CONV_CH)), const((1, LANES)), const((1, LANES)),
            const((1, SSD_WIDTH)), const((1, SSD_WIDTH)), const((L, L)), const((LANES, SSD_WIDTH)),
        ],
        out_specs=pl.BlockSpec((L, SSD_WIDTH), lambda b, c: (b * nc + c, 0)),
        out_shape=jax.ShapeDtypeStruct((bsz * seq, SSD_WIDTH), BF16),
        scratch_shapes=[
            pltpu.VMEM((L + 8, CONV_CH), F32),
            pltpu.VMEM((SSD_GROUPS, SSD_STATE, (SSD_HEADS // SSD_GROUPS) * SSD_HEAD_DIM), F32),
            pltpu.VMEM((L, SSD_WIDTH), F32),
        ],
        compiler_params=_cparams(("parallel", "arbitrary")),
        name="ssd",
    )(proj, proj, proj, conv_w, conv_b, dtb_pad, alog_pad, dskip_e, norm_w, ltri, ex)


def _rope(x, cos, s1, s2):
    n = x.shape[-1]
    half = ROPE_DIM // 2
    return x * cos + pltpu.roll(x, n - half, 1) * s1 + pltpu.roll(x, half, 1) * s2


def _merge_heads(parts, upper=False):
    lane = lax.broadcasted_iota(jnp.int32, parts[0].shape, 1)
    out = []
    for k in range(len(parts) // 2):
        a, b = parts[2 * k], parts[2 * k + 1]
        if upper:
            out.append(jnp.where(lane < HEAD_DIM, pltpu.roll(a, HEAD_DIM, 1), b))
        else:
            out.append(jnp.where(lane < HEAD_DIM, a, pltpu.roll(b, HEAD_DIM, 1)))
    return jnp.concatenate(out, axis=1)


def _kvprep_kernel(ks_ref, vs_ref, kw_ref, vw_ref, cos_ref, s1_ref, s2_ref, sel_ref,
                   kso_ref, vso_ref, kwo_ref, vwo_ref, *, tiles_per_seq):
    tr = ks_ref.shape[0]
    wide = KV_HEADS * LANES
    s0 = (pl.program_id(0) % tiles_per_seq) * tr
    cos, s1, s2 = cos_ref[...], s1_ref[...], s2_ref[...]
    sel = sel_ref[...]
    lane = lax.broadcasted_iota(jnp.int32, (tr, wide), 1) % LANES
    blk = (s0 + lax.broadcasted_iota(jnp.int32, (tr, wide), 0)) // SEL_BLOCK
    ebias = jnp.where(lane - SEL_BLOCK == blk, BLOCK_BIAS, 0.0)
    ones = jnp.where(lane >= HEAD_DIM, 1.0, 0.0)

    def place(x):
        return jnp.dot(x.astype(BF16), sel, preferred_element_type=F32)

    kso_ref[...] = (place(_rope(ks_ref[...], cos, s1, s2)) + ebias).astype(BF16)
    kwo_ref[...] = place(_rope(kw_ref[...], cos, s1, s2)).astype(BF16)
    vso_ref[...] = (place(vs_ref[...]) + ones).astype(BF16)
    vwo_ref[...] = (place(vw_ref[...]) + ones).astype(BF16)


def _head_place_matrix():
    m = np.zeros((KV_WIDTH, KV_HEADS * LANES), np.float32)
    for h in range(KV_HEADS):
        for d in range(HEAD_DIM):
            m[h * HEAD_DIM + d, h * LANES + d] = 1.0
    return m


def _kv_prep(proj, seq, cos_t, s1_t, s2_t):
    t = proj.shape[0]
    tr = 512
    tps = seq // tr
    sel = jnp.asarray(_head_place_matrix(), BF16)
    kvb = COL_KV // KV_WIDTH
    seg = lambda k: pl.BlockSpec((tr, KV_WIDTH), lambda i: (i, kvb + k))
    tab = pl.BlockSpec((tr, KV_WIDTH), lambda i: (i % tps, 0))
    wide = KV_HEADS * LANES
    out = pl.BlockSpec((tr, wide), lambda i: (i, 0))
    shp = jax.ShapeDtypeStruct((t, wide), BF16)
    return pl.pallas_call(
        functools.partial(_kvprep_kernel, tiles_per_seq=tps),
        grid=(t // tr,),
        in_specs=[seg(2), seg(3), seg(4), seg(5), tab, tab, tab,
                  pl.BlockSpec((KV_WIDTH, wide), lambda i: (0, 0))],
        out_specs=[out, out, out, out],
        out_shape=[shp, shp, shp, shp],
        compiler_params=_cparams(("parallel",)),
        name="kv_prep",
    )(proj, proj, proj, proj, cos_t, s1_t, s2_t, sel)


def _compress_kernel(gk_ref, gv_ref, pek_ref, pev_ref, w1k_ref, w1v_ref, w2kt_ref, w2v_ref,
                     kblk_ref, vblk_ref, shift_ref):
    ng = gk_ref.shape[0] // CMP_STRIDE
    heads_per_block = LANES // HEAD_DIM
    mine = (lax.broadcasted_iota(jnp.int32, (ng, LANES), 1) // HEAD_DIM) == pl.program_id(1) % heads_per_block

    def hidden(x_ref, pe_ref, w1_ref):
        top = jnp.zeros((ng, CMP_HIDDEN), F32)
        bot = jnp.zeros((ng, CMP_HIDDEN), F32)
        for r in range(CMP_STRIDE):
            x = x_ref[pl.ds(r, ng, stride=CMP_STRIDE), :]
            xt = jnp.where(mine, x + pe_ref[r:r + 1, :], 0.0).astype(BF16)
            xb = jnp.where(mine, x + pe_ref[CMP_STRIDE + r:CMP_STRIDE + r + 1, :], 0.0).astype(BF16)
            top = top + jnp.dot(xt, w1_ref[r], preferred_element_type=F32)
            bot = bot + jnp.dot(xb, w1_ref[CMP_STRIDE + r], preferred_element_type=F32)
        shift_ref[0:ng, :] = bot
        shift_ref[ng:ng + 8, :] = jnp.zeros((8, CMP_HIDDEN), F32)
        return _silu(top + shift_ref[1:ng + 1, :]).astype(BF16)

    hk = hidden(gk_ref, pek_ref, w1k_ref)
    kt = lax.dot_general(w2kt_ref[...], hk, (((1,), (1,)), ((), ())), preferred_element_type=F32)
    kblk_ref[...] = jnp.zeros(kblk_ref.shape, BF16)
    for g in range(ATT_GROUP):
        kblk_ref[0, 0, g * HEAD_DIM:(g + 1) * HEAD_DIM, g * ng:(g + 1) * ng] = kt.astype(BF16)

    hv = hidden(gv_ref, pev_ref, w1v_ref)
    vc = jnp.dot(hv, w2v_ref[...], preferred_element_type=F32)
    vblk_ref[...] = jnp.zeros(vblk_ref.shape, BF16)
    for g in range(ATT_GROUP):
        vblk_ref[0, 0, g * ng:(g + 1) * ng, g * LANES:(g + 1) * LANES] = vc.astype(BF16)


def _compress(proj, bsz, seq, pek, pev, w1k, w1v, w2kt, w2v):
    ng = seq // CMP_STRIDE
    hpb = LANES // HEAD_DIM
    kcb = COL_KV // LANES
    vcb = kcb + KV_WIDTH // LANES
    const = lambda shape: pl.BlockSpec(shape, lambda b, h: (0,) * len(shape))
    return pl.pallas_call(
        _compress_kernel,
        grid=(bsz, KV_HEADS),
        in_specs=[pl.BlockSpec((seq, LANES), lambda b, h: (b, kcb + h // hpb)),
                  pl.BlockSpec((seq, LANES), lambda b, h: (b, vcb + h // hpb)),
                  const((CMP_BLOCK, LANES)), const((CMP_BLOCK, LANES)),
                  const((CMP_BLOCK, LANES, CMP_HIDDEN)), const((CMP_BLOCK, LANES, CMP_HIDDEN)),
                  const((HEAD_DIM, CMP_HIDDEN)), const((CMP_HIDDEN, LANES))],
        out_specs=[pl.BlockSpec((1, 1, KV_WIDTH, ATT_GROUP * ng), lambda b, h: (b, h, 0, 0)),
                   pl.BlockSpec((1, 1, ATT_GROUP * ng, ATT_GROUP * LANES), lambda b, h: (b, h, 0, 0))],
        out_shape=[jax.ShapeDtypeStruct((bsz, KV_HEADS, KV_WIDTH, ATT_GROUP * ng), BF16),
                   jax.ShapeDtypeStruct((bsz, KV_HEADS, ATT_GROUP * ng, ATT_GROUP * LANES), BF16)],
        scratch_shapes=[pltpu.VMEM((ng + 8, CMP_HIDDEN), F32)],
        compiler_params=_cparams(("parallel", "parallel")),
        name="compress",
    )(proj, proj, pek, pev, w1k, w1v, w2kt, w2v)


TQC = TQ
SUBLANES = 8


CMP_HEADS_PER_STEP = 2


def _nsacmp_kernel(q_ref, kblk_ref, vblk_ref, cos_ref, s1_ref, s2_ref, ovt_ref, selq_ref, placen_ref, eye_ref,
                   qaug_ref, ocmp_ref):
    for hh in range(CMP_HEADS_PER_STEP):
        lanes = slice(hh * KV_WIDTH, (hh + 1) * KV_WIDTH)
        _nsacmp_head(q_ref.at[:, lanes], kblk_ref.at[0, hh], vblk_ref.at[0, hh], cos_ref, s1_ref, s2_ref,
                     ovt_ref, selq_ref, placen_ref, eye_ref, qaug_ref.at[0, hh], ocmp_ref.at[:, lanes])


def _nsacmp_head(q_ref, kblk_ref, vblk_ref, cos_ref, s1_ref, s2_ref, ovt_ref, selq_ref, placen_ref, eye_ref,
                 qaug_ref, ocmp_ref):
    ncp = kblk_ref.shape[1] // ATT_GROUP
    nsel = ovt_ref.shape[0]
    nt = (((1,), (1,)), ((), ()))
    s0 = pl.program_id(2) * TQC
    q = q_ref[...]
    s_all = jnp.dot((q * SCALE).astype(BF16), kblk_ref[...], preferred_element_type=F32)
    tq_pos = s0 + lax.broadcasted_iota(jnp.int32, (TQC, ncp), 0)
    cmp_end = lax.broadcasted_iota(jnp.int32, (TQC, ncp), 1) * CMP_STRIDE + (CMP_BLOCK - 1)
    mask = cmp_end <= tq_pos
    any_visible = (s0 + lax.broadcasted_iota(jnp.int32, (TQC, 1), 0) >= CMP_BLOCK - 1).astype(F32)
    ps = []
    for g in range(ATT_GROUP):
        s = jnp.where(mask, s_all[:, g * ncp:(g + 1) * ncp], NEG_INF)
        m = jnp.max(s, axis=-1, keepdims=True)
        e = jnp.exp(s - m)
        ps.append(e * (any_visible / jnp.sum(e, axis=-1, keepdims=True)))
    p_all = jnp.concatenate(ps, axis=1)

    def stack_rows(wide):
        return jnp.concatenate([wide[u * TQ:(u + 1) * TQ, g * LANES:(g + 1) * LANES]
                                for u in range(TQC // TQ) for g in range(ATT_GROUP)], axis=0)

    ow = jnp.dot(p_all.astype(BF16), vblk_ref[...], preferred_element_type=F32)
    ocmp_ref[...] = _merge_heads([ow[:, g * LANES:(g + 1) * LANES] for g in range(ATT_GROUP)])

    psum = (ps[0] + ps[1]) + (ps[2] + ps[3])
    p_hi = psum.astype(BF16)
    p_lo = (psum - p_hi.astype(F32)).astype(BF16)
    ovt = ovt_ref[...]
    imp = (lax.dot_general(ovt, p_hi, nt, preferred_element_type=F32)
           + lax.dot_general(ovt, p_lo, nt, preferred_element_type=F32))
    j = lax.broadcasted_iota(jnp.int32, (nsel, TQC), 0)
    cur = (s0 + lax.broadcasted_iota(jnp.int32, (nsel, TQC), 1)) // SEL_BLOCK
    imp = jnp.where((j == 0) | (j == cur) | (j == cur - 1), FORCE_SCORE, imp)
    imp = jnp.where(j <= cur, imp, -1.0)
    nblk = nsel // SUBLANES
    blocks = [imp[k * SUBLANES:(k + 1) * SUBLANES, :] for k in range(nblk)]
    ranks = [jnp.zeros((SUBLANES, TQC), F32) for _ in range(nblk)]
    sub = lax.broadcasted_iota(jnp.int32, (SUBLANES, TQC), 0)
    for jp in range(nsel):
        r = jnp.broadcast_to(imp[jp:jp + 1, :], (SUBLANES, TQC))
        for k in range(nblk):
            if k > jp // SUBLANES:
                inc = jnp.where(r >= blocks[k], 1.0, 0.0)
            elif k < jp // SUBLANES:
                inc = jnp.where(r > blocks[k], 1.0, 0.0)
            else:
                tie = jnp.where(sub > jp % SUBLANES, 1.0, 0.0)
                inc = jnp.where(r > blocks[k], 1.0, jnp.where(r == blocks[k], tie, 0.0))
            ranks[k] = ranks[k] + inc
    rank = jnp.concatenate(ranks, axis=0)
    notsel_t = jnp.where((rank < float(N_SELECT)) & (imp >= 0.0), 0.0, 1.0).astype(BF16)
    notsel = lax.dot_general(eye_ref[...], notsel_t, nt, preferred_element_type=F32)

    q_rot = (_rope(q, cos_ref[...], s1_ref[...], s2_ref[...]) * SCALE).astype(BF16)
    qw = (jnp.dot(q_rot, selq_ref[...], preferred_element_type=F32)
          + jnp.dot(notsel.astype(BF16), placen_ref[...], preferred_element_type=F32))
    qaug_ref[...] = stack_rows(qw).astype(BF16)


def _selection_overlap_t(n_cmp_pad, n_cmp, n_sel):
    cs = np.arange(n_cmp)[:, None] * CMP_STRIDE
    ce = cs + CMP_BLOCK
    ss = np.arange(n_sel)[None, :] * SEL_BLOCK
    se = ss + SEL_BLOCK
    ov = np.clip(np.minimum(ce, se) - np.maximum(cs, ss), 0, None) / CMP_BLOCK
    full = np.zeros((n_cmp_pad, n_sel), np.float32)
    full[:n_cmp] = ov
    return full.T.copy()


def _nsa_cmp(proj, kblk, vblk, bsz, seq, cos_t, s1_t, s2_t):
    nqc = seq // TQC
    ncp = kblk.shape[3] // ATT_GROUP
    n_cmp = (seq - CMP_BLOCK) // CMP_STRIDE + 1
    nsel = seq // SEL_BLOCK
    ovt = jnp.asarray(_selection_overlap_t(ncp, n_cmp, nsel), BF16)
    selq = jnp.asarray(_head_place_matrix(), BF16)
    placen = np.zeros((nsel, ATT_GROUP * LANES), np.float32)
    for g in range(ATT_GROUP):
        for jb in range(nsel):
            placen[jb, g * LANES + HEAD_DIM + jb] = 1.0
    placen = jnp.asarray(placen, BF16)
    eye = jnp.asarray(np.eye(TQC, dtype=np.float32), BF16)
    const = lambda shape: pl.BlockSpec(shape, lambda b, h, t: (0,) * len(shape))
    tab = pl.BlockSpec((TQC, KV_WIDTH), lambda b, h, t: (t, 0))
    hps = CMP_HEADS_PER_STEP
    qb = COL_Q // (hps * KV_WIDTH)
    rows = (TQC // TQ) * ROWS
    stacked = pl.BlockSpec((1, hps, rows, LANES), lambda b, h, t: (b, h, t, 0))
    return pl.pallas_call(
        _nsacmp_kernel,
        grid=(bsz, KV_HEADS // hps, nqc),
        in_specs=[
            pl.BlockSpec((TQC, hps * KV_WIDTH), lambda b, h, t: (b * nqc + t, qb + h)),
            pl.BlockSpec((1, hps, KV_WIDTH, ATT_GROUP * ncp), lambda b, h, t: (b, h, 0, 0)),
            pl.BlockSpec((1, hps, ATT_GROUP * ncp, ATT_GROUP * LANES), lambda b, h, t: (b, h, 0, 0)),
            tab, tab, tab,
            const((nsel, ncp)), const((KV_WIDTH, ATT_GROUP * LANES)),
            const((nsel, ATT_GROUP * LANES)), const((TQC, TQC)),
        ],
        out_specs=[stacked, pl.BlockSpec((TQC, hps * KV_WIDTH), lambda b, h, t: (b * nqc + t, h))],
        out_shape=[jax.ShapeDtypeStruct((bsz, KV_HEADS, nqc * rows, LANES), BF16),
                   jax.ShapeDtypeStruct((bsz * seq, ATT_WIDTH), F32)],
        compiler_params=_cparams(("parallel", "parallel", "parallel")),
        name="nsa_cmp",
    )(proj, kblk, vblk, cos_t, s1_t, s2_t, ovt, selq, placen, eye)


def _nsaattn_kernel(qaug_ref, ocmp_ref, ks_ref, vs_ref, kw_ref, vw_ref, gate_ref, gexp_ref, db_ref, wb_ref, o_ref,
                    m_ref, acc_ref):
    qt = pl.program_id(2)
    s0 = qt * TQ
    q = qaug_ref[0, 0]
    nt = (((1,), (1,)), ((), ()))

    def per_head(bias):
        return jnp.concatenate([bias] * ATT_GROUP, axis=0)

    m_ref[...] = jnp.full(m_ref.shape, NEG_INF, F32)
    acc_ref[...] = jnp.zeros(acc_ref.shape, F32)

    def tile(first_key, width, causal=False):
        start = pl.multiple_of(first_key, width)
        k = ks_ref[pl.ds(start, width), :]
        v = vs_ref[pl.ds(start, width), :]
        s = lax.dot_general(q, k, nt, preferred_element_type=F32)
        if causal:
            s = s + per_head(db_ref[0][:, :width])
        m_prev = m_ref[...]
        m_next = jnp.maximum(m_prev, jnp.max(s, axis=-1, keepdims=True))
        p = jnp.exp(s - jnp.concatenate([m_next] * (width // LANES), axis=1))
        acc_ref[...] = acc_ref[...] * jnp.exp(m_prev - m_next) + jnp.dot(
            p.astype(BF16), v, preferred_element_type=F32)
        m_ref[...] = m_next

    nfull = (qt * TQ) // TK

    def body(i, carry):
        tile(i * (2 * TK), 2 * TK)
        return carry

    lax.fori_loop(0, nfull // 2, body, 0)

    @pl.when(nfull % 2 == 1)
    def _():
        tile((nfull - 1) * TK, TK)

    @pl.when(s0 == nfull * TK)
    def _():
        tile(s0, TQ, causal=True)

    @pl.when(s0 != nfull * TK)
    def _():
        tile(nfull * TK, TK, causal=True)

    acc = acc_ref[...]

    wk = WINDOW + TQ
    wstart = pl.multiple_of(jnp.maximum(s0 - WINDOW, 0), TQ)
    kwin = kw_ref[pl.ds(wstart, wk), :]
    vwin = vw_ref[pl.ds(wstart, wk), :]
    sw = lax.dot_general(q, kwin, nt, preferred_element_type=F32) + per_head(wb_ref[0])
    pw = jnp.exp(sw - jnp.max(sw, axis=-1, keepdims=True))
    ow = jnp.dot(pw.astype(BF16), vwin, preferred_element_type=F32)

    def normalised(x):
        parts = [x[g * TQ:(g + 1) * TQ] for g in range(ATT_GROUP)]
        return _merge_heads(parts) / _merge_heads(parts, upper=True)

    sig = jax.nn.sigmoid(gate_ref[...])
    gexp = gexp_ref[0]
    gmap = sum(jnp.dot(t, gexp, preferred_element_type=F32) for t in _split_bf16(sig, 2))
    y = (gmap[:, 0:KV_WIDTH] * ocmp_ref[...] + gmap[:, KV_WIDTH:2 * KV_WIDTH] * normalised(acc)
         + gmap[:, 2 * KV_WIDTH:3 * KV_WIDTH] * normalised(ow))
    o_ref[...] = y.astype(o_ref.dtype)


def _nsa_attn(qaug, ocmp, ksa, vsa, kwa, vwa, proj, bsz, seq):
    nqt = seq // TQ
    gexp = np.zeros((KV_HEADS, LANES, N_BRANCH * KV_WIDTH), np.float32)
    for h in range(KV_HEADS):
        for g in range(ATT_GROUP):
            for br in range(N_BRANCH):
                src = (h * ATT_GROUP + g) * N_BRANCH + br
                gexp[h, src, br * KV_WIDTH + g * HEAD_DIM:br * KV_WIDTH + (g + 1) * HEAD_DIM] = 1.0
    gexp = jnp.asarray(gexp, BF16)
    r = np.arange(TQ)[:, None]
    diag_cases = TK // TQ
    db = np.stack([np.where(np.arange(TK)[None, :] <= c * TQ + r, 0.0, NEG_INF) for c in range(diag_cases)])
    win_cases = WINDOW // TQ + 1
    wk = WINDOW + TQ
    dist = lambda c: c * TQ + r - np.arange(wk)[None, :]
    wb = np.stack([np.where((dist(c) >= 0) & (dist(c) < WINDOW), 0.0, NEG_INF) for c in range(win_cases)])
    db = jnp.asarray(db, F32)
    wb = jnp.asarray(wb, F32)
    stacked = pl.BlockSpec((1, 1, ROWS, LANES), lambda b, h, t: (b, h, t, 0))
    kv = pl.BlockSpec((seq, LANES), lambda b, h, t: (b, h))
    return pl.pallas_call(
        _nsaattn_kernel,
        grid=(bsz, KV_HEADS, nqt),
        in_specs=[stacked, pl.BlockSpec((TQ, KV_WIDTH), lambda b, h, t: (b * nqt + t, h)), kv, kv, kv, kv,
                  pl.BlockSpec((TQ, LANES), lambda b, h, t: (b * nqt + t, COL_GATE // LANES)),
                  pl.BlockSpec((1, LANES, N_BRANCH * KV_WIDTH), lambda b, h, t: (h, 0, 0)),
                  pl.BlockSpec((1, TQ, TK), lambda b, h, t: (t % diag_cases, 0, 0)),
                  pl.BlockSpec((1, TQ, wk), lambda b, h, t: (jnp.minimum(t, win_cases - 1), 0, 0))],
        out_specs=pl.BlockSpec((TQ, KV_WIDTH), lambda b, h, t: (b * nqt + t, h)),
        out_shape=jax.ShapeDtypeStruct((bsz * seq, ATT_WIDTH), BF16),
        scratch_shapes=[pltpu.VMEM((ROWS, LANES), F32), pltpu.VMEM((ROWS, LANES), F32)],
        compiler_params=_cparams(("parallel", "parallel", "arbitrary")),
        name="nsa_attn",
    )(qaug, ocmp, ksa, vsa, kwa, vwa, proj, gexp, db, wb)


def _outproj_kernel(x_ref, ys_ref, ya_ref, w1_ref, w2_ref, o_ref):
    o_ref[...] = (x_ref[...] + jnp.dot(ys_ref[...], w1_ref[...], preferred_element_type=F32)
                  + jnp.dot(ya_ref[...], w2_ref[...], preferred_element_type=F32))


def _out_proj(x2, ys, ya, wo_b):
    t = x2.shape[0]
    tm = 512
    return pl.pallas_call(
        _outproj_kernel,
        grid=(t // tm,),
        in_specs=[
            pl.BlockSpec((tm, D_MODEL), lambda i: (i, 0)),
            pl.BlockSpec((tm, SSD_WIDTH), lambda i: (i, 0)),
            pl.BlockSpec((tm, ATT_WIDTH), lambda i: (i, 0)),
            pl.BlockSpec((SSD_WIDTH, D_MODEL), lambda i: (0, 0)),
            pl.BlockSpec((ATT_WIDTH, D_MODEL), lambda i: (1, 0)),
        ],
        out_specs=pl.BlockSpec((tm, D_MODEL), lambda i: (i, 0)),
        out_shape=jax.ShapeDtypeStruct((t, D_MODEL), F32),
        compiler_params=_cparams(("parallel",)),
        name="out_proj",
    )(x2, ys, ya, wo_b, wo_b)


def _ffn_kernel(h_ref, nw_ref, fw_ref, wg_ref, wu_ref, wd_ref, o_ref, v_ref):
    j = pl.program_id(1)

    @pl.when(j == 0)
    def _():
        h = h_ref[...]
        ms = jnp.mean(h * h, axis=-1, keepdims=True)
        v_ref[...] = ((h * lax.rsqrt(ms + NORM_EPS)) * nw_ref[...]).astype(BF16)
        o_ref[...] = h

    v = v_ref[...]
    gate = jnp.dot(v, wg_ref[...].astype(BF16), preferred_element_type=F32)
    up = jnp.dot(v, wu_ref[...].astype(BF16), preferred_element_type=F32)
    o_ref[...] += jnp.dot((_silu(gate) * up).astype(BF16), wd_ref[...].astype(BF16), preferred_element_type=F32)

    @pl.when(j == pl.num_programs(1) - 1)
    def _():
        h2 = o_ref[...]
        ms = jnp.mean(h2 * h2, axis=-1, keepdims=True)
        o_ref[...] = (h2 * lax.rsqrt(ms + NORM_EPS)) * fw_ref[...]


def _ffn(h1, ffn_nw, final_w, wg_b, wu_b, wd_b):
    t = h1.shape[0]
    tm, tf = 1024, 256
    return pl.pallas_call(
        _ffn_kernel,
        grid=(t // tm, D_FF // tf),
        in_specs=[
            pl.BlockSpec((tm, D_MODEL), lambda i, j: (i, 0)),
            pl.BlockSpec((1, D_MODEL), lambda i, j: (0, 0)),
            pl.BlockSpec((1, D_MODEL), lambda i, j: (0, 0)),
            pl.BlockSpec((D_MODEL, tf), lambda i, j: (0, j)),
            pl.BlockSpec((D_MODEL, tf), lambda i, j: (0, j)),
            pl.BlockSpec((tf, D_MODEL), lambda i, j: (j, 0)),
        ],
        out_specs=pl.BlockSpec((tm, D_MODEL), lambda i, j: (i, 0)),
        out_shape=jax.ShapeDtypeStruct((t, D_MODEL), F32),
        scratch_shapes=[pltpu.VMEM((tm, D_MODEL), BF16)],
        compiler_params=_cparams(("parallel", "arbitrary")),
        name="ffn",
    )(h1, ffn_nw, final_w, wg_b, wu_b, wd_b)


def _rope_tables(seq):
    f32 = np.float32
    inv = (f32(1.0) / (f32(ROPE_THETA) ** (np.arange(0, ROPE_DIM, 2, dtype=f32) / f32(ROPE_DIM)))).astype(f32)
    ang = (np.arange(seq, dtype=f32)[:, None] * inv[None, :]).astype(f32)
    cos, sin = np.cos(ang).astype(f32), np.sin(ang).astype(f32)
    half = ROPE_DIM // 2
    rest_one = np.ones((seq, HEAD_DIM - ROPE_DIM), f32)
    rest_zero = np.zeros((seq, HEAD_DIM - ROPE_DIM), f32)
    zero_h = np.zeros((seq, half), f32)
    cos_h = np.concatenate([cos, cos, rest_one], axis=1)
    s1_h = np.concatenate([-sin, zero_h, rest_zero], axis=1)
    s2_h = np.concatenate([zero_h, sin, rest_zero], axis=1)
    tile = lambda a: jnp.asarray(np.tile(a, (1, KV_HEADS)))
    return tile(cos_h), tile(s1_h), tile(s2_h)


def _pad_lanes(a, width):
    return jnp.pad(a, ((0, 0), (0, width - a.shape[1])))


def _layer(h2d, bsz, seq, p):
    (attn_norm_w, w_in, conv_w, conv_b, dt_bias, a_log, d_skip, ssd_norm_w, cmp_w1_k, cmp_w2_k, cmp_w1_v,
     cmp_w2_v, cmp_pe_k, cmp_pe_v, w_out, ffn_norm_w, w_gate, w_up, w_down) = p
    o_xbc, o_dt, o_q, o_kv, o_gate = 1024, 2560, 2576, 3600, 5136
    w_perm = jnp.concatenate([
        w_in[:, o_xbc:o_dt], w_in[:, o_kv:o_gate], w_in[:, :o_xbc], w_in[:, o_q:o_kv],
        _pad_lanes(w_in[:, o_dt:o_q], LANES), _pad_lanes(w_in[:, o_gate:], LANES)], axis=1).astype(BF16)
    proj = _in_proj(h2d, attn_norm_w[None, :], w_perm)

    y_ssd = _ssd(proj, bsz, seq, conv_w, conv_b[None, :], _pad_lanes(dt_bias[None, :], LANES),
                 _pad_lanes(a_log[None, :], LANES), jnp.repeat(d_skip, SSD_HEAD_DIM)[None, :],
                 ssd_norm_w[None, :])

    cos_t, s1_t, s2_t = _rope_tables(seq)
    ksa, vsa, kwa, vwa = _kv_prep(proj, seq, cos_t, s1_t, s2_t)

    hpb = LANES // HEAD_DIM
    pe_rep = lambda pe: jnp.tile(pe, (1, hpb))
    w1_rep = lambda w1: jnp.tile(w1.astype(BF16).reshape(CMP_BLOCK, HEAD_DIM, CMP_HIDDEN), (1, hpb, 1))
    kblk, vblk = _compress(proj, bsz, seq, pe_rep(cmp_pe_k), pe_rep(cmp_pe_v),
                           w1_rep(cmp_w1_k), w1_rep(cmp_w1_v), cmp_w2_k.T.astype(BF16),
                           _pad_lanes(cmp_w2_v, LANES).astype(BF16))
    qaug, ocmp = _nsa_cmp(proj, kblk, vblk, bsz, seq, cos_t, s1_t, s2_t)
    y_att = _nsa_attn(qaug, ocmp, ksa, vsa, kwa, vwa, proj, bsz, seq)

    h1 = _out_proj(h2d, y_ssd, y_att, w_out.astype(BF16))
    return h1, (ffn_norm_w, w_gate, w_up, w_down)


def kernel(x, attn_norm_w, w_in, conv_w, conv_b, dt_bias, a_log, d_skip, ssd_norm_w, cmp_w1_k, cmp_w2_k,
           cmp_w1_v, cmp_w2_v, cmp_pe_k, cmp_pe_v, w_out, ffn_norm_w, w_gate, w_up, w_down, final_norm_w):
    bsz, seq, _ = x.shape
    depth = w_in.shape[0]
    assert depth == 1, "the final rmsnorm is fused into the last layer's ffn kernel"
    h = x.reshape(bsz * seq, D_MODEL)
    l = 0
    params = (attn_norm_w[l], w_in[l], conv_w[l], conv_b[l], dt_bias[l], a_log[l], d_skip[l], ssd_norm_w[l],
              cmp_w1_k[l], cmp_w2_k[l], cmp_w1_v[l], cmp_w2_v[l], cmp_pe_k[l], cmp_pe_v[l], w_out[l],
              ffn_norm_w[l], w_gate[l], w_up[l], w_down[l])
    h1, (fnw, wg_b, wu_b, wd_b) = _layer(h, bsz, seq, params)
    out = _ffn(h1, fnw[None, :], final_norm_w[None, :], wg_b, wu_b, wd_b)
    return out.reshape(bsz, seq, D_MODEL)
```

```python
import functools

import numpy as np
import jax
import jax.numpy as jnp
from jax import lax
from jax.experimental import pallas as pl
from jax.experimental.pallas import tpu as pltpu

F32 = jnp.float32
BF16 = jnp.bfloat16
HI = lax.Precision.HIGHEST

D_MODEL = 2048
SSD_WIDTH = 1024
ATT_WIDTH = 1024
SSD_HEAD_DIM = 64
SSD_HEADS = 16
SSD_GROUPS = 2
SSD_STATE = 128
SSD_CHUNK = 128
CONV_WIDTH = 4
CONV_CH = SSD_WIDTH + 2 * SSD_GROUPS * SSD_STATE
HEAD_DIM = 64
ATT_HEADS = 16
KV_HEADS = 4
ATT_GROUP = 4
KV_WIDTH = KV_HEADS * HEAD_DIM
CMP_BLOCK = 32
CMP_STRIDE = 16
CMP_HIDDEN = 256
SEL_BLOCK = 64
N_SELECT = 16
WINDOW = 512
N_BRANCH = 3
ROPE_THETA = 500000.0
ROPE_DIM = 16
D_FF = 5632
NORM_EPS = 1e-6
NEG_INF = -1e30
FORCE_SCORE = 1e4
SCALE = HEAD_DIM ** -0.5
BLOCK_BIAS = -(2.0 ** 100)

LANES = 128
VMEM_LIMIT = 56 * 1024 * 1024

NP = 5376
COL_XBC = 0
COL_KV = 1536
COL_Z = 3072
COL_Q = 4096
COL_DT = 5120
COL_GATE = 5248

TQ = 256
TK = 512
ROWS = ATT_GROUP * TQ


def _silu(x):
    return x * jax.nn.sigmoid(x)


def _split_bf16(x, terms):
    out = []
    for _ in range(terms - 1):
        t = x.astype(BF16)
        out.append(t)
        x = x - t.astype(F32)
    out.append(x.astype(BF16))
    return out


def _cparams(sem):
    return pltpu.CompilerParams(dimension_semantics=sem, vmem_limit_bytes=VMEM_LIMIT)


def _inproj_kernel(x_ref, nw_ref, w_ref, o_ref, u_ref):
    @pl.when(pl.program_id(1) == 0)
    def _():
        x = x_ref[...]
        ms = jnp.mean(x * x, axis=-1, keepdims=True)
        u_ref[...] = ((x * lax.rsqrt(ms + NORM_EPS)) * nw_ref[...]).astype(BF16)

    o_ref[...] = jnp.dot(u_ref[...], w_ref[...], preferred_element_type=F32)


def _in_proj(x2, norm_w, w_perm):
    t = x2.shape[0]
    tm, tn = 1024, 768
    return pl.pallas_call(
        _inproj_kernel,
        grid=(t // tm, NP // tn),
        in_specs=[
            pl.BlockSpec((tm, D_MODEL), lambda i, j: (i, 0)),
            pl.BlockSpec((1, D_MODEL), lambda i, j: (0, 0)),
            pl.BlockSpec((D_MODEL, tn), lambda i, j: (0, j)),
        ],
        out_specs=pl.BlockSpec((tm, tn), lambda i, j: (i, j)),
        out_shape=jax.ShapeDtypeStruct((t, NP), F32),
        scratch_shapes=[pltpu.VMEM((tm, D_MODEL), BF16)],
        compiler_params=_cparams(("parallel", "arbitrary")),
        name="in_proj",
    )(x2, norm_w, w_perm)


def _ssd_kernel(xbc_ref, z_ref, dt_ref, cw_ref, cb_ref, dtb_ref, alog_ref, dskip_ref, nw_ref,
                ltri_ref, ex_ref, o_ref, ext_ref, state_ref, y_ref):
    L = SSD_CHUNK
    c = pl.program_id(1)

    @pl.when(c == 0)
    def _():
        ext_ref[0:8, :] = jnp.zeros((8, CONV_CH), F32)
        state_ref[...] = jnp.zeros(state_ref.shape, F32)

    ext_ref[8:8 + L, :] = xbc_ref[...]
    w = cw_ref[...]
    y = (ext_ref[5:5 + L, :] * w[0:1, :] + ext_ref[6:6 + L, :] * w[1:2, :]
         + ext_ref[7:7 + L, :] * w[2:3, :] + ext_ref[8:8 + L, :] * w[3:4, :]) + cb_ref[...]
    tail = ext_ref[L:L + 8, :]
    ext_ref[0:8, :] = tail
    act = _silu(y)
    xs = act[:, :SSD_WIDTH]
    bm = act[:, SSD_WIDTH:SSD_WIDTH + SSD_GROUPS * SSD_STATE]
    cm = act[:, SSD_WIDTH + SSD_GROUPS * SSD_STATE:]

    v = dt_ref[...] + dtb_ref[...]
    dt = jnp.maximum(v, 0.0) + jnp.log1p(jnp.exp(-jnp.abs(v)))
    a = -jnp.exp(alog_ref[...])
    adt = a * dt
    ltri = ltri_ref[...]
    acum = sum(jnp.dot(ltri, t, preferred_element_type=F32) for t in _split_bf16(adt, 3))
    acum_t = acum.T
    last = acum[L - 1:L, :]
    stacked = jnp.concatenate([dt, jnp.exp(acum), jnp.exp(last - acum)], axis=0)
    ex = ex_ref[...]
    expanded = sum(jnp.dot(t, ex, preferred_element_type=F32) for t in _split_bf16(stacked, 2))
    dt_e = expanded[0:L]
    expa_e = expanded[L:2 * L]
    dst_e = expanded[2 * L:3 * L]

    xdt = xs * dt_e
    xds_b = (xdt * dst_e).astype(BF16)
    row = lax.broadcasted_iota(jnp.int32, (L, L), 0)
    col = lax.broadcasted_iota(jnp.int32, (L, L), 1)
    causal = row >= col
    lane = lax.broadcasted_iota(jnp.int32, (L, LANES), 1)
    hg = SSD_HEADS // SSD_GROUPS
    gw = hg * SSD_HEAD_DIM
    for g in range(SSD_GROUPS):
        bg = bm[:, g * SSD_STATE:(g + 1) * SSD_STATE]
        cg_b = cm[:, g * SSD_STATE:(g + 1) * SSD_STATE].astype(BF16)
        bg_b = bg.astype(BF16)
        cb = lax.dot_general(cg_b, bg_b, (((1,), (1,)), ((), ())), preferred_element_type=F32)
        hprev = state_ref[g]
        yoff = jnp.dot(cg_b, hprev.astype(BF16), preferred_element_type=F32) * expa_e[:, g * gw:(g + 1) * gw]
        snew = jnp.dot(bg.T.astype(BF16), xds_b[:, g * gw:(g + 1) * gw], preferred_element_type=F32)
        state_ref[g] = hprev * expa_e[L - 1:L, g * gw:(g + 1) * gw] + snew
        for k in range(hg // 2):
            pair = g * (hg // 2) + k
            ms = []
            for hh in (2 * pair, 2 * pair + 1):
                seg = acum[:, hh:hh + 1] - acum_t[hh:hh + 1, :]
                decay = jnp.exp(jnp.where(causal, seg, -jnp.inf))
                ms.append((cb * decay).astype(BF16))
            mpair = jnp.concatenate(ms, axis=1)
            xp = xdt[:, pair * LANES:(pair + 1) * LANES]
            xblk = jnp.concatenate([jnp.where(lane < SSD_HEAD_DIM, xp, 0.0),
                                    jnp.where(lane >= SSD_HEAD_DIM, xp, 0.0)], axis=0).astype(BF16)
            yd = jnp.dot(mpair, xblk, preferred_element_type=F32)
            y_ref[:, pair * LANES:(pair + 1) * LANES] = yd + yoff[:, k * LANES:(k + 1) * LANES]

    yy = y_ref[...] + dskip_ref[...] * xs
    z = z_ref[...]
    yz = yy * _silu(z)
    ms2 = jnp.mean(yz * yz, axis=-1, keepdims=True)
    o_ref[...] = ((yz * lax.rsqrt(ms2 + NORM_EPS)) * nw_ref[...]).astype(o_ref.dtype)


def _ssd(proj, bsz, seq, conv_w, conv_b, dtb_pad, alog_pad, dskip_e, norm_w):
    nc = seq // SSD_CHUNK
    L = SSD_CHUNK
    ltri = jnp.asarray(np.tril(np.ones((L, L), np.float32)), BF16)
    ex = np.zeros((LANES, SSD_WIDTH), np.float32)
    for h in range(SSD_HEADS):
        ex[h, h * SSD_HEAD_DIM:(h + 1) * SSD_HEAD_DIM] = 1.0
    ex = jnp.asarray(ex, BF16)
    const = lambda shape: pl.BlockSpec(shape, lambda b, c: (0,) * len(shape))
    return pl.pallas_call(
        _ssd_kernel,
        grid=(bsz, nc),
        in_specs=[
            pl.BlockSpec((L, CONV_CH), lambda b, c: (b * nc + c, COL_XBC // CONV_CH)),
            pl.BlockSpec((L, SSD_WIDTH), lambda b, c: (b * nc + c, COL_Z // SSD_WIDTH)),
            pl.BlockSpec((L, LANES), lambda b, c: (b * nc + c, COL_DT // LANES)),
            const((CONV_WIDTH, CONV_CH)), const((1, CONV_CH)), const((1, LANES)), const((1, LANES)),
            const((1, SSD_WIDTH)), const((1, SSD_WIDTH)), const((L, L)), const((LANES, SSD_WIDTH)),
        ],
        out_specs=pl.BlockSpec((L, SSD_WIDTH), lambda b, c: (b * nc + c, 0)),
        out_shape=jax.ShapeDtypeStruct((bsz * seq, SSD_WIDTH), BF16),
        scratch_shapes=[
            pltpu.VMEM((L + 8, CONV_CH), F32),
            pltpu.VMEM((SSD_GROUPS, SSD_STATE, (SSD_HEADS // SSD_GROUPS) * SSD_HEAD_DIM), F32),
            pltpu.VMEM((L, SSD_WIDTH), F32),
        ],
        compiler_params=_cparams(("parallel", "arbitrary")),
        name="ssd",
    )(proj, proj, proj, conv_w, conv_b, dtb_pad, alog_pad, dskip_e, norm_w, ltri, ex)


def _rope(x, cos, s1, s2):
    n = x.shape[-1]
    half = ROPE_DIM // 2
    return x * cos + pltpu.roll(x, n - half, 1) * s1 + pltpu.roll(x, half, 1) * s2


def _merge_heads(parts, upper=False):
    lane = lax.broadcasted_iota(jnp.int32, parts[0].shape, 1)
    out = []
    for k in range(len(parts) // 2):
        a, b = parts[2 * k], parts[2 * k + 1]
        if upper:
            out.append(jnp.where(lane < HEAD_DIM, pltpu.roll(a, HEAD_DIM, 1), b))
        else:
            out.append(jnp.where(lane < HEAD_DIM, a, pltpu.roll(b, HEAD_DIM, 1)))
    return jnp.concatenate(out, axis=1)


def _kvprep_kernel(ks_ref, vs_ref, kw_ref, vw_ref, cos_ref, s1_ref, s2_ref, sel_ref,
                   kso_ref, vso_ref, kwo_ref, vwo_ref, *, tiles_per_seq):
    tr = ks_ref.shape[0]
    wide = KV_HEADS * LANES
    s0 = (pl.program_id(0) % tiles_per_seq) * tr
    cos, s1, s2 = cos_ref[...], s1_ref[...], s2_ref[...]
    sel = sel_ref[...]
    lane = lax.broadcasted_iota(jnp.int32, (tr, wide), 1) % LANES
    blk = (s0 + lax.broadcasted_iota(jnp.int32, (tr, wide), 0)) // SEL_BLOCK
    ebias = jnp.where(lane - SEL_BLOCK == blk, BLOCK_BIAS, 0.0)
    ones = jnp.where(lane >= HEAD_DIM, 1.0, 0.0)

    def place(x):
        return jnp.dot(x.astype(BF16), sel, preferred_element_type=F32)

    kso_ref[...] = (place(_rope(ks_ref[...], cos, s1, s2)) + ebias).astype(BF16)
    kwo_ref[...] = place(_rope(kw_ref[...], cos, s1, s2)).astype(BF16)
    vso_ref[...] = (place(vs_ref[...]) + ones).astype(BF16)
    vwo_ref[...] = (place(vw_ref[...]) + ones).astype(BF16)


def _head_place_matrix():
    m = np.zeros((KV_WIDTH, KV_HEADS * LANES), np.float32)
    for h in range(KV_HEADS):
        for d in range(HEAD_DIM):
            m[h * HEAD_DIM + d, h * LANES + d] = 1.0
    return m


def _kv_prep(proj, seq, cos_t, s1_t, s2_t):
    t = proj.shape[0]
    tr = 512
    tps = seq // tr
    sel = jnp.asarray(_head_place_matrix(), BF16)
    kvb = COL_KV // KV_WIDTH
    seg = lambda k: pl.BlockSpec((tr, KV_WIDTH), lambda i: (i, kvb + k))
    tab = pl.BlockSpec((tr, KV_WIDTH), lambda i: (i % tps, 0))
    wide = KV_HEADS * LANES
    out = pl.BlockSpec((tr, wide), lambda i: (i, 0))
    shp = jax.ShapeDtypeStruct((t, wide), BF16)
    return pl.pallas_call(
        functools.partial(_kvprep_kernel, tiles_per_seq=tps),
        grid=(t // tr,),
        in_specs=[seg(2), seg(3), seg(4), seg(5), tab, tab, tab,
                  pl.BlockSpec((KV_WIDTH, wide), lambda i: (0, 0))],
        out_specs=[out, out, out, out],
        out_shape=[shp, shp, shp, shp],
        compiler_params=_cparams(("parallel",)),
        name="kv_prep",
    )(proj, proj, proj, proj, cos_t, s1_t, s2_t, sel)


def _compress_kernel(gk_ref, gv_ref, pek_ref, pev_ref, w1k_ref, w1v_ref, w2kt_ref, w2v_ref,
                     kblk_ref, vblk_ref, shift_ref):
    ng = gk_ref.shape[0] // CMP_STRIDE
    heads_per_block = LANES // HEAD_DIM
    mine = (lax.broadcasted_iota(jnp.int32, (ng, LANES), 1) // HEAD_DIM) == pl.program_id(1) % heads_per_block

    def hidden(x_ref, pe_ref, w1_ref):
        top = jnp.zeros((ng, CMP_HIDDEN), F32)
        bot = jnp.zeros((ng, CMP_HIDDEN), F32)
        for r in range(CMP_STRIDE):
            x = x_ref[pl.ds(r, ng, stride=CMP_STRIDE), :]
            xt = jnp.where(mine, x + pe_ref[r:r + 1, :], 0.0).astype(BF16)
            xb = jnp.where(mine, x + pe_ref[CMP_STRIDE + r:CMP_STRIDE + r + 1, :], 0.0).astype(BF16)
            top = top + jnp.dot(xt, w1_ref[r], preferred_element_type=F32)
            bot = bot + jnp.dot(xb, w1_ref[CMP_STRIDE + r], preferred_element_type=F32)
        shift_ref[0:ng, :] = bot
        shift_ref[ng:ng + 8, :] = jnp.zeros((8, CMP_HIDDEN), F32)
        return _silu(top + shift_ref[1:ng + 1, :]).astype(BF16)

    hk = hidden(gk_ref, pek_ref, w1k_ref)
    kt = lax.dot_general(w2kt_ref[...], hk, (((1,), (1,)), ((), ())), preferred_element_type=F32)
    kblk_ref[...] = jnp.zeros(kblk_ref.shape, BF16)
    for g in range(ATT_GROUP):
        kblk_ref[0, 0, g * HEAD_DIM:(g + 1) * HEAD_DIM, g * ng:(g + 1) * ng] = kt.astype(BF16)

    hv = hidden(gv_ref, pev_ref, w1v_ref)
    vc = jnp.dot(hv, w2v_ref[...], preferred_element_type=F32)
    vblk_ref[...] = jnp.zeros(vblk_ref.shape, BF16)
    for g in range(ATT_GROUP):
        vblk_ref[0, 0, g * ng:(g + 1) * ng, g * LANES:(g + 1) * LANES] = vc.astype(BF16)


def _compress(proj, bsz, seq, pek, pev, w1k, w1v, w2kt, w2v):
    ng = seq // CMP_STRIDE
    hpb = LANES // HEAD_DIM
    kcb = COL_KV // LANES
    vcb = kcb + KV_WIDTH // LANES
    const = lambda shape: pl.BlockSpec(shape, lambda b, h: (0,) * len(shape))
    return pl.pallas_call(
        _compress_kernel,
        grid=(bsz, KV_HEADS),
        in_specs=[pl.BlockSpec((seq, LANES), lambda b, h: (b, kcb + h // hpb)),
                  pl.BlockSpec((seq, LANES), lambda b, h: (b, vcb + h // hpb)),
                  const((CMP_BLOCK, LANES)), const((CMP_BLOCK, LANES)),
                  const((CMP_BLOCK, LANES, CMP_HIDDEN)), const((CMP_BLOCK, LANES, CMP_HIDDEN)),
                  const((HEAD_DIM, CMP_HIDDEN)), const((CMP_HIDDEN, LANES))],
        out_specs=[pl.BlockSpec((1, 1, KV_WIDTH, ATT_GROUP * ng), lambda b, h: (b, h, 0, 0)),
                   pl.BlockSpec((1, 1, ATT_GROUP * ng, ATT_GROUP * LANES), lambda b, h: (b, h, 0, 0))],
        out_shape=[jax.ShapeDtypeStruct((bsz, KV_HEADS, KV_WIDTH, ATT_GROUP * ng), BF16),
                   jax.ShapeDtypeStruct((bsz, KV_HEADS, ATT_GROUP * ng, ATT_GROUP * LANES), BF16)],
        scratch_shapes=[pltpu.VMEM((ng + 8, CMP_HIDDEN), F32)],
        compiler_params=_cparams(("parallel", "parallel")),
        name="compress",
    )(proj, proj, pek, pev, w1k, w1v, w2kt, w2v)


TQC = TQ
SUBLANES = 8


CMP_HEADS_PER_STEP = 2


def _nsacmp_kernel(q_ref, kblk_ref, vblk_ref, cos_ref, s1_ref, s2_ref, ovt_ref, selq_ref, placen_ref, eye_ref,
                   qaug_ref, ocmp_ref):
    for hh in range(CMP_HEADS_PER_STEP):
        lanes = slice(hh * KV_WIDTH, (hh + 1) * KV_WIDTH)
        _nsacmp_head(q_ref.at[:, lanes], kblk_ref.at[0, hh], vblk_ref.at[0, hh], cos_ref, s1_ref, s2_ref,
                     ovt_ref, selq_ref, placen_ref, eye_ref, qaug_ref.at[0, hh], ocmp_ref.at[:, lanes])


def _nsacmp_head(q_ref, kblk_ref, vblk_ref, cos_ref, s1_ref, s2_ref, ovt_ref, selq_ref, placen_ref, eye_ref,
                 qaug_ref, ocmp_ref):
    ncp = kblk_ref.shape[1] // ATT_GROUP
    nsel = ovt_ref.shape[0]
    nt = (((1,), (1,)), ((), ()))
    s0 = pl.program_id(2) * TQC
    q = q_ref[...]
    s_all = jnp.dot((q * SCALE).astype(BF16), kblk_ref[...], preferred_element_type=F32)
    tq_pos = s0 + lax.broadcasted_iota(jnp.int32, (TQC, ncp), 0)
    cmp_end = lax.broadcasted_iota(jnp.int32, (TQC, ncp), 1) * CMP_STRIDE + (CMP_BLOCK - 1)
    mask = cmp_end <= tq_pos
    any_visible = (s0 + lax.broadcasted_iota(jnp.int32, (TQC, 1), 0) >= CMP_BLOCK - 1).astype(F32)
    ps = []
    for g in range(ATT_GROUP):
        s = jnp.where(mask, s_all[:, g * ncp:(g + 1) * ncp], NEG_INF)
        m = jnp.max(s, axis=-1, keepdims=True)
        e = jnp.exp(s - m)
        ps.append(e * (any_visible / jnp.sum(e, axis=-1, keepdims=True)))
    p_all = jnp.concatenate(ps, axis=1)

    def stack_rows(wide):
        return jnp.concatenate([wide[u * TQ:(u + 1) * TQ, g * LANES:(g + 1) * LANES]
                                for u in range(TQC // TQ) for g in range(ATT_GROUP)], axis=0)

    ow = jnp.dot(p_all.astype(BF16), vblk_ref[...], preferred_element_type=F32)
    ocmp_ref[...] = _merge_heads([ow[:, g * LANES:(g + 1) * LANES] for g in range(ATT_GROUP)])

    psum = (ps[0] + ps[1]) + (ps[2] + ps[3])
    p_hi = psum.astype(BF16)
    p_lo = (psum - p_hi.astype(F32)).astype(BF16)
    ovt = ovt_ref[...]
    imp = (lax.dot_general(ovt, p_hi, nt, preferred_element_type=F32)
           + lax.dot_general(ovt, p_lo, nt, preferred_element_type=F32))
    j = lax.broadcasted_iota(jnp.int32, (nsel, TQC), 0)
    cur = (s0 + lax.broadcasted_iota(jnp.int32, (nsel, TQC), 1)) // SEL_BLOCK
    imp = jnp.where((j == 0) | (j == cur) | (j == cur - 1), FORCE_SCORE, imp)
    imp = jnp.where(j <= cur, imp, -1.0)
    nblk = nsel // SUBLANES
    blocks = [imp[k * SUBLANES:(k + 1) * SUBLANES, :] for k in range(nblk)]
    ranks = [jnp.zeros((SUBLANES, TQC), F32) for _ in range(nblk)]
    sub = lax.broadcasted_iota(jnp.int32, (SUBLANES, TQC), 0)
    for jp in range(nsel):
        r = jnp.broadcast_to(imp[jp:jp + 1, :], (SUBLANES, TQC))
        for k in range(nblk):
            if k > jp // SUBLANES:
                inc = jnp.where(r >= blocks[k], 1.0, 0.0)
            elif k < jp // SUBLANES:
                inc = jnp.where(r > blocks[k], 1.0, 0.0)
            else:
                tie = jnp.where(sub > jp % SUBLANES, 1.0, 0.0)
                inc = jnp.where(r > blocks[k], 1.0, jnp.where(r == blocks[k], tie, 0.0))
            ranks[k] = ranks[k] + inc
    rank = jnp.concatenate(ranks, axis=0)
    notsel_t = jnp.where((rank < float(N_SELECT)) & (imp >= 0.0), 0.0, 1.0).astype(BF16)
    notsel = lax.dot_general(eye_ref[...], notsel_t, nt, preferred_element_type=F32)

    q_rot = (_rope(q, cos_ref[...], s1_ref[...], s2_ref[...]) * SCALE).astype(BF16)
    qw = (jnp.dot(q_rot, selq_ref[...], preferred_element_type=F32)
          + jnp.dot(notsel.astype(BF16), placen_ref[...], preferred_element_type=F32))
    qaug_ref[...] = stack_rows(qw).astype(BF16)


def _selection_overlap_t(n_cmp_pad, n_cmp, n_sel):
    cs = np.arange(n_cmp)[:, None] * CMP_STRIDE
    ce = cs + CMP_BLOCK
    ss = np.arange(n_sel)[None, :] * SEL_BLOCK
    se = ss + SEL_BLOCK
    ov = np.clip(np.minimum(ce, se) - np.maximum(cs, ss), 0, None) / CMP_BLOCK
    full = np.zeros((n_cmp_pad, n_sel), np.float32)
    full[:n_cmp] = ov
    return full.T.copy()


def _nsa_cmp(proj, kblk, vblk, bsz, seq, cos_t, s1_t, s2_t):
    nqc = seq // TQC
    ncp = kblk.shape[3] // ATT_GROUP
    n_cmp = (seq - CMP_BLOCK) // CMP_STRIDE + 1
    nsel = seq // SEL_BLOCK
    ovt = jnp.asarray(_selection_overlap_t(ncp, n_cmp, nsel), BF16)
    selq = jnp.asarray(_head_place_matrix(), BF16)
    placen = np.zeros((nsel, ATT_GROUP * LANES), np.float32)
    for g in range(ATT_GROUP):
        for jb in range(nsel):
            placen[jb, g * LANES + HEAD_DIM + jb] = 1.0
    placen = jnp.asarray(placen, BF16)
    eye = jnp.asarray(np.eye(TQC, dtype=np.float32), BF16)
    const = lambda shape: pl.BlockSpec(shape, lambda b, h, t: (0,) * len(shape))
    tab = pl.BlockSpec((TQC, KV_WIDTH), lambda b, h, t: (t, 0))
    hps = CMP_HEADS_PER_STEP
    qb = COL_Q // (hps * KV_WIDTH)
    rows = (TQC // TQ) * ROWS
    stacked = pl.BlockSpec((1, hps, rows, LANES), lambda b, h, t: (b, h, t, 0))
    return pl.pallas_call(
        _nsacmp_kernel,
        grid=(bsz, KV_HEADS // hps, nqc),
        in_specs=[
            pl.BlockSpec((TQC, hps * KV_WIDTH), lambda b, h, t: (b * nqc + t, qb + h)),
            pl.BlockSpec((1, hps, KV_WIDTH, ATT_GROUP * ncp), lambda b, h, t: (b, h, 0, 0)),
            pl.BlockSpec((1, hps, ATT_GROUP * ncp, ATT_GROUP * LANES), lambda b, h, t: (b, h, 0, 0)),
            tab, tab, tab,
            const((nsel, ncp)), const((KV_WIDTH, ATT_GROUP * LANES)),
            const((nsel, ATT_GROUP * LANES)), const((TQC, TQC)),
        ],
        out_specs=[stacked, pl.BlockSpec((TQC, hps * KV_WIDTH), lambda b, h, t: (b * nqc + t, h))],
        out_shape=[jax.ShapeDtypeStruct((bsz, KV_HEADS, nqc * rows, LANES), BF16),
                   jax.ShapeDtypeStruct((bsz * seq, ATT_WIDTH), F32)],
        compiler_params=_cparams(("parallel", "parallel", "parallel")),
        name="nsa_cmp",
    )(proj, kblk, vblk, cos_t, s1_t, s2_t, ovt, selq, placen, eye)


ATT_HEADS_PER_STEP = 2


def _nsaattn_kernel(qaug_ref, ocmp_ref, ks_ref, vs_ref, kw_ref, vw_ref, gate_ref, gexp_ref, db_ref, wb_ref, o_ref,
                    m_ref, acc_ref):
    qt = pl.program_id(2)
    s0 = qt * TQ
    nt = (((1,), (1,)), ((), ()))
    heads = range(ATT_HEADS_PER_STEP)
    qs = [qaug_ref[0, hh] for hh in heads]
    klanes = [slice(hh * LANES, (hh + 1) * LANES) for hh in heads]
    olanes = [slice(hh * KV_WIDTH, (hh + 1) * KV_WIDTH) for hh in heads]

    def per_head(bias):
        return jnp.concatenate([bias] * ATT_GROUP, axis=0)

    m_ref[...] = jnp.full(m_ref.shape, NEG_INF, F32)
    acc_ref[...] = jnp.zeros(acc_ref.shape, F32)

    def tile(first_key, width, causal=False):
        start = pl.multiple_of(first_key, width)
        for hh in heads:
            k = ks_ref[pl.ds(start, width), klanes[hh]]
            v = vs_ref[pl.ds(start, width), klanes[hh]]
            s = lax.dot_general(qs[hh], k, nt, preferred_element_type=F32)
            if causal:
                s = s + per_head(db_ref[0][:, :width])
            m_prev = m_ref[hh]
            m_next = jnp.maximum(m_prev, jnp.max(s, axis=-1, keepdims=True))
            p = jnp.exp(s - jnp.concatenate([m_next] * (width // LANES), axis=1))
            acc_ref[hh] = acc_ref[hh] * jnp.exp(m_prev - m_next) + jnp.dot(
                p.astype(BF16), v, preferred_element_type=F32)
            m_ref[hh] = m_next

    nfull = (qt * TQ) // TK

    def body(i, carry):
        tile(i * (2 * TK), 2 * TK)
        return carry

    lax.fori_loop(0, nfull // 2, body, 0)

    @pl.when(nfull % 2 == 1)
    def _():
        tile((nfull - 1) * TK, TK)

    @pl.when(s0 == nfull * TK)
    def _():
        tile(s0, TQ, causal=True)

    @pl.when(s0 != nfull * TK)
    def _():
        tile(nfull * TK, TK, causal=True)

    def normalised(x):
        parts = [x[g * TQ:(g + 1) * TQ] for g in range(ATT_GROUP)]
        return _merge_heads(parts) / _merge_heads(parts, upper=True)

    wk = WINDOW + TQ
    wstart = pl.multiple_of(jnp.maximum(s0 - WINDOW, 0), TQ)
    sig_terms = _split_bf16(jax.nn.sigmoid(gate_ref[...]), 2)
    for hh in heads:
        kwin = kw_ref[pl.ds(wstart, wk), klanes[hh]]
        vwin = vw_ref[pl.ds(wstart, wk), klanes[hh]]
        sw = lax.dot_general(qs[hh], kwin, nt, preferred_element_type=F32) + per_head(wb_ref[0])
        pw = jnp.exp(sw - jnp.max(sw, axis=-1, keepdims=True))
        ow = jnp.dot(pw.astype(BF16), vwin, preferred_element_type=F32)

        gexp = gexp_ref[hh]
        gmap = sum(jnp.dot(t, gexp, preferred_element_type=F32) for t in sig_terms)
        y = (gmap[:, 0:KV_WIDTH] * ocmp_ref[:, olanes[hh]]
             + gmap[:, KV_WIDTH:2 * KV_WIDTH] * normalised(acc_ref[hh])
             + gmap[:, 2 * KV_WIDTH:3 * KV_WIDTH] * normalised(ow))
        o_ref[:, olanes[hh]] = y.astype(o_ref.dtype)


def _nsa_attn(qaug, ocmp, ksa, vsa, kwa, vwa, proj, bsz, seq):
    nqt = seq // TQ
    gexp = np.zeros((KV_HEADS, LANES, N_BRANCH * KV_WIDTH), np.float32)
    for h in range(KV_HEADS):
        for g in range(ATT_GROUP):
            for br in range(N_BRANCH):
                src = (h * ATT_GROUP + g) * N_BRANCH + br
                gexp[h, src, br * KV_WIDTH + g * HEAD_DIM:br * KV_WIDTH + (g + 1) * HEAD_DIM] = 1.0
    gexp = jnp.asarray(gexp, BF16)
    r = np.arange(TQ)[:, None]
    diag_cases = TK // TQ
    db = np.stack([np.where(np.arange(TK)[None, :] <= c * TQ + r, 0.0, NEG_INF) for c in range(diag_cases)])
    win_cases = WINDOW // TQ + 1
    wk = WINDOW + TQ
    dist = lambda c: c * TQ + r - np.arange(wk)[None, :]
    wb = np.stack([np.where((dist(c) >= 0) & (dist(c) < WINDOW), 0.0, NEG_INF) for c in range(win_cases)])
    db = jnp.asarray(db, F32)
    wb = jnp.asarray(wb, F32)
    hps = ATT_HEADS_PER_STEP
    stacked = pl.BlockSpec((1, hps, ROWS, LANES), lambda b, h, t: (b, h, t, 0))
    kv = pl.BlockSpec((seq, hps * LANES), lambda b, h, t: (b, h))
    merged = pl.BlockSpec((TQ, hps * KV_WIDTH), lambda b, h, t: (b * nqt + t, h))
    return pl.pallas_call(
        _nsaattn_kernel,
        grid=(bsz, KV_HEADS // hps, nqt),
        in_specs=[stacked, merged, kv, kv, kv, kv,
                  pl.BlockSpec((TQ, LANES), lambda b, h, t: (b * nqt + t, COL_GATE // LANES)),
                  pl.BlockSpec((hps, LANES, N_BRANCH * KV_WIDTH), lambda b, h, t: (h, 0, 0)),
                  pl.BlockSpec((1, TQ, TK), lambda b, h, t: (t % diag_cases, 0, 0)),
                  pl.BlockSpec((1, TQ, wk), lambda b, h, t: (jnp.minimum(t, win_cases - 1), 0, 0))],
        out_specs=merged,
        out_shape=jax.ShapeDtypeStruct((bsz * seq, ATT_WIDTH), BF16),
        scratch_shapes=[pltpu.VMEM((hps, ROWS, LANES), F32), pltpu.VMEM((hps, ROWS, LANES), F32)],
        compiler_params=_cparams(("parallel", "parallel", "arbitrary")),
        name="nsa_attn",
    )(qaug, ocmp, ksa, vsa, kwa, vwa, proj, gexp, db, wb)


def _outproj_kernel(x_ref, ys_ref, ya_ref, w1_ref, w2_ref, o_ref):
    o_ref[...] = (x_ref[...] + jnp.dot(ys_ref[...], w1_ref[...], preferred_element_type=F32)
                  + jnp.dot(ya_ref[...], w2_ref[...], preferred_element_type=F32))


def _out_proj(x2, ys, ya, wo_b):
    t = x2.shape[0]
    tm = 512
    return pl.pallas_call(
        _outproj_kernel,
        grid=(t // tm,),
        in_specs=[
            pl.BlockSpec((tm, D_MODEL), lambda i: (i, 0)),
            pl.BlockSpec((tm, SSD_WIDTH), lambda i: (i, 0)),
            pl.BlockSpec((tm, ATT_WIDTH), lambda i: (i, 0)),
            pl.BlockSpec((SSD_WIDTH, D_MODEL), lambda i: (0, 0)),
            pl.BlockSpec((ATT_WIDTH, D_MODEL), lambda i: (1, 0)),
        ],
        out_specs=pl.BlockSpec((tm, D_MODEL), lambda i: (i, 0)),
        out_shape=jax.ShapeDtypeStruct((t, D_MODEL), F32),
        compiler_params=_cparams(("parallel",)),
        name="out_proj",
    )(x2, ys, ya, wo_b, wo_b)


def _ffn_kernel(h_ref, nw_ref, fw_ref, wg_ref, wu_ref, wd_ref, o_ref, v_ref):
    j = pl.program_id(1)

    @pl.when(j == 0)
    def _():
        h = h_ref[...]
        ms = jnp.mean(h * h, axis=-1, keepdims=True)
        v_ref[...] = ((h * lax.rsqrt(ms + NORM_EPS)) * nw_ref[...]).astype(BF16)
        o_ref[...] = h

    v = v_ref[...]
    gate = jnp.dot(v, wg_ref[...].astype(BF16), preferred_element_type=F32)
    up = jnp.dot(v, wu_ref[...].astype(BF16), preferred_element_type=F32)
    o_ref[...] += jnp.dot((_silu(gate) * up).astype(BF16), wd_ref[...].astype(BF16), preferred_element_type=F32)

    @pl.when(j == pl.num_programs(1) - 1)
    def _():
        h2 = o_ref[...]
        ms = jnp.mean(h2 * h2, axis=-1, keepdims=True)
        o_ref[...] = (h2 * lax.rsqrt(ms + NORM_EPS)) * fw_ref[...]


def _ffn(h1, ffn_nw, final_w, wg_b, wu_b, wd_b):
    t = h1.shape[0]
    tm, tf = 1024, 256
    return pl.pallas_call(
        _ffn_kernel,
        grid=(t // tm, D_FF // tf),
        in_specs=[
            pl.BlockSpec((tm, D_MODEL), lambda i, j: (i, 0)),
            pl.BlockSpec((1, D_MODEL), lambda i, j: (0, 0)),
            pl.BlockSpec((1, D_MODEL), lambda i, j: (0, 0)),
            pl.BlockSpec((D_MODEL, tf), lambda i, j: (0, j)),
            pl.BlockSpec((D_MODEL, tf), lambda i, j: (0, j)),
            pl.BlockSpec((tf, D_MODEL), lambda i, j: (j, 0)),
        ],
        out_specs=pl.BlockSpec((tm, D_MODEL), lambda i, j: (i, 0)),
        out_shape=jax.ShapeDtypeStruct((t, D_MODEL), F32),
        scratch_shapes=[pltpu.VMEM((tm, D_MODEL), BF16)],
        compiler_params=_cparams(("parallel", "arbitrary")),
        name="ffn",
    )(h1, ffn_nw, final_w, wg_b, wu_b, wd_b)


def _rope_tables(seq):
    f32 = np.float32
    inv = (f32(1.0) / (f32(ROPE_THETA) ** (np.arange(0, ROPE_DIM, 2, dtype=f32) / f32(ROPE_DIM)))).astype(f32)
    ang = (np.arange(seq, dtype=f32)[:, None] * inv[None, :]).astype(f32)
    cos, sin = np.cos(ang).astype(f32), np.sin(ang).astype(f32)
    half = ROPE_DIM // 2
    rest_one = np.ones((seq, HEAD_DIM - ROPE_DIM), f32)
    rest_zero = np.zeros((seq, HEAD_DIM - ROPE_DIM), f32)
    zero_h = np.zeros((seq, half), f32)
    cos_h = np.concatenate([cos, cos, rest_one], axis=1)
    s1_h = np.concatenate([-sin, zero_h, rest_zero], axis=1)
    s2_h = np.concatenate([zero_h, sin, rest_zero], axis=1)
    tile = lambda a: jnp.asarray(np.tile(a, (1, KV_HEADS)))
    return tile(cos_h), tile(s1_h), tile(s2_h)


def _pad_lanes(a, width):
    return jnp.pad(a, ((0, 0), (0, width - a.shape[1])))


def _layer(h2d, bsz, seq, p):
    (attn_norm_w, w_in, conv_w, conv_b, dt_bias, a_log, d_skip, ssd_norm_w, cmp_w1_k, cmp_w2_k, cmp_w1_v,
     cmp_w2_v, cmp_pe_k, cmp_pe_v, w_out, ffn_norm_w, w_gate, w_up, w_down) = p
    o_xbc, o_dt, o_q, o_kv, o_gate = 1024, 2560, 2576, 3600, 5136
    w_perm = jnp.concatenate([
        w_in[:, o_xbc:o_dt], w_in[:, o_kv:o_gate], w_in[:, :o_xbc], w_in[:, o_q:o_kv],
        _pad_lanes(w_in[:, o_dt:o_q], LANES), _pad_lanes(w_in[:, o_gate:], LANES)], axis=1).astype(BF16)
    proj = _in_proj(h2d, attn_norm_w[None, :], w_perm)

    y_ssd = _ssd(proj, bsz, seq, conv_w, conv_b[None, :], _pad_lanes(dt_bias[None, :], LANES),
                 _pad_lanes(a_log[None, :], LANES), jnp.repeat(d_skip, SSD_HEAD_DIM)[None, :],
                 ssd_norm_w[None, :])

    cos_t, s1_t, s2_t = _rope_tables(seq)
    ksa, vsa, kwa, vwa = _kv_prep(proj, seq, cos_t, s1_t, s2_t)

    hpb = LANES // HEAD_DIM
    pe_rep = lambda pe: jnp.tile(pe, (1, hpb))
    w1_rep = lambda w1: jnp.tile(w1.astype(BF16).reshape(CMP_BLOCK, HEAD_DIM, CMP_HIDDEN), (1, hpb, 1))
    kblk, vblk = _compress(proj, bsz, seq, pe_rep(cmp_pe_k), pe_rep(cmp_pe_v),
                           w1_rep(cmp_w1_k), w1_rep(cmp_w1_v), cmp_w2_k.T.astype(BF16),
                           _pad_lanes(cmp_w2_v, LANES).astype(BF16))
    qaug, ocmp = _nsa_cmp(proj, kblk, vblk, bsz, seq, cos_t, s1_t, s2_t)
    y_att = _nsa_attn(qaug, ocmp, ksa, vsa, kwa, vwa, proj, bsz, seq)

    h1 = _out_proj(h2d, y_ssd, y_att, w_out.astype(BF16))
    return h1, (ffn_norm_w, w_gate, w_up, w_down)


def kernel(x, attn_norm_w, w_in, conv_w, conv_b, dt_bias, a_log, d_skip, ssd_norm_w, cmp_w1_k, cmp_w2_k,
           cmp_w1_v, cmp_w2_v, cmp_pe_k, cmp_pe_v, w_out, ffn_norm_w, w_gate, w_up, w_down, final_norm_w):
    bsz, seq, _ = x.shape
    depth = w_in.shape[0]
    assert depth == 1, "the final rmsnorm is fused into the last layer's ffn kernel"
    h = x.reshape(bsz * seq, D_MODEL)
    l = 0
    params = (attn_norm_w[l], w_in[l], conv_w[l], conv_b[l], dt_bias[l], a_log[l], d_skip[l], ssd_norm_w[l],
              cmp_w1_k[l], cmp_w2_k[l], cmp_w1_v[l], cmp_w2_v[l], cmp_pe_k[l], cmp_pe_v[l], w_out[l],
              ffn_norm_w[l], w_gate[l], w_up[l], w_down[l])
    h1, (fnw, wg_b, wu_b, wd_b) = _layer(h, bsz, seq, params)
    out = _ffn(h1, fnw[None, :], final_norm_w[None, :], wg_b, wu_b, wd_b)
    return out.reshape(bsz, seq, D_MODEL)
```

```python
import functools

import numpy as np
import jax
import jax.numpy as jnp
from jax import lax
from jax.experimental import pallas as pl
from jax.experimental.pallas import tpu as pltpu

F32 = jnp.float32
BF16 = jnp.bfloat16
HI = lax.Precision.HIGHEST

D_MODEL = 2048
SSD_WIDTH = 1024
ATT_WIDTH = 1024
SSD_HEAD_DIM = 64
SSD_HEADS = 16
SSD_GROUPS = 2
SSD_STATE = 128
SSD_CHUNK = 128
CONV_WIDTH = 4
CONV_CH = SSD_WIDTH + 2 * SSD_GROUPS * SSD_STATE
HEAD_DIM = 64
ATT_HEADS = 16
KV_HEADS = 4
ATT_GROUP = 4
KV_WIDTH = KV_HEADS * HEAD_DIM
CMP_BLOCK = 32
CMP_STRIDE = 16
CMP_HIDDEN = 256
SEL_BLOCK = 64
N_SELECT = 16
WINDOW = 512
N_BRANCH = 3
ROPE_THETA = 500000.0
ROPE_DIM = 16
D_FF = 5632
NORM_EPS = 1e-6
NEG_INF = -1e30
FORCE_SCORE = 1e4
SCALE = HEAD_DIM ** -0.5
BLOCK_BIAS = -(2.0 ** 100)

LANES = 128
VMEM_LIMIT = 56 * 1024 * 1024

NP = 5376
COL_XBC = 0
COL_KV = 1536
COL_Z = 3072
COL_Q = 4096
COL_DT = 5120
COL_GATE = 5248

TQ = 256
TK = 512
ROWS = ATT_GROUP * TQ


def _silu(x):
    return x * jax.nn.sigmoid(x)


def _split_bf16(x, terms):
    out = []
    for _ in range(terms - 1):
        t = x.astype(BF16)
        out.append(t)
        x = x - t.astype(F32)
    out.append(x.astype(BF16))
    return out


def _cparams(sem):
    return pltpu.CompilerParams(dimension_semantics=sem, vmem_limit_bytes=VMEM_LIMIT)


def _inproj_kernel(x_ref, nw_ref, w_ref, o_ref, u_ref):
    @pl.when(pl.program_id(1) == 0)
    def _():
        x = x_ref[...]
        ms = jnp.mean(x * x, axis=-1, keepdims=True)
        u_ref[...] = ((x * lax.rsqrt(ms + NORM_EPS)) * nw_ref[...]).astype(BF16)

    o_ref[...] = jnp.dot(u_ref[...], w_ref[...], preferred_element_type=F32)


def _in_proj(x2, norm_w, w_perm):
    t = x2.shape[0]
    tm, tn = 1024, 1792
    return pl.pallas_call(
        _inproj_kernel,
        grid=(t // tm, NP // tn),
        in_specs=[
            pl.BlockSpec((tm, D_MODEL), lambda i, j: (i, 0)),
            pl.BlockSpec((1, D_MODEL), lambda i, j: (0, 0)),
            pl.BlockSpec((D_MODEL, tn), lambda i, j: (0, j)),
        ],
        out_specs=pl.BlockSpec((tm, tn), lambda i, j: (i, j)),
        out_shape=jax.ShapeDtypeStruct((t, NP), F32),
        scratch_shapes=[pltpu.VMEM((tm, D_MODEL), BF16)],
        compiler_params=_cparams(("parallel", "arbitrary")),
        name="in_proj",
    )(x2, norm_w, w_perm)


def _ssd_kernel(xbc_ref, z_ref, dt_ref, cw_ref, cb_ref, dtb_ref, alog_ref, dskip_ref, nw_ref,
                ltri_ref, ex_ref, o_ref, ext_ref, state_ref, y_ref):
    L = SSD_CHUNK
    c = pl.program_id(1)

    @pl.when(c == 0)
    def _():
        ext_ref[0:8, :] = jnp.zeros((8, CONV_CH), F32)
        state_ref[...] = jnp.zeros(state_ref.shape, F32)

    ext_ref[8:8 + L, :] = xbc_ref[...]
    w = cw_ref[...]
    y = (ext_ref[5:5 + L, :] * w[0:1, :] + ext_ref[6:6 + L, :] * w[1:2, :]
         + ext_ref[7:7 + L, :] * w[2:3, :] + ext_ref[8:8 + L, :] * w[3:4, :]) + cb_ref[...]
    tail = ext_ref[L:L + 8, :]
    ext_ref[0:8, :] = tail
    act = _silu(y)
    xs = act[:, :SSD_WIDTH]
    bm = act[:, SSD_WIDTH:SSD_WIDTH + SSD_GROUPS * SSD_STATE]
    cm = act[:, SSD_WIDTH + SSD_GROUPS * SSD_STATE:]

    v = dt_ref[...] + dtb_ref[...]
    dt = jnp.maximum(v, 0.0) + jnp.log1p(jnp.exp(-jnp.abs(v)))
    a = -jnp.exp(alog_ref[...])
    adt = a * dt
    ltri = ltri_ref[...]
    acum = sum(jnp.dot(ltri, t, preferred_element_type=F32) for t in _split_bf16(adt, 3))
    acum_t = acum.T
    last = acum[L - 1:L, :]
    stacked = jnp.concatenate([dt, jnp.exp(acum), jnp.exp(last - acum)], axis=0)
    ex = ex_ref[...]
    expanded = sum(jnp.dot(t, ex, preferred_element_type=F32) for t in _split_bf16(stacked, 2))
    dt_e = expanded[0:L]
    expa_e = expanded[L:2 * L]
    dst_e = expanded[2 * L:3 * L]

    xdt = xs * dt_e
    xds_b = (xdt * dst_e).astype(BF16)
    row = lax.broadcasted_iota(jnp.int32, (L, L), 0)
    col = lax.broadcasted_iota(jnp.int32, (L, L), 1)
    causal = row >= col
    lane = lax.broadcasted_iota(jnp.int32, (L, LANES), 1)
    hg = SSD_HEADS // SSD_GROUPS
    gw = hg * SSD_HEAD_DIM
    for g in range(SSD_GROUPS):
        bg = bm[:, g * SSD_STATE:(g + 1) * SSD_STATE]
        cg_b = cm[:, g * SSD_STATE:(g + 1) * SSD_STATE].astype(BF16)
        bg_b = bg.astype(BF16)
        cb = lax.dot_general(cg_b, bg_b, (((1,), (1,)), ((), ())), preferred_element_type=F32)
        hprev = state_ref[g]
        yoff = jnp.dot(cg_b, hprev.astype(BF16), preferred_element_type=F32) * expa_e[:, g * gw:(g + 1) * gw]
        snew = jnp.dot(bg.T.astype(BF16), xds_b[:, g * gw:(g + 1) * gw], preferred_element_type=F32)
        state_ref[g] = hprev * expa_e[L - 1:L, g * gw:(g + 1) * gw] + snew
        for k in range(hg // 2):
            pair = g * (hg // 2) + k
            ms = []
            for hh in (2 * pair, 2 * pair + 1):
                seg = acum[:, hh:hh + 1] - acum_t[hh:hh + 1, :]
                decay = jnp.exp(jnp.where(causal, seg, -jnp.inf))
                ms.append((cb * decay).astype(BF16))
            mpair = jnp.concatenate(ms, axis=1)
            xp = xdt[:, pair * LANES:(pair + 1) * LANES]
            xblk = jnp.concatenate([jnp.where(lane < SSD_HEAD_DIM, xp, 0.0),
                                    jnp.where(lane >= SSD_HEAD_DIM, xp, 0.0)], axis=0).astype(BF16)
            yd = jnp.dot(mpair, xblk, preferred_element_type=F32)
            y_ref[:, pair * LANES:(pair + 1) * LANES] = yd + yoff[:, k * LANES:(k + 1) * LANES]

    yy = y_ref[...] + dskip_ref[...] * xs
    z = z_ref[...]
    yz = yy * _silu(z)
    ms2 = jnp.mean(yz * yz, axis=-1, keepdims=True)
    o_ref[...] = ((yz * lax.rsqrt(ms2 + NORM_EPS)) * nw_ref[...]).astype(o_ref.dtype)


def _ssd(proj, bsz, seq, conv_w, conv_b, dtb_pad, alog_pad, dskip_e, norm_w):
    nc = seq // SSD_CHUNK
    L = SSD_CHUNK
    ltri = jnp.asarray(np.tril(np.ones((L, L), np.float32)), BF16)
    ex = np.zeros((LANES, SSD_WIDTH), np.float32)
    for h in range(SSD_HEADS):
        ex[h, h * SSD_HEAD_DIM:(h + 1) * SSD_HEAD_DIM] = 1.0
    ex = jnp.asarray(ex, BF16)
    const = lambda shape: pl.BlockSpec(shape, lambda b, c: (0,) * len(shape))
    return pl.pallas_call(
        _ssd_kernel,
        grid=(bsz, nc),
        in_specs=[
            pl.BlockSpec((L, CONV_CH), lambda b, c: (b * nc + c, COL_XBC // CONV_CH)),
            pl.BlockSpec((L, SSD_WIDTH), lambda b, c: (b * nc + c, COL_Z // SSD_WIDTH)),
            pl.BlockSpec((L, LANES), lambda b, c: (b * nc + c, COL_DT // LANES)),
            const((CONV_WIDTH, CONV_CH)), const((1, CONV_CH)), const((1, LANES)), const((1, LANES)),
            const((1, SSD_WIDTH)), const((1, SSD_WIDTH)), const((L, L)), const((LANES, SSD_WIDTH)),
        ],
        out_specs=pl.BlockSpec((L, SSD_WIDTH), lambda b, c: (b * nc + c, 0)),
        out_shape=jax.ShapeDtypeStruct((bsz * seq, SSD_WIDTH), BF16),
        scratch_shapes=[
            pltpu.VMEM((L + 8, CONV_CH), F32),
            pltpu.VMEM((SSD_GROUPS, SSD_STATE, (SSD_HEADS // SSD_GROUPS) * SSD_HEAD_DIM), F32),
            pltpu.VMEM((L, SSD_WIDTH), F32),
        ],
        compiler_params=_cparams(("parallel", "arbitrary")),
        name="ssd",
    )(proj, proj, proj, conv_w, conv_b, dtb_pad, alog_pad, dskip_e, norm_w, ltri, ex)


def _rope(x, cos, s1, s2):
    n = x.shape[-1]
    half = ROPE_DIM // 2
    return x * cos + pltpu.roll(x, n - half, 1) * s1 + pltpu.roll(x, half, 1) * s2


def _merge_heads(parts, upper=False):
    lane = lax.broadcasted_iota(jnp.int32, parts[0].shape, 1)
    out = []
    for k in range(len(parts) // 2):
        a, b = parts[2 * k], parts[2 * k + 1]
        if upper:
            out.append(jnp.where(lane < HEAD_DIM, pltpu.roll(a, HEAD_DIM, 1), b))
        else:
            out.append(jnp.where(lane < HEAD_DIM, a, pltpu.roll(b, HEAD_DIM, 1)))
    return jnp.concatenate(out, axis=1)


def _kvprep_kernel(ks_ref, vs_ref, kw_ref, vw_ref, cos_ref, s1_ref, s2_ref, sel_ref,
                   kso_ref, vso_ref, kwo_ref, vwo_ref, *, tiles_per_seq):
    tr = ks_ref.shape[0]
    wide = KV_HEADS * LANES
    s0 = (pl.program_id(0) % tiles_per_seq) * tr
    cos, s1, s2 = cos_ref[...], s1_ref[...], s2_ref[...]
    sel = sel_ref[...]
    lane = lax.broadcasted_iota(jnp.int32, (tr, wide), 1) % LANES
    blk = (s0 + lax.broadcasted_iota(jnp.int32, (tr, wide), 0)) // SEL_BLOCK
    ebias = jnp.where(lane - SEL_BLOCK == blk, BLOCK_BIAS, 0.0)
    ones = jnp.where(lane >= HEAD_DIM, 1.0, 0.0)

    def place(x):
        return jnp.dot(x.astype(BF16), sel, preferred_element_type=F32)

    kso_ref[...] = (place(_rope(ks_ref[...], cos, s1, s2)) + ebias).astype(BF16)
    kwo_ref[...] = place(_rope(kw_ref[...], cos, s1, s2)).astype(BF16)
    vso_ref[...] = (place(vs_ref[...]) + ones).astype(BF16)
    vwo_ref[...] = (place(vw_ref[...]) + ones).astype(BF16)


def _head_place_matrix():
    m = np.zeros((KV_WIDTH, KV_HEADS * LANES), np.float32)
    for h in range(KV_HEADS):
        for d in range(HEAD_DIM):
            m[h * HEAD_DIM + d, h * LANES + d] = 1.0
    return m


def _kv_prep(proj, seq, cos_t, s1_t, s2_t):
    t = proj.shape[0]
    tr = 512
    tps = seq // tr
    sel = jnp.asarray(_head_place_matrix(), BF16)
    kvb = COL_KV // KV_WIDTH
    seg = lambda k: pl.BlockSpec((tr, KV_WIDTH), lambda i: (i, kvb + k))
    tab = pl.BlockSpec((tr, KV_WIDTH), lambda i: (i % tps, 0))
    wide = KV_HEADS * LANES
    out = pl.BlockSpec((tr, wide), lambda i: (i, 0))
    shp = jax.ShapeDtypeStruct((t, wide), BF16)
    return pl.pallas_call(
        functools.partial(_kvprep_kernel, tiles_per_seq=tps),
        grid=(t // tr,),
        in_specs=[seg(2), seg(3), seg(4), seg(5), tab, tab, tab,
                  pl.BlockSpec((KV_WIDTH, wide), lambda i: (0, 0))],
        out_specs=[out, out, out, out],
        out_shape=[shp, shp, shp, shp],
        compiler_params=_cparams(("parallel",)),
        name="kv_prep",
    )(proj, proj, proj, proj, cos_t, s1_t, s2_t, sel)


def _compress_kernel(gk_ref, gv_ref, pek_ref, pev_ref, w1k_ref, w1v_ref, w2kt_ref, w2v_ref,
                     kblk_ref, vblk_ref, shift_ref):
    ng = gk_ref.shape[0] // CMP_STRIDE
    heads_per_block = LANES // HEAD_DIM
    mine = (lax.broadcasted_iota(jnp.int32, (ng, LANES), 1) // HEAD_DIM) == pl.program_id(1) % heads_per_block

    def hidden(x_ref, pe_ref, w1_ref):
        top = jnp.zeros((ng, CMP_HIDDEN), F32)
        bot = jnp.zeros((ng, CMP_HIDDEN), F32)
        for r in range(CMP_STRIDE):
            x = x_ref[pl.ds(r, ng, stride=CMP_STRIDE), :]
            xt = jnp.where(mine, x + pe_ref[r:r + 1, :], 0.0).astype(BF16)
            xb = jnp.where(mine, x + pe_ref[CMP_STRIDE + r:CMP_STRIDE + r + 1, :], 0.0).astype(BF16)
            top = top + jnp.dot(xt, w1_ref[r], preferred_element_type=F32)
            bot = bot + jnp.dot(xb, w1_ref[CMP_STRIDE + r], preferred_element_type=F32)
        shift_ref[0:ng, :] = bot
        shift_ref[ng:ng + 8, :] = jnp.zeros((8, CMP_HIDDEN), F32)
        return _silu(top + shift_ref[1:ng + 1, :]).astype(BF16)

    hk = hidden(gk_ref, pek_ref, w1k_ref)
    kt = lax.dot_general(w2kt_ref[...], hk, (((1,), (1,)), ((), ())), preferred_element_type=F32)
    kblk_ref[...] = jnp.zeros(kblk_ref.shape, BF16)
    for g in range(ATT_GROUP):
        kblk_ref[0, 0, g * HEAD_DIM:(g + 1) * HEAD_DIM, g * ng:(g + 1) * ng] = kt.astype(BF16)

    hv = hidden(gv_ref, pev_ref, w1v_ref)
    vc = jnp.dot(hv, w2v_ref[...], preferred_element_type=F32)
    vblk_ref[...] = jnp.zeros(vblk_ref.shape, BF16)
    for g in range(ATT_GROUP):
        vblk_ref[0, 0, g * ng:(g + 1) * ng, g * LANES:(g + 1) * LANES] = vc.astype(BF16)


def _compress(proj, bsz, seq, pek, pev, w1k, w1v, w2kt, w2v):
    ng = seq // CMP_STRIDE
    hpb = LANES // HEAD_DIM
    kcb = COL_KV // LANES
    vcb = kcb + KV_WIDTH // LANES
    const = lambda shape: pl.BlockSpec(shape, lambda b, h: (0,) * len(shape))
    return pl.pallas_call(
        _compress_kernel,
        grid=(bsz, KV_HEADS),
        in_specs=[pl.BlockSpec((seq, LANES), lambda b, h: (b, kcb + h // hpb)),
                  pl.BlockSpec((seq, LANES), lambda b, h: (b, vcb + h // hpb)),
                  const((CMP_BLOCK, LANES)), const((CMP_BLOCK, LANES)),
                  const((CMP_BLOCK, LANES, CMP_HIDDEN)), const((CMP_BLOCK, LANES, CMP_HIDDEN)),
                  const((HEAD_DIM, CMP_HIDDEN)), const((CMP_HIDDEN, LANES))],
        out_specs=[pl.BlockSpec((1, 1, KV_WIDTH, ATT_GROUP * ng), lambda b, h: (b, h, 0, 0)),
                   pl.BlockSpec((1, 1, ATT_GROUP * ng, ATT_GROUP * LANES), lambda b, h: (b, h, 0, 0))],
        out_shape=[jax.ShapeDtypeStruct((bsz, KV_HEADS, KV_WIDTH, ATT_GROUP * ng), BF16),
                   jax.ShapeDtypeStruct((bsz, KV_HEADS, ATT_GROUP * ng, ATT_GROUP * LANES), BF16)],
        scratch_shapes=[pltpu.VMEM((ng + 8, CMP_HIDDEN), F32)],
        compiler_params=_cparams(("parallel", "parallel")),
        name="compress",
    )(proj, proj, pek, pev, w1k, w1v, w2kt, w2v)


TQC = TQ
SUBLANES = 8


CMP_HEADS_PER_STEP = 4


def _nsacmp_kernel(q_ref, kblk_ref, vblk_ref, cos_ref, s1_ref, s2_ref, ovt_ref, selq_ref, placen_ref, eye_ref,
                   qaug_ref, ocmp_ref):
    for hh in range(CMP_HEADS_PER_STEP):
        lanes = slice(hh * KV_WIDTH, (hh + 1) * KV_WIDTH)
        _nsacmp_head(q_ref.at[:, lanes], kblk_ref.at[0, hh], vblk_ref.at[0, hh], cos_ref, s1_ref, s2_ref,
                     ovt_ref, selq_ref, placen_ref, eye_ref, qaug_ref.at[0, hh], ocmp_ref.at[:, lanes])


def _nsacmp_head(q_ref, kblk_ref, vblk_ref, cos_ref, s1_ref, s2_ref, ovt_ref, selq_ref, placen_ref, eye_ref,
                 qaug_ref, ocmp_ref):
    ncp = kblk_ref.shape[1] // ATT_GROUP
    nsel = ovt_ref.shape[0]
    nt = (((1,), (1,)), ((), ()))
    s0 = pl.program_id(2) * TQC
    q = q_ref[...]
    s_all = jnp.dot((q * SCALE).astype(BF16), kblk_ref[...], preferred_element_type=F32)
    tq_pos = s0 + lax.broadcasted_iota(jnp.int32, (TQC, ncp), 0)
    cmp_end = lax.broadcasted_iota(jnp.int32, (TQC, ncp), 1) * CMP_STRIDE + (CMP_BLOCK - 1)
    mask = cmp_end <= tq_pos
    any_visible = (s0 + lax.broadcasted_iota(jnp.int32, (TQC, 1), 0) >= CMP_BLOCK - 1).astype(F32)
    ps = []
    for g in range(ATT_GROUP):
        s = jnp.where(mask, s_all[:, g * ncp:(g + 1) * ncp], NEG_INF)
        m = jnp.max(s, axis=-1, keepdims=True)
        e = jnp.exp(s - m)
        ps.append(e * (any_visible / jnp.sum(e, axis=-1, keepdims=True)))
    p_all = jnp.concatenate(ps, axis=1)

    def stack_rows(wide):
        return jnp.concatenate([wide[u * TQ:(u + 1) * TQ, g * LANES:(g + 1) * LANES]
                                for u in range(TQC // TQ) for g in range(ATT_GROUP)], axis=0)

    ow = jnp.dot(p_all.astype(BF16), vblk_ref[...], preferred_element_type=F32)
    ocmp_ref[...] = _merge_heads([ow[:, g * LANES:(g + 1) * LANES] for g in range(ATT_GROUP)])

    psum = (ps[0] + ps[1]) + (ps[2] + ps[3])
    p_hi = psum.astype(BF16)
    p_lo = (psum - p_hi.astype(F32)).astype(BF16)
    ovt = ovt_ref[...]
    imp = (lax.dot_general(ovt, p_hi, nt, preferred_element_type=F32)
           + lax.dot_general(ovt, p_lo, nt, preferred_element_type=F32))
    j = lax.broadcasted_iota(jnp.int32, (nsel, TQC), 0)
    cur = (s0 + lax.broadcasted_iota(jnp.int32, (nsel, TQC), 1)) // SEL_BLOCK
    imp = jnp.where((j == 0) | (j == cur) | (j == cur - 1), FORCE_SCORE, imp)
    imp = jnp.where(j <= cur, imp, -1.0)
    nblk = nsel // SUBLANES
    blocks = [imp[k * SUBLANES:(k + 1) * SUBLANES, :] for k in range(nblk)]
    ranks = [jnp.zeros((SUBLANES, TQC), F32) for _ in range(nblk)]
    sub = lax.broadcasted_iota(jnp.int32, (SUBLANES, TQC), 0)
    for jp in range(nsel):
        r = jnp.broadcast_to(imp[jp:jp + 1, :], (SUBLANES, TQC))
        for k in range(nblk):
            if k > jp // SUBLANES:
                inc = jnp.where(r >= blocks[k], 1.0, 0.0)
            elif k < jp // SUBLANES:
                inc = jnp.where(r > blocks[k], 1.0, 0.0)
            else:
                tie = jnp.where(sub > jp % SUBLANES, 1.0, 0.0)
                inc = jnp.where(r > blocks[k], 1.0, jnp.where(r == blocks[k], tie, 0.0))
            ranks[k] = ranks[k] + inc
    rank = jnp.concatenate(ranks, axis=0)
    notsel_t = jnp.where((rank < float(N_SELECT)) & (imp >= 0.0), 0.0, 1.0).astype(BF16)
    notsel = lax.dot_general(eye_ref[...], notsel_t, nt, preferred_element_type=F32)

    q_rot = (_rope(q, cos_ref[...], s1_ref[...], s2_ref[...]) * SCALE).astype(BF16)
    qw = (jnp.dot(q_rot, selq_ref[...], preferred_element_type=F32)
          + jnp.dot(notsel.astype(BF16), placen_ref[...], preferred_element_type=F32))
    qaug_ref[...] = stack_rows(qw).astype(BF16)


def _selection_overlap_t(n_cmp_pad, n_cmp, n_sel):
    cs = np.arange(n_cmp)[:, None] * CMP_STRIDE
    ce = cs + CMP_BLOCK
    ss = np.arange(n_sel)[None, :] * SEL_BLOCK
    se = ss + SEL_BLOCK
    ov = np.clip(np.minimum(ce, se) - np.maximum(cs, ss), 0, None) / CMP_BLOCK
    full = np.zeros((n_cmp_pad, n_sel), np.float32)
    full[:n_cmp] = ov
    return full.T.copy()


def _nsa_cmp(proj, kblk, vblk, bsz, seq, cos_t, s1_t, s2_t):
    nqc = seq // TQC
    ncp = kblk.shape[3] // ATT_GROUP
    n_cmp = (seq - CMP_BLOCK) // CMP_STRIDE + 1
    nsel = seq // SEL_BLOCK
    ovt = jnp.asarray(_selection_overlap_t(ncp, n_cmp, nsel), BF16)
    selq = jnp.asarray(_head_place_matrix(), BF16)
    placen = np.zeros((nsel, ATT_GROUP * LANES), np.float32)
    for g in range(ATT_GROUP):
        for jb in range(nsel):
            placen[jb, g * LANES + HEAD_DIM + jb] = 1.0
    placen = jnp.asarray(placen, BF16)
    eye = jnp.asarray(np.eye(TQC, dtype=np.float32), BF16)
    const = lambda shape: pl.BlockSpec(shape, lambda b, h, t: (0,) * len(shape))
    tab = pl.BlockSpec((TQC, KV_WIDTH), lambda b, h, t: (t, 0))
    hps = CMP_HEADS_PER_STEP
    qb = COL_Q // (hps * KV_WIDTH)
    rows = (TQC // TQ) * ROWS
    stacked = pl.BlockSpec((1, hps, rows, LANES), lambda b, h, t: (b, h, t, 0))
    return pl.pallas_call(
        _nsacmp_kernel,
        grid=(bsz, KV_HEADS // hps, nqc),
        in_specs=[
            pl.BlockSpec((TQC, hps * KV_WIDTH), lambda b, h, t: (b * nqc + t, qb + h)),
            pl.BlockSpec((1, hps, KV_WIDTH, ATT_GROUP * ncp), lambda b, h, t: (b, h, 0, 0)),
            pl.BlockSpec((1, hps, ATT_GROUP * ncp, ATT_GROUP * LANES), lambda b, h, t: (b, h, 0, 0)),
            tab, tab, tab,
            const((nsel, ncp)), const((KV_WIDTH, ATT_GROUP * LANES)),
            const((nsel, ATT_GROUP * LANES)), const((TQC, TQC)),
        ],
        out_specs=[stacked, pl.BlockSpec((TQC, hps * KV_WIDTH), lambda b, h, t: (b * nqc + t, h))],
        out_shape=[jax.ShapeDtypeStruct((bsz, KV_HEADS, nqc * rows, LANES), BF16),
                   jax.ShapeDtypeStruct((bsz * seq, ATT_WIDTH), F32)],
        compiler_params=_cparams(("parallel", "parallel", "parallel")),
        name="nsa_cmp",
    )(proj, kblk, vblk, cos_t, s1_t, s2_t, ovt, selq, placen, eye)


ATT_HEADS_PER_STEP = 2


def _nsaattn_kernel(qaug_ref, ocmp_ref, ks_ref, vs_ref, kw_ref, vw_ref, gate_ref, gexp_ref, db_ref, wb_ref, o_ref,
                    m_ref, acc_ref):
    qt = pl.program_id(2)
    s0 = qt * TQ
    nt = (((1,), (1,)), ((), ()))
    heads = range(ATT_HEADS_PER_STEP)
    qs = [qaug_ref[0, hh] for hh in heads]
    klanes = [slice(hh * LANES, (hh + 1) * LANES) for hh in heads]
    olanes = [slice(hh * KV_WIDTH, (hh + 1) * KV_WIDTH) for hh in heads]

    def per_head(bias):
        return jnp.concatenate([bias] * ATT_GROUP, axis=0)

    m_ref[...] = jnp.full(m_ref.shape, NEG_INF, F32)
    acc_ref[...] = jnp.zeros(acc_ref.shape, F32)

    def tile(first_key, width, causal=False):
        start = pl.multiple_of(first_key, width)
        for hh in heads:
            k = ks_ref[pl.ds(start, width), klanes[hh]]
            v = vs_ref[pl.ds(start, width), klanes[hh]]
            s = lax.dot_general(qs[hh], k, nt, preferred_element_type=F32)
            if causal:
                s = s + per_head(db_ref[0][:, :width])
            m_prev = m_ref[hh]
            m_next = jnp.maximum(m_prev, jnp.max(s, axis=-1, keepdims=True))
            p = jnp.exp(s - jnp.concatenate([m_next] * (width // LANES), axis=1))
            acc_ref[hh] = acc_ref[hh] * jnp.exp(m_prev - m_next) + jnp.dot(
                p.astype(BF16), v, preferred_element_type=F32)
            m_ref[hh] = m_next

    nfull = (qt * TQ) // TK

    def body(i, carry):
        tile(i * (2 * TK), 2 * TK)
        return carry

    lax.fori_loop(0, nfull // 2, body, 0)

    @pl.when(nfull % 2 == 1)
    def _():
        tile((nfull - 1) * TK, TK)

    @pl.when(s0 == nfull * TK)
    def _():
        tile(s0, TQ, causal=True)

    @pl.when(s0 != nfull * TK)
    def _():
        tile(nfull * TK, TK, causal=True)

    def normalised(x):
        parts = [x[g * TQ:(g + 1) * TQ] for g in range(ATT_GROUP)]
        return _merge_heads(parts) / _merge_heads(parts, upper=True)

    wk = WINDOW + TQ
    wstart = pl.multiple_of(jnp.maximum(s0 - WINDOW, 0), TQ)
    sig_terms = _split_bf16(jax.nn.sigmoid(gate_ref[...]), 2)
    for hh in heads:
        kwin = kw_ref[pl.ds(wstart, wk), klanes[hh]]
        vwin = vw_ref[pl.ds(wstart, wk), klanes[hh]]
        sw = lax.dot_general(qs[hh], kwin, nt, preferred_element_type=F32) + per_head(wb_ref[0])
        pw = jnp.exp(sw - jnp.max(sw, axis=-1, keepdims=True))
        ow = jnp.dot(pw.astype(BF16), vwin, preferred_element_type=F32)

        gexp = gexp_ref[hh]
        gmap = sum(jnp.dot(t, gexp, preferred_element_type=F32) for t in sig_terms)
        y = (gmap[:, 0:KV_WIDTH] * ocmp_ref[:, olanes[hh]]
             + gmap[:, KV_WIDTH:2 * KV_WIDTH] * normalised(acc_ref[hh])
             + gmap[:, 2 * KV_WIDTH:3 * KV_WIDTH] * normalised(ow))
        o_ref[:, olanes[hh]] = y.astype(o_ref.dtype)


def _nsa_attn(qaug, ocmp, ksa, vsa, kwa, vwa, proj, bsz, seq):
    nqt = seq // TQ
    gexp = np.zeros((KV_HEADS, LANES, N_BRANCH * KV_WIDTH), np.float32)
    for h in range(KV_HEADS):
        for g in range(ATT_GROUP):
            for br in range(N_BRANCH):
                src = (h * ATT_GROUP + g) * N_BRANCH + br
                gexp[h, src, br * KV_WIDTH + g * HEAD_DIM:br * KV_WIDTH + (g + 1) * HEAD_DIM] = 1.0
    gexp = jnp.asarray(gexp, BF16)
    r = np.arange(TQ)[:, None]
    diag_cases = TK // TQ
    db = np.stack([np.where(np.arange(TK)[None, :] <= c * TQ + r, 0.0, NEG_INF) for c in range(diag_cases)])
    win_cases = WINDOW // TQ + 1
    wk = WINDOW + TQ
    dist = lambda c: c * TQ + r - np.arange(wk)[None, :]
    wb = np.stack([np.where((dist(c) >= 0) & (dist(c) < WINDOW), 0.0, NEG_INF) for c in range(win_cases)])
    db = jnp.asarray(db, F32)
    wb = jnp.asarray(wb, F32)
    hps = ATT_HEADS_PER_STEP
    stacked = pl.BlockSpec((1, hps, ROWS, LANES), lambda b, h, t: (b, h, t, 0))
    kv = pl.BlockSpec((seq, hps * LANES), lambda b, h, t: (b, h))
    merged = pl.BlockSpec((TQ, hps * KV_WIDTH), lambda b, h, t: (b * nqt + t, h))
    return pl.pallas_call(
        _nsaattn_kernel,
        grid=(bsz, KV_HEADS // hps, nqt),
        in_specs=[stacked, merged, kv, kv, kv, kv,
                  pl.BlockSpec((TQ, LANES), lambda b, h, t: (b * nqt + t, COL_GATE // LANES)),
                  pl.BlockSpec((hps, LANES, N_BRANCH * KV_WIDTH), lambda b, h, t: (h, 0, 0)),
                  pl.BlockSpec((1, TQ, TK), lambda b, h, t: (t % diag_cases, 0, 0)),
                  pl.BlockSpec((1, TQ, wk), lambda b, h, t: (jnp.minimum(t, win_cases - 1), 0, 0))],
        out_specs=merged,
        out_shape=jax.ShapeDtypeStruct((bsz * seq, ATT_WIDTH), BF16),
        scratch_shapes=[pltpu.VMEM((hps, ROWS, LANES), F32), pltpu.VMEM((hps, ROWS, LANES), F32)],
        compiler_params=_cparams(("parallel", "parallel", "arbitrary")),
        name="nsa_attn",
    )(qaug, ocmp, ksa, vsa, kwa, vwa, proj, gexp, db, wb)


def _outproj_kernel(x_ref, ys_ref, ya_ref, w1_ref, w2_ref, o_ref):
    o_ref[...] = (x_ref[...] + jnp.dot(ys_ref[...], w1_ref[...], preferred_element_type=F32)
                  + jnp.dot(ya_ref[...], w2_ref[...], preferred_element_type=F32))


def _out_proj(x2, ys, ya, wo_b):
    t = x2.shape[0]
    tm = 512
    return pl.pallas_call(
        _outproj_kernel,
        grid=(t // tm,),
        in_specs=[
            pl.BlockSpec((tm, D_MODEL), lambda i: (i, 0)),
            pl.BlockSpec((tm, SSD_WIDTH), lambda i: (i, 0)),
            pl.BlockSpec((tm, ATT_WIDTH), lambda i: (i, 0)),
            pl.BlockSpec((SSD_WIDTH, D_MODEL), lambda i: (0, 0)),
            pl.BlockSpec((ATT_WIDTH, D_MODEL), lambda i: (1, 0)),
        ],
        out_specs=pl.BlockSpec((tm, D_MODEL), lambda i: (i, 0)),
        out_shape=jax.ShapeDtypeStruct((t, D_MODEL), F32),
        compiler_params=_cparams(("parallel",)),
        name="out_proj",
    )(x2, ys, ya, wo_b, wo_b)


def _ffn_kernel(h_ref, nw_ref, fw_ref, wg_ref, wu_ref, wd_ref, o_ref, v_ref):
    j = pl.program_id(1)

    @pl.when(j == 0)
    def _():
        h = h_ref[...]
        ms = jnp.mean(h * h, axis=-1, keepdims=True)
        v_ref[...] = ((h * lax.rsqrt(ms + NORM_EPS)) * nw_ref[...]).astype(BF16)
        o_ref[...] = h

    v = v_ref[...]
    gate = jnp.dot(v, wg_ref[...].astype(BF16), preferred_element_type=F32)
    up = jnp.dot(v, wu_ref[...].astype(BF16), preferred_element_type=F32)
    o_ref[...] += jnp.dot((_silu(gate) * up).astype(BF16), wd_ref[...].astype(BF16), preferred_element_type=F32)

    @pl.when(j == pl.num_programs(1) - 1)
    def _():
        h2 = o_ref[...]
        ms = jnp.mean(h2 * h2, axis=-1, keepdims=True)
        o_ref[...] = (h2 * lax.rsqrt(ms + NORM_EPS)) * fw_ref[...]


def _ffn(h1, ffn_nw, final_w, wg_b, wu_b, wd_b):
    t = h1.shape[0]
    tm, tf = 1024, 256
    return pl.pallas_call(
        _ffn_kernel,
        grid=(t // tm, D_FF // tf),
        in_specs=[
            pl.BlockSpec((tm, D_MODEL), lambda i, j: (i, 0)),
            pl.BlockSpec((1, D_MODEL), lambda i, j: (0, 0)),
            pl.BlockSpec((1, D_MODEL), lambda i, j: (0, 0)),
            pl.BlockSpec((D_MODEL, tf), lambda i, j: (0, j)),
            pl.BlockSpec((D_MODEL, tf), lambda i, j: (0, j)),
            pl.BlockSpec((tf, D_MODEL), lambda i, j: (j, 0)),
        ],
        out_specs=pl.BlockSpec((tm, D_MODEL), lambda i, j: (i, 0)),
        out_shape=jax.ShapeDtypeStruct((t, D_MODEL), F32),
        scratch_shapes=[pltpu.VMEM((tm, D_MODEL), BF16)],
        compiler_params=_cparams(("parallel", "arbitrary")),
        name="ffn",
    )(h1, ffn_nw, final_w, wg_b, wu_b, wd_b)


def _rope_tables(seq):
    f32 = np.float32
    inv = (f32(1.0) / (f32(ROPE_THETA) ** (np.arange(0, ROPE_DIM, 2, dtype=f32) / f32(ROPE_DIM)))).astype(f32)
    ang = (np.arange(seq, dtype=f32)[:, None] * inv[None, :]).astype(f32)
    cos, sin = np.cos(ang).astype(f32), np.sin(ang).astype(f32)
    half = ROPE_DIM // 2
    rest_one = np.ones((seq, HEAD_DIM - ROPE_DIM), f32)
    rest_zero = np.zeros((seq, HEAD_DIM - ROPE_DIM), f32)
    zero_h = np.zeros((seq, half), f32)
    cos_h = np.concatenate([cos, cos, rest_one], axis=1)
    s1_h = np.concatenate([-sin, zero_h, rest_zero], axis=1)
    s2_h = np.concatenate([zero_h, sin, rest_zero], axis=1)
    tile = lambda a: jnp.asarray(np.tile(a, (1, KV_HEADS)))
    return tile(cos_h), tile(s1_h), tile(s2_h)


def _pad_lanes(a, width):
    return jnp.pad(a, ((0, 0), (0, width - a.shape[1])))


def _layer(h2d, bsz, seq, p):
    (attn_norm_w, w_in, conv_w, conv_b, dt_bias, a_log, d_skip, ssd_norm_w, cmp_w1_k, cmp_w2_k, cmp_w1_v,
     cmp_w2_v, cmp_pe_k, cmp_pe_v, w_out, ffn_norm_w, w_gate, w_up, w_down) = p
    o_xbc, o_dt, o_q, o_kv, o_gate = 1024, 2560, 2576, 3600, 5136
    w_perm = jnp.concatenate([
        w_in[:, o_xbc:o_dt], w_in[:, o_kv:o_gate], w_in[:, :o_xbc], w_in[:, o_q:o_kv],
        _pad_lanes(w_in[:, o_dt:o_q], LANES), _pad_lanes(w_in[:, o_gate:], LANES)], axis=1).astype(BF16)
    proj = _in_proj(h2d, attn_norm_w[None, :], w_perm)

    y_ssd = _ssd(proj, bsz, seq, conv_w, conv_b[None, :], _pad_lanes(dt_bias[None, :], LANES),
                 _pad_lanes(a_log[None, :], LANES), jnp.repeat(d_skip, SSD_HEAD_DIM)[None, :],
                 ssd_norm_w[None, :])

    cos_t, s1_t, s2_t = _rope_tables(seq)
    ksa, vsa, kwa, vwa = _kv_prep(proj, seq, cos_t, s1_t, s2_t)

    hpb = LANES // HEAD_DIM
    pe_rep = lambda pe: jnp.tile(pe, (1, hpb))
    w1_rep = lambda w1: jnp.tile(w1.astype(BF16).reshape(CMP_BLOCK, HEAD_DIM, CMP_HIDDEN), (1, hpb, 1))
    kblk, vblk = _compress(proj, bsz, seq, pe_rep(cmp_pe_k), pe_rep(cmp_pe_v),
                           w1_rep(cmp_w1_k), w1_rep(cmp_w1_v), cmp_w2_k.T.astype(BF16),
                           _pad_lanes(cmp_w2_v, LANES).astype(BF16))
    qaug, ocmp = _nsa_cmp(proj, kblk, vblk, bsz, seq, cos_t, s1_t, s2_t)
    y_att = _nsa_attn(qaug, ocmp, ksa, vsa, kwa, vwa, proj, bsz, seq)

    h1 = _out_proj(h2d, y_ssd, y_att, w_out.astype(BF16))
    return h1, (ffn_norm_w, w_gate, w_up, w_down)


def kernel(x, attn_norm_w, w_in, conv_w, conv_b, dt_bias, a_log, d_skip, ssd_norm_w, cmp_w1_k, cmp_w2_k,
           cmp_w1_v, cmp_w2_v, cmp_pe_k, cmp_pe_v, w_out, ffn_norm_w, w_gate, w_up, w_down, final_norm_w):
    bsz, seq, _ = x.shape
    depth = w_in.shape[0]
    assert depth == 1, "the final rmsnorm is fused into the last layer's ffn kernel"
    h = x.reshape(bsz * seq, D_MODEL)
    l = 0
    params = (attn_norm_w[l], w_in[l], conv_w[l], conv_b[l], dt_bias[l], a_log[l], d_skip[l], ssd_norm_w[l],
              cmp_w1_k[l], cmp_w2_k[l], cmp_w1_v[l], cmp_w2_v[l], cmp_pe_k[l], cmp_pe_v[l], w_out[l],
              ffn_norm_w[l], w_gate[l], w_up[l], w_down[l])
    h1, (fnw, wg_b, wu_b, wd_b) = _layer(h, bsz, seq, params)
    out = _ffn(h1, fnw[None, :], final_norm_w[None, :], wg_b, wu_b, wd_b)
    return out.reshape(bsz, seq, D_MODEL)
```

```python
import functools

import numpy as np
import jax
import jax.numpy as jnp
from jax import lax
from jax.experimental import pallas as pl
from jax.experimental.pallas import tpu as pltpu

F32 = jnp.float32
BF16 = jnp.bfloat16

D_MODEL = 2048
SSD_WIDTH = 1024
ATT_WIDTH = 1024
SSD_HEAD_DIM = 64
SSD_HEADS = 16
SSD_GROUPS = 2
SSD_STATE = 128
SSD_CHUNK = 128
CONV_WIDTH = 4
CONV_CH = SSD_WIDTH + 2 * SSD_GROUPS * SSD_STATE
HEAD_DIM = 64
ATT_HEADS = 16
KV_HEADS = 4
ATT_GROUP = 4
KV_WIDTH = KV_HEADS * HEAD_DIM
CMP_BLOCK = 32
CMP_STRIDE = 16
CMP_HIDDEN = 256
SEL_BLOCK = 64
N_SELECT = 16
WINDOW = 512
N_BRANCH = 3
ROPE_THETA = 500000.0
ROPE_DIM = 16
D_FF = 5632
NORM_EPS = 1e-6
NEG_INF = -1e30
FORCE_SCORE = 1e4
SCALE = HEAD_DIM ** -0.5
BLOCK_BIAS = -(2.0 ** 100)

LANES = 128
VMEM_LIMIT = 56 * 1024 * 1024

NP = 5376
COL_XBC = 0
COL_KV = 1536
COL_Z = 3072
COL_Q = 4096
COL_DT = 5120
COL_GATE = 5248

TQ = 256
TK = 512
ROWS = ATT_GROUP * TQ


def _silu(x):
    return x * jax.nn.sigmoid(x)


def _split_bf16(x, terms):
    out = []
    for _ in range(terms - 1):
        t = x.astype(BF16)
        out.append(t)
        x = x - t.astype(F32)
    out.append(x.astype(BF16))
    return out


def _cparams(sem):
    return pltpu.CompilerParams(dimension_semantics=sem, vmem_limit_bytes=VMEM_LIMIT)


def _inproj_kernel(x_ref, nw_ref, w_ref, o_ref, u_ref):
    @pl.when(pl.program_id(1) == 0)
    def _():
        x = x_ref[...]
        ms = jnp.mean(x * x, axis=-1, keepdims=True)
        u_ref[...] = ((x * lax.rsqrt(ms + NORM_EPS)) * nw_ref[...]).astype(BF16)

    o_ref[...] = jnp.dot(u_ref[...], w_ref[...], preferred_element_type=F32)


def _in_proj(x2, norm_w, w_perm):
    t = x2.shape[0]
    tm, tn = 1024, 1792
    return pl.pallas_call(
        _inproj_kernel,
        grid=(t // tm, NP // tn),
        in_specs=[
            pl.BlockSpec((tm, D_MODEL), lambda i, j: (i, 0)),
            pl.BlockSpec((1, D_MODEL), lambda i, j: (0, 0)),
            pl.BlockSpec((D_MODEL, tn), lambda i, j: (0, j)),
        ],
        out_specs=pl.BlockSpec((tm, tn), lambda i, j: (i, j)),
        out_shape=jax.ShapeDtypeStruct((t, NP), F32),
        scratch_shapes=[pltpu.VMEM((tm, D_MODEL), BF16)],
        compiler_params=_cparams(("parallel", "arbitrary")),
        name="in_proj",
    )(x2, norm_w, w_perm)


SSD_CHUNKS_PER_STEP = 2


def _ssd_kernel(xbc_ref, z_ref, dt_ref, cw_ref, cb_ref, dtb_ref, alog_ref, dskip_ref, nw_ref,
                ltri_ref, ex_ref, o_ref, ext_ref, state_ref, y_ref):
    @pl.when(pl.program_id(1) == 0)
    def _():
        ext_ref[0:8, :] = jnp.zeros((8, CONV_CH), F32)
        state_ref[...] = jnp.zeros(state_ref.shape, F32)

    for u in range(SSD_CHUNKS_PER_STEP):
        rows = slice(u * SSD_CHUNK, (u + 1) * SSD_CHUNK)
        _ssd_chunk(xbc_ref.at[rows, :], z_ref.at[rows, :], dt_ref.at[rows, :], cw_ref, cb_ref, dtb_ref, alog_ref,
                   dskip_ref, nw_ref, ltri_ref, ex_ref, o_ref.at[rows, :], ext_ref, state_ref, y_ref.at[u])


def _ssd_chunk(xbc_ref, z_ref, dt_ref, cw_ref, cb_ref, dtb_ref, alog_ref, dskip_ref, nw_ref,
               ltri_ref, ex_ref, o_ref, ext_ref, state_ref, y_ref):
    L = SSD_CHUNK
    ext_ref[8:8 + L, :] = xbc_ref[...]
    w = cw_ref[...]
    y = (ext_ref[5:5 + L, :] * w[0:1, :] + ext_ref[6:6 + L, :] * w[1:2, :]
         + ext_ref[7:7 + L, :] * w[2:3, :] + ext_ref[8:8 + L, :] * w[3:4, :]) + cb_ref[...]
    tail = ext_ref[L:L + 8, :]
    ext_ref[0:8, :] = tail
    act = _silu(y)
    xs = act[:, :SSD_WIDTH]
    bm = act[:, SSD_WIDTH:SSD_WIDTH + SSD_GROUPS * SSD_STATE]
    cm = act[:, SSD_WIDTH + SSD_GROUPS * SSD_STATE:]

    v = dt_ref[...] + dtb_ref[...]
    dt = jnp.maximum(v, 0.0) + jnp.log1p(jnp.exp(-jnp.abs(v)))
    a = -jnp.exp(alog_ref[...])
    adt = a * dt
    ltri = ltri_ref[...]
    acum = sum(jnp.dot(ltri, t, preferred_element_type=F32) for t in _split_bf16(adt, 3))
    acum_t = acum.T
    last = acum[L - 1:L, :]
    stacked = jnp.concatenate([dt, jnp.exp(acum), jnp.exp(last - acum)], axis=0)
    ex = ex_ref[...]
    expanded = sum(jnp.dot(t, ex, preferred_element_type=F32) for t in _split_bf16(stacked, 2))
    dt_e = expanded[0:L]
    expa_e = expanded[L:2 * L]
    dst_e = expanded[2 * L:3 * L]

    xdt = xs * dt_e
    xds_b = (xdt * dst_e).astype(BF16)
    row = lax.broadcasted_iota(jnp.int32, (L, L), 0)
    col = lax.broadcasted_iota(jnp.int32, (L, L), 1)
    causal = row >= col
    lane = lax.broadcasted_iota(jnp.int32, (L, LANES), 1)
    hg = SSD_HEADS // SSD_GROUPS
    gw = hg * SSD_HEAD_DIM
    for g in range(SSD_GROUPS):
        bg = bm[:, g * SSD_STATE:(g + 1) * SSD_STATE]
        cg_b = cm[:, g * SSD_STATE:(g + 1) * SSD_STATE].astype(BF16)
        bg_b = bg.astype(BF16)
        cb = lax.dot_general(cg_b, bg_b, (((1,), (1,)), ((), ())), preferred_element_type=F32)
        hprev = state_ref[g]
        yoff = jnp.dot(cg_b, hprev.astype(BF16), preferred_element_type=F32) * expa_e[:, g * gw:(g + 1) * gw]
        snew = jnp.dot(bg.T.astype(BF16), xds_b[:, g * gw:(g + 1) * gw], preferred_element_type=F32)
        state_ref[g] = hprev * expa_e[L - 1:L, g * gw:(g + 1) * gw] + snew
        for k in range(hg // 2):
            pair = g * (hg // 2) + k
            ms = []
            for hh in (2 * pair, 2 * pair + 1):
                seg = acum[:, hh:hh + 1] - acum_t[hh:hh + 1, :]
                decay = jnp.exp(jnp.where(causal, seg, -jnp.inf))
                ms.append((cb * decay).astype(BF16))
            mpair = jnp.concatenate(ms, axis=1)
            xp = xdt[:, pair * LANES:(pair + 1) * LANES]
            xblk = jnp.concatenate([jnp.where(lane < SSD_HEAD_DIM, xp, 0.0),
                                    jnp.where(lane >= SSD_HEAD_DIM, xp, 0.0)], axis=0).astype(BF16)
            yd = jnp.dot(mpair, xblk, preferred_element_type=F32)
            y_ref[:, pair * LANES:(pair + 1) * LANES] = yd + yoff[:, k * LANES:(k + 1) * LANES]

    yy = y_ref[...] + dskip_ref[...] * xs
    z = z_ref[...]
    yz = yy * _silu(z)
    ms2 = jnp.mean(yz * yz, axis=-1, keepdims=True)
    o_ref[...] = ((yz * lax.rsqrt(ms2 + NORM_EPS)) * nw_ref[...]).astype(o_ref.dtype)


def _ssd(proj, bsz, seq, conv_w, conv_b, dtb_pad, alog_pad, dskip_e, norm_w):
    nc = seq // SSD_CHUNK
    L = SSD_CHUNK
    ltri = jnp.asarray(np.tril(np.ones((L, L), np.float32)), BF16)
    ex = np.zeros((LANES, SSD_WIDTH), np.float32)
    for h in range(SSD_HEADS):
        ex[h, h * SSD_HEAD_DIM:(h + 1) * SSD_HEAD_DIM] = 1.0
    ex = jnp.asarray(ex, BF16)
    const = lambda shape: pl.BlockSpec(shape, lambda b, c: (0,) * len(shape))
    cps = SSD_CHUNKS_PER_STEP
    ns = nc // cps
    rows = cps * L
    return pl.pallas_call(
        _ssd_kernel,
        grid=(bsz, ns),
        in_specs=[
            pl.BlockSpec((rows, CONV_CH), lambda b, c: (b * ns + c, COL_XBC // CONV_CH)),
            pl.BlockSpec((rows, SSD_WIDTH), lambda b, c: (b * ns + c, COL_Z // SSD_WIDTH)),
            pl.BlockSpec((rows, LANES), lambda b, c: (b * ns + c, COL_DT // LANES)),
            const((CONV_WIDTH, CONV_CH)), const((1, CONV_CH)), const((1, LANES)), const((1, LANES)),
            const((1, SSD_WIDTH)), const((1, SSD_WIDTH)), const((L, L)), const((LANES, SSD_WIDTH)),
        ],
        out_specs=pl.BlockSpec((rows, SSD_WIDTH), lambda b, c: (b * ns + c, 0)),
        out_shape=jax.ShapeDtypeStruct((bsz * seq, SSD_WIDTH), BF16),
        scratch_shapes=[
            pltpu.VMEM((L + 8, CONV_CH), F32),
            pltpu.VMEM((SSD_GROUPS, SSD_STATE, (SSD_HEADS // SSD_GROUPS) * SSD_HEAD_DIM), F32),
            pltpu.VMEM((cps, L, SSD_WIDTH), F32),
        ],
        compiler_params=_cparams(("parallel", "arbitrary")),
        name="ssd",
    )(proj, proj, proj, conv_w, conv_b, dtb_pad, alog_pad, dskip_e, norm_w, ltri, ex)


def _rope(x, cos, s1, s2):
    n = x.shape[-1]
    half = ROPE_DIM // 2
    return x * cos + pltpu.roll(x, n - half, 1) * s1 + pltpu.roll(x, half, 1) * s2


def _merge_heads(parts, upper=False):
    lane = lax.broadcasted_iota(jnp.int32, parts[0].shape, 1)
    out = []
    for k in range(len(parts) // 2):
        a, b = parts[2 * k], parts[2 * k + 1]
        if upper:
            out.append(jnp.where(lane < HEAD_DIM, pltpu.roll(a, HEAD_DIM, 1), b))
        else:
            out.append(jnp.where(lane < HEAD_DIM, a, pltpu.roll(b, HEAD_DIM, 1)))
    return jnp.concatenate(out, axis=1)


def _kvprep_kernel(ks_ref, vs_ref, kw_ref, vw_ref, cos_ref, s1_ref, s2_ref, sel_ref,
                   kso_ref, vso_ref, kwo_ref, vwo_ref, *, tiles_per_seq):
    tr = ks_ref.shape[0]
    wide = KV_HEADS * LANES
    s0 = (pl.program_id(0) % tiles_per_seq) * tr
    cos, s1, s2 = cos_ref[...], s1_ref[...], s2_ref[...]
    sel = sel_ref[...]
    lane = lax.broadcasted_iota(jnp.int32, (tr, wide), 1) % LANES
    blk = (s0 + lax.broadcasted_iota(jnp.int32, (tr, wide), 0)) // SEL_BLOCK
    ebias = jnp.where(lane - SEL_BLOCK == blk, BLOCK_BIAS, 0.0)
    ones = jnp.where(lane >= HEAD_DIM, 1.0, 0.0)

    def place(x):
        return jnp.dot(x.astype(BF16), sel, preferred_element_type=F32)

    kso_ref[...] = (place(_rope(ks_ref[...], cos, s1, s2)) + ebias).astype(BF16)
    kwo_ref[...] = place(_rope(kw_ref[...], cos, s1, s2)).astype(BF16)
    vso_ref[...] = (place(vs_ref[...]) + ones).astype(BF16)
    vwo_ref[...] = (place(vw_ref[...]) + ones).astype(BF16)


def _head_place_matrix():
    m = np.zeros((KV_WIDTH, KV_HEADS * LANES), np.float32)
    for h in range(KV_HEADS):
        for d in range(HEAD_DIM):
            m[h * HEAD_DIM + d, h * LANES + d] = 1.0
    return m


def _kv_prep(proj, seq, cos_t, s1_t, s2_t):
    t = proj.shape[0]
    tr = 512
    tps = seq // tr
    sel = jnp.asarray(_head_place_matrix(), BF16)
    kvb = COL_KV // KV_WIDTH
    seg = lambda k: pl.BlockSpec((tr, KV_WIDTH), lambda i: (i, kvb + k))
    tab = pl.BlockSpec((tr, KV_WIDTH), lambda i: (i % tps, 0))
    wide = KV_HEADS * LANES
    out = pl.BlockSpec((tr, wide), lambda i: (i, 0))
    shp = jax.ShapeDtypeStruct((t, wide), BF16)
    return pl.pallas_call(
        functools.partial(_kvprep_kernel, tiles_per_seq=tps),
        grid=(t // tr,),
        in_specs=[seg(2), seg(3), seg(4), seg(5), tab, tab, tab,
                  pl.BlockSpec((KV_WIDTH, wide), lambda i: (0, 0))],
        out_specs=[out, out, out, out],
        out_shape=[shp, shp, shp, shp],
        compiler_params=_cparams(("parallel",)),
        name="kv_prep",
    )(proj, proj, proj, proj, cos_t, s1_t, s2_t, sel)


def _compress_kernel(gk_ref, gv_ref, pek_ref, pev_ref, w1k_ref, w1v_ref, w2kt_ref, w2v_ref,
                     kblk_ref, vblk_ref, shift_ref):
    ng = gk_ref.shape[0] // CMP_STRIDE
    heads_per_block = LANES // HEAD_DIM
    mine = (lax.broadcasted_iota(jnp.int32, (ng, LANES), 1) // HEAD_DIM) == pl.program_id(1) % heads_per_block

    def hidden(x_ref, pe_ref, w1_ref):
        top = jnp.zeros((ng, CMP_HIDDEN), F32)
        bot = jnp.zeros((ng, CMP_HIDDEN), F32)
        for r in range(CMP_STRIDE):
            x = x_ref[pl.ds(r, ng, stride=CMP_STRIDE), :]
            xt = jnp.where(mine, x + pe_ref[r:r + 1, :], 0.0).astype(BF16)
            xb = jnp.where(mine, x + pe_ref[CMP_STRIDE + r:CMP_STRIDE + r + 1, :], 0.0).astype(BF16)
            top = top + jnp.dot(xt, w1_ref[r], preferred_element_type=F32)
            bot = bot + jnp.dot(xb, w1_ref[CMP_STRIDE + r], preferred_element_type=F32)
        shift_ref[0:ng, :] = bot
        shift_ref[ng:ng + 8, :] = jnp.zeros((8, CMP_HIDDEN), F32)
        return _silu(top + shift_ref[1:ng + 1, :]).astype(BF16)

    hk = hidden(gk_ref, pek_ref, w1k_ref)
    kt = lax.dot_general(w2kt_ref[...], hk, (((1,), (1,)), ((), ())), preferred_element_type=F32)
    kblk_ref[...] = jnp.zeros(kblk_ref.shape, BF16)
    for g in range(ATT_GROUP):
        kblk_ref[0, 0, g * HEAD_DIM:(g + 1) * HEAD_DIM, g * ng:(g + 1) * ng] = kt.astype(BF16)

    hv = hidden(gv_ref, pev_ref, w1v_ref)
    vc = jnp.dot(hv, w2v_ref[...], preferred_element_type=F32)
    vblk_ref[...] = jnp.zeros(vblk_ref.shape, BF16)
    for g in range(ATT_GROUP):
        vblk_ref[0, 0, g * ng:(g + 1) * ng, g * LANES:(g + 1) * LANES] = vc.astype(BF16)


def _compress(proj, bsz, seq, pek, pev, w1k, w1v, w2kt, w2v):
    ng = seq // CMP_STRIDE
    hpb = LANES // HEAD_DIM
    kcb = COL_KV // LANES
    vcb = kcb + KV_WIDTH // LANES
    const = lambda shape: pl.BlockSpec(shape, lambda b, h: (0,) * len(shape))
    return pl.pallas_call(
        _compress_kernel,
        grid=(bsz, KV_HEADS),
        in_specs=[pl.BlockSpec((seq, LANES), lambda b, h: (b, kcb + h // hpb)),
                  pl.BlockSpec((seq, LANES), lambda b, h: (b, vcb + h // hpb)),
                  const((CMP_BLOCK, LANES)), const((CMP_BLOCK, LANES)),
                  const((CMP_BLOCK, LANES, CMP_HIDDEN)), const((CMP_BLOCK, LANES, CMP_HIDDEN)),
                  const((HEAD_DIM, CMP_HIDDEN)), const((CMP_HIDDEN, LANES))],
        out_specs=[pl.BlockSpec((1, 1, KV_WIDTH, ATT_GROUP * ng), lambda b, h: (b, h, 0, 0)),
                   pl.BlockSpec((1, 1, ATT_GROUP * ng, ATT_GROUP * LANES), lambda b, h: (b, h, 0, 0))],
        out_shape=[jax.ShapeDtypeStruct((bsz, KV_HEADS, KV_WIDTH, ATT_GROUP * ng), BF16),
                   jax.ShapeDtypeStruct((bsz, KV_HEADS, ATT_GROUP * ng, ATT_GROUP * LANES), BF16)],
        scratch_shapes=[pltpu.VMEM((ng + 8, CMP_HIDDEN), F32)],
        compiler_params=_cparams(("parallel", "parallel")),
        name="compress",
    )(proj, proj, pek, pev, w1k, w1v, w2kt, w2v)


TQC = TQ
SUBLANES = 8


CMP_HEADS_PER_STEP = 4


def _nsacmp_kernel(q_ref, kblk_ref, vblk_ref, cos_ref, s1_ref, s2_ref, ovt_ref, selq_ref, placen_ref, eye_ref,
                   qaug_ref, ocmp_ref):
    for hh in range(CMP_HEADS_PER_STEP):
        lanes = slice(hh * KV_WIDTH, (hh + 1) * KV_WIDTH)
        _nsacmp_head(q_ref.at[:, lanes], kblk_ref.at[0, hh], vblk_ref.at[0, hh], cos_ref, s1_ref, s2_ref,
                     ovt_ref, selq_ref, placen_ref, eye_ref, qaug_ref.at[0, hh], ocmp_ref.at[:, lanes])


def _nsacmp_head(q_ref, kblk_ref, vblk_ref, cos_ref, s1_ref, s2_ref, ovt_ref, selq_ref, placen_ref, eye_ref,
                 qaug_ref, ocmp_ref):
    ncp = kblk_ref.shape[1] // ATT_GROUP
    nsel = ovt_ref.shape[0]
    nt = (((1,), (1,)), ((), ()))
    s0 = pl.program_id(2) * TQC
    q = q_ref[...]
    s_all = jnp.dot((q * SCALE).astype(BF16), kblk_ref[...], preferred_element_type=F32)
    tq_pos = s0 + lax.broadcasted_iota(jnp.int32, (TQC, ncp), 0)
    cmp_end = lax.broadcasted_iota(jnp.int32, (TQC, ncp), 1) * CMP_STRIDE + (CMP_BLOCK - 1)
    mask = cmp_end <= tq_pos
    any_visible = (s0 + lax.broadcasted_iota(jnp.int32, (TQC, 1), 0) >= CMP_BLOCK - 1).astype(F32)
    ps = []
    for g in range(ATT_GROUP):
        s = jnp.where(mask, s_all[:, g * ncp:(g + 1) * ncp], NEG_INF)
        m = jnp.max(s, axis=-1, keepdims=True)
        e = jnp.exp(s - m)
        ps.append(e * (any_visible / jnp.sum(e, axis=-1, keepdims=True)))
    p_all = jnp.concatenate(ps, axis=1)

    def stack_rows(wide):
        return jnp.concatenate([wide[u * TQ:(u + 1) * TQ, g * LANES:(g + 1) * LANES]
                                for u in range(TQC // TQ) for g in range(ATT_GROUP)], axis=0)

    ow = jnp.dot(p_all.astype(BF16), vblk_ref[...], preferred_element_type=F32)
    ocmp_ref[...] = _merge_heads([ow[:, g * LANES:(g + 1) * LANES] for g in range(ATT_GROUP)])

    psum = (ps[0] + ps[1]) + (ps[2] + ps[3])
    p_hi = psum.astype(BF16)
    p_lo = (psum - p_hi.astype(F32)).astype(BF16)
    ovt = ovt_ref[...]
    imp = (lax.dot_general(ovt, p_hi, nt, preferred_element_type=F32)
           + lax.dot_general(ovt, p_lo, nt, preferred_element_type=F32))
    j = lax.broadcasted_iota(jnp.int32, (nsel, TQC), 0)
    cur = (s0 + lax.broadcasted_iota(jnp.int32, (nsel, TQC), 1)) // SEL_BLOCK
    imp = jnp.where((j == 0) | (j == cur) | (j == cur - 1), FORCE_SCORE, imp)
    imp = jnp.where(j <= cur, imp, -1.0)
    nblk = nsel // SUBLANES
    blocks = [imp[k * SUBLANES:(k + 1) * SUBLANES, :] for k in range(nblk)]
    ranks = [jnp.zeros((SUBLANES, TQC), F32) for _ in range(nblk)]
    sub = lax.broadcasted_iota(jnp.int32, (SUBLANES, TQC), 0)
    for jp in range(nsel):
        r = jnp.broadcast_to(imp[jp:jp + 1, :], (SUBLANES, TQC))
        for k in range(nblk):
            if k > jp // SUBLANES:
                inc = jnp.where(r >= blocks[k], 1.0, 0.0)
            elif k < jp // SUBLANES:
                inc = jnp.where(r > blocks[k], 1.0, 0.0)
            else:
                tie = jnp.where(sub > jp % SUBLANES, 1.0, 0.0)
                inc = jnp.where(r > blocks[k], 1.0, jnp.where(r == blocks[k], tie, 0.0))
            ranks[k] = ranks[k] + inc
    rank = jnp.concatenate(ranks, axis=0)
    notsel_t = jnp.where((rank < float(N_SELECT)) & (imp >= 0.0), 0.0, 1.0).astype(BF16)
    notsel = lax.dot_general(eye_ref[...], notsel_t, nt, preferred_element_type=F32)

    q_rot = (_rope(q, cos_ref[...], s1_ref[...], s2_ref[...]) * SCALE).astype(BF16)
    qw = (jnp.dot(q_rot, selq_ref[...], preferred_element_type=F32)
          + jnp.dot(notsel.astype(BF16), placen_ref[...], preferred_element_type=F32))
    qaug_ref[...] = stack_rows(qw).astype(BF16)


def _selection_overlap_t(n_cmp_pad, n_cmp, n_sel):
    cs = np.arange(n_cmp)[:, None] * CMP_STRIDE
    ce = cs + CMP_BLOCK
    ss = np.arange(n_sel)[None, :] * SEL_BLOCK
    se = ss + SEL_BLOCK
    ov = np.clip(np.minimum(ce, se) - np.maximum(cs, ss), 0, None) / CMP_BLOCK
    full = np.zeros((n_cmp_pad, n_sel), np.float32)
    full[:n_cmp] = ov
    return full.T.copy()


def _nsa_cmp(proj, kblk, vblk, bsz, seq, cos_t, s1_t, s2_t):
    nqc = seq // TQC
    ncp = kblk.shape[3] // ATT_GROUP
    n_cmp = (seq - CMP_BLOCK) // CMP_STRIDE + 1
    nsel = seq // SEL_BLOCK
    ovt = jnp.asarray(_selection_overlap_t(ncp, n_cmp, nsel), BF16)
    selq = jnp.asarray(_head_place_matrix(), BF16)
    placen = np.zeros((nsel, ATT_GROUP * LANES), np.float32)
    for g in range(ATT_GROUP):
        for jb in range(nsel):
            placen[jb, g * LANES + HEAD_DIM + jb] = 1.0
    placen = jnp.asarray(placen, BF16)
    eye = jnp.asarray(np.eye(TQC, dtype=np.float32), BF16)
    const = lambda shape: pl.BlockSpec(shape, lambda b, h, t: (0,) * len(shape))
    tab = pl.BlockSpec((TQC, KV_WIDTH), lambda b, h, t: (t, 0))
    hps = CMP_HEADS_PER_STEP
    qb = COL_Q // (hps * KV_WIDTH)
    rows = (TQC // TQ) * ROWS
    stacked = pl.BlockSpec((1, hps, rows, LANES), lambda b, h, t: (b, h, t, 0))
    return pl.pallas_call(
        _nsacmp_kernel,
        grid=(bsz, KV_HEADS // hps, nqc),
        in_specs=[
            pl.BlockSpec((TQC, hps * KV_WIDTH), lambda b, h, t: (b * nqc + t, qb + h)),
            pl.BlockSpec((1, hps, KV_WIDTH, ATT_GROUP * ncp), lambda b, h, t: (b, h, 0, 0)),
            pl.BlockSpec((1, hps, ATT_GROUP * ncp, ATT_GROUP * LANES), lambda b, h, t: (b, h, 0, 0)),
            tab, tab, tab,
            const((nsel, ncp)), const((KV_WIDTH, ATT_GROUP * LANES)),
            const((nsel, ATT_GROUP * LANES)), const((TQC, TQC)),
        ],
        out_specs=[stacked, pl.BlockSpec((TQC, hps * KV_WIDTH), lambda b, h, t: (b * nqc + t, h))],
        out_shape=[jax.ShapeDtypeStruct((bsz, KV_HEADS, nqc * rows, LANES), BF16),
                   jax.ShapeDtypeStruct((bsz * seq, ATT_WIDTH), F32)],
        compiler_params=_cparams(("parallel", "parallel", "parallel")),
        name="nsa_cmp",
    )(proj, kblk, vblk, cos_t, s1_t, s2_t, ovt, selq, placen, eye)


ATT_HEADS_PER_STEP = 2


def _nsaattn_kernel(qaug_ref, ocmp_ref, ks_ref, vs_ref, kw_ref, vw_ref, gate_ref, gexp_ref, db_ref, wb_ref, o_ref,
                    m_ref, acc_ref):
    qt = pl.program_id(2)
    s0 = qt * TQ
    nt = (((1,), (1,)), ((), ()))
    heads = range(ATT_HEADS_PER_STEP)
    qs = [qaug_ref[0, hh] for hh in heads]
    klanes = [slice(hh * LANES, (hh + 1) * LANES) for hh in heads]
    olanes = [slice(hh * KV_WIDTH, (hh + 1) * KV_WIDTH) for hh in heads]

    def per_head(bias):
        return jnp.concatenate([bias] * ATT_GROUP, axis=0)

    m_ref[...] = jnp.full(m_ref.shape, NEG_INF, F32)
    acc_ref[...] = jnp.zeros(acc_ref.shape, F32)

    def tile(first_key, width, causal=False):
        start = pl.multiple_of(first_key, width)
        for hh in heads:
            k = ks_ref[pl.ds(start, width), klanes[hh]]
            v = vs_ref[pl.ds(start, width), klanes[hh]]
            s = lax.dot_general(qs[hh], k, nt, preferred_element_type=F32)
            if causal:
                s = s + per_head(db_ref[0][:, :width])
            m_prev = m_ref[hh]
            m_next = jnp.maximum(m_prev, jnp.max(s, axis=-1, keepdims=True))
            p = jnp.exp(s - jnp.concatenate([m_next] * (width // LANES), axis=1))
            acc_ref[hh] = acc_ref[hh] * jnp.exp(m_prev - m_next) + jnp.dot(
                p.astype(BF16), v, preferred_element_type=F32)
            m_ref[hh] = m_next

    nfull = (qt * TQ) // TK

    def body(i, carry):
        tile(i * (2 * TK), 2 * TK)
        return carry

    lax.fori_loop(0, nfull // 2, body, 0)

    @pl.when(nfull % 2 == 1)
    def _():
        tile((nfull - 1) * TK, TK)

    @pl.when(s0 == nfull * TK)
    def _():
        tile(s0, TQ, causal=True)

    @pl.when(s0 != nfull * TK)
    def _():
        tile(nfull * TK, TK, causal=True)

    def normalised(x):
        parts = [x[g * TQ:(g + 1) * TQ] for g in range(ATT_GROUP)]
        return _merge_heads(parts) / _merge_heads(parts, upper=True)

    wk = WINDOW + TQ
    wstart = pl.multiple_of(jnp.maximum(s0 - WINDOW, 0), TQ)
    sig_terms = _split_bf16(jax.nn.sigmoid(gate_ref[...]), 2)
    for hh in heads:
        kwin = kw_ref[pl.ds(wstart, wk), klanes[hh]]
        vwin = vw_ref[pl.ds(wstart, wk), klanes[hh]]
        sw = lax.dot_general(qs[hh], kwin, nt, preferred_element_type=F32) + per_head(wb_ref[0])
        pw = jnp.exp(sw - jnp.max(sw, axis=-1, keepdims=True))
        ow = jnp.dot(pw.astype(BF16), vwin, preferred_element_type=F32)

        gexp = gexp_ref[hh]
        gmap = sum(jnp.dot(t, gexp, preferred_element_type=F32) for t in sig_terms)
        y = (gmap[:, 0:KV_WIDTH] * ocmp_ref[:, olanes[hh]]
             + gmap[:, KV_WIDTH:2 * KV_WIDTH] * normalised(acc_ref[hh])
             + gmap[:, 2 * KV_WIDTH:3 * KV_WIDTH] * normalised(ow))
        o_ref[:, olanes[hh]] = y.astype(o_ref.dtype)


def _nsa_attn(qaug, ocmp, ksa, vsa, kwa, vwa, proj, bsz, seq):
    nqt = seq // TQ
    gexp = np.zeros((KV_HEADS, LANES, N_BRANCH * KV_WIDTH), np.float32)
    for h in range(KV_HEADS):
        for g in range(ATT_GROUP):
            for br in range(N_BRANCH):
                src = (h * ATT_GROUP + g) * N_BRANCH + br
                gexp[h, src, br * KV_WIDTH + g * HEAD_DIM:br * KV_WIDTH + (g + 1) * HEAD_DIM] = 1.0
    gexp = jnp.asarray(gexp, BF16)
    r = np.arange(TQ)[:, None]
    diag_cases = TK // TQ
    db = np.stack([np.where(np.arange(TK)[None, :] <= c * TQ + r, 0.0, NEG_INF) for c in range(diag_cases)])
    win_cases = WINDOW // TQ + 1
    wk = WINDOW + TQ
    dist = lambda c: c * TQ + r - np.arange(wk)[None, :]
    wb = np.stack([np.where((dist(c) >= 0) & (dist(c) < WINDOW), 0.0, NEG_INF) for c in range(win_cases)])
    db = jnp.asarray(db, F32)
    wb = jnp.asarray(wb, F32)
    hps = ATT_HEADS_PER_STEP
    stacked = pl.BlockSpec((1, hps, ROWS, LANES), lambda b, h, t: (b, h, t, 0))
    kv = pl.BlockSpec((seq, hps * LANES), lambda b, h, t: (b, h))
    merged = pl.BlockSpec((TQ, hps * KV_WIDTH), lambda b, h, t: (b * nqt + t, h))
    return pl.pallas_call(
        _nsaattn_kernel,
        grid=(bsz, KV_HEADS // hps, nqt),
        in_specs=[stacked, merged, kv, kv, kv, kv,
                  pl.BlockSpec((TQ, LANES), lambda b, h, t: (b * nqt + t, COL_GATE // LANES)),
                  pl.BlockSpec((hps, LANES, N_BRANCH * KV_WIDTH), lambda b, h, t: (h, 0, 0)),
                  pl.BlockSpec((1, TQ, TK), lambda b, h, t: (t % diag_cases, 0, 0)),
                  pl.BlockSpec((1, TQ, wk), lambda b, h, t: (jnp.minimum(t, win_cases - 1), 0, 0))],
        out_specs=merged,
        out_shape=jax.ShapeDtypeStruct((bsz * seq, ATT_WIDTH), BF16),
        scratch_shapes=[pltpu.VMEM((hps, ROWS, LANES), F32), pltpu.VMEM((hps, ROWS, LANES), F32)],
        compiler_params=_cparams(("parallel", "parallel", "arbitrary")),
        name="nsa_attn",
    )(qaug, ocmp, ksa, vsa, kwa, vwa, proj, gexp, db, wb)


def _outproj_kernel(x_ref, ys_ref, ya_ref, w1_ref, w2_ref, o_ref):
    o_ref[...] = (x_ref[...] + jnp.dot(ys_ref[...], w1_ref[...], preferred_element_type=F32)
                  + jnp.dot(ya_ref[...], w2_ref[...], preferred_element_type=F32))


def _out_proj(x2, ys, ya, wo_b):
    t = x2.shape[0]
    tm = 512
    return pl.pallas_call(
        _outproj_kernel,
        grid=(t // tm,),
        in_specs=[
            pl.BlockSpec((tm, D_MODEL), lambda i: (i, 0)),
            pl.BlockSpec((tm, SSD_WIDTH), lambda i: (i, 0)),
            pl.BlockSpec((tm, ATT_WIDTH), lambda i: (i, 0)),
            pl.BlockSpec((SSD_WIDTH, D_MODEL), lambda i: (0, 0)),
            pl.BlockSpec((ATT_WIDTH, D_MODEL), lambda i: (1, 0)),
        ],
        out_specs=pl.BlockSpec((tm, D_MODEL), lambda i: (i, 0)),
        out_shape=jax.ShapeDtypeStruct((t, D_MODEL), F32),
        compiler_params=_cparams(("parallel",)),
        name="out_proj",
    )(x2, ys, ya, wo_b, wo_b)


def _ffn_kernel(h_ref, nw_ref, fw_ref, wg_ref, wu_ref, wd_ref, o_ref, v_ref):
    j = pl.program_id(1)

    @pl.when(j == 0)
    def _():
        h = h_ref[...]
        ms = jnp.mean(h * h, axis=-1, keepdims=True)
        v_ref[...] = ((h * lax.rsqrt(ms + NORM_EPS)) * nw_ref[...]).astype(BF16)
        o_ref[...] = h

    v = v_ref[...]
    gate = jnp.dot(v, wg_ref[...].astype(BF16), preferred_element_type=F32)
    up = jnp.dot(v, wu_ref[...].astype(BF16), preferred_element_type=F32)
    o_ref[...] += jnp.dot((_silu(gate) * up).astype(BF16), wd_ref[...].astype(BF16), preferred_element_type=F32)

    @pl.when(j == pl.num_programs(1) - 1)
    def _():
        h2 = o_ref[...]
        ms = jnp.mean(h2 * h2, axis=-1, keepdims=True)
        o_ref[...] = (h2 * lax.rsqrt(ms + NORM_EPS)) * fw_ref[...]


def _ffn(h1, ffn_nw, final_w, wg_b, wu_b, wd_b):
    t = h1.shape[0]
    tm, tf = 1024, 256
    return pl.pallas_call(
        _ffn_kernel,
        grid=(t // tm, D_FF // tf),
        in_specs=[
            pl.BlockSpec((tm, D_MODEL), lambda i, j: (i, 0)),
            pl.BlockSpec((1, D_MODEL), lambda i, j: (0, 0)),
            pl.BlockSpec((1, D_MODEL), lambda i, j: (0, 0)),
            pl.BlockSpec((D_MODEL, tf), lambda i, j: (0, j)),
            pl.BlockSpec((D_MODEL, tf), lambda i, j: (0, j)),
            pl.BlockSpec((tf, D_MODEL), lambda i, j: (j, 0)),
        ],
        out_specs=pl.BlockSpec((tm, D_MODEL), lambda i, j: (i, 0)),
        out_shape=jax.ShapeDtypeStruct((t, D_MODEL), F32),
        scratch_shapes=[pltpu.VMEM((tm, D_MODEL), BF16)],
        compiler_params=_cparams(("parallel", "arbitrary")),
        name="ffn",
    )(h1, ffn_nw, final_w, wg_b, wu_b, wd_b)


def _rope_tables(seq):
    f32 = np.float32
    inv = (f32(1.0) / (f32(ROPE_THETA) ** (np.arange(0, ROPE_DIM, 2, dtype=f32) / f32(ROPE_DIM)))).astype(f32)
    ang = (np.arange(seq, dtype=f32)[:, None] * inv[None, :]).astype(f32)
    cos, sin = np.cos(ang).astype(f32), np.sin(ang).astype(f32)
    half = ROPE_DIM // 2
    rest_one = np.ones((seq, HEAD_DIM - ROPE_DIM), f32)
    rest_zero = np.zeros((seq, HEAD_DIM - ROPE_DIM), f32)
    zero_h = np.zeros((seq, half), f32)
    cos_h = np.concatenate([cos, cos, rest_one], axis=1)
    s1_h = np.concatenate([-sin, zero_h, rest_zero], axis=1)
    s2_h = np.concatenate([zero_h, sin, rest_zero], axis=1)
    tile = lambda a: jnp.asarray(np.tile(a, (1, KV_HEADS)))
    return tile(cos_h), tile(s1_h), tile(s2_h)


def _pad_lanes(a, width):
    return jnp.pad(a, ((0, 0), (0, width - a.shape[1])))


def _layer(h2d, bsz, seq, p):
    (attn_norm_w, w_in, conv_w, conv_b, dt_bias, a_log, d_skip, ssd_norm_w, cmp_w1_k, cmp_w2_k, cmp_w1_v,
     cmp_w2_v, cmp_pe_k, cmp_pe_v, w_out, ffn_norm_w, w_gate, w_up, w_down) = p
    o_xbc, o_dt, o_q, o_kv, o_gate = 1024, 2560, 2576, 3600, 5136
    w_perm = jnp.concatenate([
        w_in[:, o_xbc:o_dt], w_in[:, o_kv:o_gate], w_in[:, :o_xbc], w_in[:, o_q:o_kv],
        _pad_lanes(w_in[:, o_dt:o_q], LANES), _pad_lanes(w_in[:, o_gate:], LANES)], axis=1).astype(BF16)
    proj = _in_proj(h2d, attn_norm_w[None, :], w_perm)

    y_ssd = _ssd(proj, bsz, seq, conv_w, conv_b[None, :], _pad_lanes(dt_bias[None, :], LANES),
                 _pad_lanes(a_log[None, :], LANES), jnp.repeat(d_skip, SSD_HEAD_DIM)[None, :],
                 ssd_norm_w[None, :])

    cos_t, s1_t, s2_t = _rope_tables(seq)
    ksa, vsa, kwa, vwa = _kv_prep(proj, seq, cos_t, s1_t, s2_t)

    hpb = LANES // HEAD_DIM
    pe_rep = lambda pe: jnp.tile(pe, (1, hpb))
    w1_rep = lambda w1: jnp.tile(w1.astype(BF16).reshape(CMP_BLOCK, HEAD_DIM, CMP_HIDDEN), (1, hpb, 1))
    kblk, vblk = _compress(proj, bsz, seq, pe_rep(cmp_pe_k), pe_rep(cmp_pe_v),
                           w1_rep(cmp_w1_k), w1_rep(cmp_w1_v), cmp_w2_k.T.astype(BF16),
                           _pad_lanes(cmp_w2_v, LANES).astype(BF16))
    qaug, ocmp = _nsa_cmp(proj, kblk, vblk, bsz, seq, cos_t, s1_t, s2_t)
    y_att = _nsa_attn(qaug, ocmp, ksa, vsa, kwa, vwa, proj, bsz, seq)

    h1 = _out_proj(h2d, y_ssd, y_att, w_out.astype(BF16))
    return h1, (ffn_norm_w, w_gate, w_up, w_down)


def kernel(x, attn_norm_w, w_in, conv_w, conv_b, dt_bias, a_log, d_skip, ssd_norm_w, cmp_w1_k, cmp_w2_k,
           cmp_w1_v, cmp_w2_v, cmp_pe_k, cmp_pe_v, w_out, ffn_norm_w, w_gate, w_up, w_down, final_norm_w):
    bsz, seq, _ = x.shape
    depth = w_in.shape[0]
    assert depth == 1, "the final rmsnorm is fused into the last layer's ffn kernel"
    h = x.reshape(bsz * seq, D_MODEL)
    l = 0
    params = (attn_norm_w[l], w_in[l], conv_w[l], conv_b[l], dt_bias[l], a_log[l], d_skip[l], ssd_norm_w[l],
              cmp_w1_k[l], cmp_w2_k[l], cmp_w1_v[l], cmp_w2_v[l], cmp_pe_k[l], cmp_pe_v[l], w_out[l],
              ffn_norm_w[l], w_gate[l], w_up[l], w_down[l])
    h1, (fnw, wg_b, wu_b, wd_b) = _layer(h, bsz, seq, params)
    out = _ffn(h1, fnw[None, :], final_norm_w[None, :], wg_b, wu_b, wd_b)
    return out.reshape(bsz, seq, D_MODEL)
```

```python
import functools

import numpy as np
import jax
import jax.numpy as jnp
from jax import lax
from jax.experimental import pallas as pl
from jax.experimental.pallas import tpu as pltpu

F32 = jnp.float32
BF16 = jnp.bfloat16

D_MODEL = 2048
SSD_WIDTH = 1024
ATT_WIDTH = 1024
SSD_HEAD_DIM = 64
SSD_HEADS = 16
SSD_GROUPS = 2
SSD_STATE = 128
SSD_CHUNK = 128
CONV_WIDTH = 4
CONV_CH = SSD_WIDTH + 2 * SSD_GROUPS * SSD_STATE
HEAD_DIM = 64
ATT_HEADS = 16
KV_HEADS = 4
ATT_GROUP = 4
KV_WIDTH = KV_HEADS * HEAD_DIM
CMP_BLOCK = 32
CMP_STRIDE = 16
CMP_HIDDEN = 256
SEL_BLOCK = 64
N_SELECT = 16
WINDOW = 512
N_BRANCH = 3
ROPE_THETA = 500000.0
ROPE_DIM = 16
D_FF = 5632
NORM_EPS = 1e-6
NEG_INF = -1e30
FORCE_SCORE = 1e4
SCALE = HEAD_DIM ** -0.5
BLOCK_BIAS = -(2.0 ** 100)

LANES = 128
VMEM_LIMIT = 56 * 1024 * 1024

NP = 5376
COL_XBC = 0
COL_KV = 1536
COL_Z = 3072
COL_Q = 4096
COL_DT = 5120
COL_GATE = 5248

TQ = 256
TK = 512
ROWS = ATT_GROUP * TQ


def _silu(x):
    return x * jax.nn.sigmoid(x)


def _split_bf16(x, terms):
    out = []
    for _ in range(terms - 1):
        t = x.astype(BF16)
        out.append(t)
        x = x - t.astype(F32)
    out.append(x.astype(BF16))
    return out


def _cparams(sem):
    return pltpu.CompilerParams(dimension_semantics=sem, vmem_limit_bytes=VMEM_LIMIT)


def _inproj_kernel(x_ref, nw_ref, w_ref, o_ref, u_ref):
    @pl.when(pl.program_id(1) == 0)
    def _():
        x = x_ref[...]
        ms = jnp.mean(x * x, axis=-1, keepdims=True)
        u_ref[...] = ((x * lax.rsqrt(ms + NORM_EPS)) * nw_ref[...]).astype(BF16)

    o_ref[...] = jnp.dot(u_ref[...], w_ref[...], preferred_element_type=F32)


def _in_proj(x2, norm_w, w_perm):
    t = x2.shape[0]
    tm, tn = 1024, 1792
    return pl.pallas_call(
        _inproj_kernel,
        grid=(t // tm, NP // tn),
        in_specs=[
            pl.BlockSpec((tm, D_MODEL), lambda i, j: (i, 0)),
            pl.BlockSpec((1, D_MODEL), lambda i, j: (0, 0)),
            pl.BlockSpec((D_MODEL, tn), lambda i, j: (0, j)),
        ],
        out_specs=pl.BlockSpec((tm, tn), lambda i, j: (i, j)),
        out_shape=jax.ShapeDtypeStruct((t, NP), F32),
        scratch_shapes=[pltpu.VMEM((tm, D_MODEL), BF16)],
        compiler_params=_cparams(("parallel", "arbitrary")),
        name="in_proj",
    )(x2, norm_w, w_perm)


SSD_CHUNKS_PER_STEP = 2


def _ssd_kernel(xbc_ref, z_ref, dt_ref, cw_ref, cb_ref, dtb_ref, alog_ref, dskip_ref, nw_ref,
                ltri_ref, ex_ref, o_ref, ext_ref, state_ref, y_ref):
    @pl.when(pl.program_id(1) == 0)
    def _():
        ext_ref[0:8, :] = jnp.zeros((8, CONV_CH), F32)
        state_ref[...] = jnp.zeros(state_ref.shape, F32)

    for u in range(SSD_CHUNKS_PER_STEP):
        rows = slice(u * SSD_CHUNK, (u + 1) * SSD_CHUNK)
        _ssd_chunk(xbc_ref.at[rows, :], z_ref.at[rows, :], dt_ref.at[rows, :], cw_ref, cb_ref, dtb_ref, alog_ref,
                   dskip_ref, nw_ref, ltri_ref, ex_ref, o_ref.at[rows, :], ext_ref, state_ref, y_ref.at[u])


def _ssd_chunk(xbc_ref, z_ref, dt_ref, cw_ref, cb_ref, dtb_ref, alog_ref, dskip_ref, nw_ref,
               ltri_ref, ex_ref, o_ref, ext_ref, state_ref, y_ref):
    L = SSD_CHUNK
    ext_ref[8:8 + L, :] = xbc_ref[...]
    w = cw_ref[...]
    y = (ext_ref[5:5 + L, :] * w[0:1, :] + ext_ref[6:6 + L, :] * w[1:2, :]
         + ext_ref[7:7 + L, :] * w[2:3, :] + ext_ref[8:8 + L, :] * w[3:4, :]) + cb_ref[...]
    tail = ext_ref[L:L + 8, :]
    ext_ref[0:8, :] = tail
    act = _silu(y)
    xs = act[:, :SSD_WIDTH]
    bm = act[:, SSD_WIDTH:SSD_WIDTH + SSD_GROUPS * SSD_STATE]
    cm = act[:, SSD_WIDTH + SSD_GROUPS * SSD_STATE:]

    v = dt_ref[...] + dtb_ref[...]
    dt = jnp.maximum(v, 0.0) + jnp.log1p(jnp.exp(-jnp.abs(v)))
    a = -jnp.exp(alog_ref[...])
    adt = a * dt
    ltri = ltri_ref[...]
    acum = sum(jnp.dot(ltri, t, preferred_element_type=F32) for t in _split_bf16(adt, 3))
    acum_t = acum.T
    last = acum[L - 1:L, :]
    stacked = jnp.concatenate([dt, jnp.exp(acum), jnp.exp(last - acum)], axis=0)
    ex = ex_ref[...]
    expanded = sum(jnp.dot(t, ex, preferred_element_type=F32) for t in _split_bf16(stacked, 2))
    dt_e = expanded[0:L]
    expa_e = expanded[L:2 * L]
    dst_e = expanded[2 * L:3 * L]

    xdt = xs * dt_e
    xds_b = (xdt * dst_e).astype(BF16)
    row = lax.broadcasted_iota(jnp.int32, (L, L), 0)
    col = lax.broadcasted_iota(jnp.int32, (L, L), 1)
    causal = row >= col
    lane = lax.broadcasted_iota(jnp.int32, (L, LANES), 1)
    hg = SSD_HEADS // SSD_GROUPS
    gw = hg * SSD_HEAD_DIM
    for g in range(SSD_GROUPS):
        bg = bm[:, g * SSD_STATE:(g + 1) * SSD_STATE]
        cg_b = cm[:, g * SSD_STATE:(g + 1) * SSD_STATE].astype(BF16)
        bg_b = bg.astype(BF16)
        cb = lax.dot_general(cg_b, bg_b, (((1,), (1,)), ((), ())), preferred_element_type=F32)
        hprev = state_ref[g]
        yoff = jnp.dot(cg_b, hprev.astype(BF16), preferred_element_type=F32) * expa_e[:, g * gw:(g + 1) * gw]
        snew = jnp.dot(bg.T.astype(BF16), xds_b[:, g * gw:(g + 1) * gw], preferred_element_type=F32)
        state_ref[g] = hprev * expa_e[L - 1:L, g * gw:(g + 1) * gw] + snew
        for k in range(hg // 2):
            pair = g * (hg // 2) + k
            ms = []
            for hh in (2 * pair, 2 * pair + 1):
                seg = acum[:, hh:hh + 1] - acum_t[hh:hh + 1, :]
                decay = jnp.exp(jnp.where(causal, seg, -jnp.inf))
                ms.append((cb * decay).astype(BF16))
            mpair = jnp.concatenate(ms, axis=1)
            xp = xdt[:, pair * LANES:(pair + 1) * LANES]
            xblk = jnp.concatenate([jnp.where(lane < SSD_HEAD_DIM, xp, 0.0),
                                    jnp.where(lane >= SSD_HEAD_DIM, xp, 0.0)], axis=0).astype(BF16)
            yd = jnp.dot(mpair, xblk, preferred_element_type=F32)
            y_ref[:, pair * LANES:(pair + 1) * LANES] = yd + yoff[:, k * LANES:(k + 1) * LANES]

    yy = y_ref[...] + dskip_ref[...] * xs
    z = z_ref[...]
    yz = yy * _silu(z)
    ms2 = jnp.mean(yz * yz, axis=-1, keepdims=True)
    o_ref[...] = ((yz * lax.rsqrt(ms2 + NORM_EPS)) * nw_ref[...]).astype(o_ref.dtype)


def _ssd(proj, bsz, seq, conv_w, conv_b, dtb_pad, alog_pad, dskip_e, norm_w):
    nc = seq // SSD_CHUNK
    L = SSD_CHUNK
    ltri = jnp.asarray(np.tril(np.ones((L, L), np.float32)), BF16)
    ex = np.zeros((LANES, SSD_WIDTH), np.float32)
    for h in range(SSD_HEADS):
        ex[h, h * SSD_HEAD_DIM:(h + 1) * SSD_HEAD_DIM] = 1.0
    ex = jnp.asarray(ex, BF16)
    const = lambda shape: pl.BlockSpec(shape, lambda b, c: (0,) * len(shape))
    cps = SSD_CHUNKS_PER_STEP
    ns = nc // cps
    rows = cps * L
    return pl.pallas_call(
        _ssd_kernel,
        grid=(bsz, ns),
        in_specs=[
            pl.BlockSpec((rows, CONV_CH), lambda b, c: (b * ns + c, COL_XBC // CONV_CH)),
            pl.BlockSpec((rows, SSD_WIDTH), lambda b, c: (b * ns + c, COL_Z // SSD_WIDTH)),
            pl.BlockSpec((rows, LANES), lambda b, c: (b * ns + c, COL_DT // LANES)),
            const((CONV_WIDTH, CONV_CH)), const((1, CONV_CH)), const((1, LANES)), const((1, LANES)),
            const((1, SSD_WIDTH)), const((1, SSD_WIDTH)), const((L, L)), const((LANES, SSD_WIDTH)),
        ],
        out_specs=pl.BlockSpec((rows, SSD_WIDTH), lambda b, c: (b * ns + c, 0)),
        out_shape=jax.ShapeDtypeStruct((bsz * seq, SSD_WIDTH), BF16),
        scratch_shapes=[
            pltpu.VMEM((L + 8, CONV_CH), F32),
            pltpu.VMEM((SSD_GROUPS, SSD_STATE, (SSD_HEADS // SSD_GROUPS) * SSD_HEAD_DIM), F32),
            pltpu.VMEM((cps, L, SSD_WIDTH), F32),
        ],
        compiler_params=_cparams(("parallel", "arbitrary")),
        name="ssd",
    )(proj, proj, proj, conv_w, conv_b, dtb_pad, alog_pad, dskip_e, norm_w, ltri, ex)


def _rope(x, cos, s1, s2):
    n = x.shape[-1]
    half = ROPE_DIM // 2
    return x * cos + pltpu.roll(x, n - half, 1) * s1 + pltpu.roll(x, half, 1) * s2


def _merge_heads(parts, upper=False):
    lane = lax.broadcasted_iota(jnp.int32, parts[0].shape, 1)
    out = []
    for k in range(len(parts) // 2):
        a, b = parts[2 * k], parts[2 * k + 1]
        if upper:
            out.append(jnp.where(lane < HEAD_DIM, pltpu.roll(a, HEAD_DIM, 1), b))
        else:
            out.append(jnp.where(lane < HEAD_DIM, a, pltpu.roll(b, HEAD_DIM, 1)))
    return jnp.concatenate(out, axis=1)


def _kvprep_kernel(ks_ref, vs_ref, kw_ref, vw_ref, cos_ref, s1_ref, s2_ref, sel_ref,
                   kso_ref, vso_ref, kwo_ref, vwo_ref, *, tiles_per_seq):
    tr = ks_ref.shape[0]
    wide = KV_HEADS * LANES
    s0 = (pl.program_id(0) % tiles_per_seq) * tr
    cos, s1, s2 = cos_ref[...], s1_ref[...], s2_ref[...]
    sel = sel_ref[...]
    lane = lax.broadcasted_iota(jnp.int32, (tr, wide), 1) % LANES
    blk = (s0 + lax.broadcasted_iota(jnp.int32, (tr, wide), 0)) // SEL_BLOCK
    ebias = jnp.where(lane - SEL_BLOCK == blk, BLOCK_BIAS, 0.0)
    ones = jnp.where(lane >= HEAD_DIM, 1.0, 0.0)

    def place(x):
        return jnp.dot(x.astype(BF16), sel, preferred_element_type=F32)

    kso_ref[...] = (place(_rope(ks_ref[...], cos, s1, s2)) + ebias).astype(BF16)
    kwo_ref[...] = place(_rope(kw_ref[...], cos, s1, s2)).astype(BF16)
    vso_ref[...] = (place(vs_ref[...]) + ones).astype(BF16)
    vwo_ref[...] = (place(vw_ref[...]) + ones).astype(BF16)


def _head_place_matrix():
    m = np.zeros((KV_WIDTH, KV_HEADS * LANES), np.float32)
    for h in range(KV_HEADS):
        for d in range(HEAD_DIM):
            m[h * HEAD_DIM + d, h * LANES + d] = 1.0
    return m


def _kv_prep(proj, seq, cos_t, s1_t, s2_t):
    t = proj.shape[0]
    tr = 512
    tps = seq // tr
    sel = jnp.asarray(_head_place_matrix(), BF16)
    kvb = COL_KV // KV_WIDTH
    seg = lambda k: pl.BlockSpec((tr, KV_WIDTH), lambda i: (i, kvb + k))
    tab = pl.BlockSpec((tr, KV_WIDTH), lambda i: (i % tps, 0))
    wide = KV_HEADS * LANES
    out = pl.BlockSpec((tr, wide), lambda i: (i, 0))
    shp = jax.ShapeDtypeStruct((t, wide), BF16)
    return pl.pallas_call(
        functools.partial(_kvprep_kernel, tiles_per_seq=tps),
        grid=(t // tr,),
        in_specs=[seg(2), seg(3), seg(4), seg(5), tab, tab, tab,
                  pl.BlockSpec((KV_WIDTH, wide), lambda i: (0, 0))],
        out_specs=[out, out, out, out],
        out_shape=[shp, shp, shp, shp],
        compiler_params=_cparams(("parallel",)),
        name="kv_prep",
    )(proj, proj, proj, proj, cos_t, s1_t, s2_t, sel)


def _compress_kernel(gk_ref, gv_ref, pek_ref, pev_ref, w1k_ref, w1v_ref, w2kt_ref, w2v_ref,
                     kblk_ref, vblk_ref, shift_ref):
    ng = gk_ref.shape[0] // CMP_STRIDE
    heads_per_block = LANES // HEAD_DIM
    mine = (lax.broadcasted_iota(jnp.int32, (ng, LANES), 1) // HEAD_DIM) == pl.program_id(1) % heads_per_block

    def hidden(x_ref, pe_ref, w1_ref):
        top = jnp.zeros((ng, CMP_HIDDEN), F32)
        bot = jnp.zeros((ng, CMP_HIDDEN), F32)
        for r in range(CMP_STRIDE):
            x = x_ref[pl.ds(r, ng, stride=CMP_STRIDE), :]
            xt = jnp.where(mine, x + pe_ref[r:r + 1, :], 0.0).astype(BF16)
            xb = jnp.where(mine, x + pe_ref[CMP_STRIDE + r:CMP_STRIDE + r + 1, :], 0.0).astype(BF16)
            top = top + jnp.dot(xt, w1_ref[r], preferred_element_type=F32)
            bot = bot + jnp.dot(xb, w1_ref[CMP_STRIDE + r], preferred_element_type=F32)
        shift_ref[0:ng, :] = bot
        shift_ref[ng:ng + 8, :] = jnp.zeros((8, CMP_HIDDEN), F32)
        return _silu(top + shift_ref[1:ng + 1, :]).astype(BF16)

    hk = hidden(gk_ref, pek_ref, w1k_ref)
    kt = lax.dot_general(w2kt_ref[...], hk, (((1,), (1,)), ((), ())), preferred_element_type=F32)
    kblk_ref[...] = jnp.zeros(kblk_ref.shape, BF16)
    for g in range(ATT_GROUP):
        kblk_ref[0, 0, g * HEAD_DIM:(g + 1) * HEAD_DIM, g * ng:(g + 1) * ng] = kt.astype(BF16)

    hv = hidden(gv_ref, pev_ref, w1v_ref)
    vc = jnp.dot(hv, w2v_ref[...], preferred_element_type=F32)
    vblk_ref[...] = jnp.zeros(vblk_ref.shape, BF16)
    for g in range(ATT_GROUP):
        vblk_ref[0, 0, g * ng:(g + 1) * ng, g * LANES:(g + 1) * LANES] = vc.astype(BF16)


def _compress(proj, bsz, seq, pek, pev, w1k, w1v, w2kt, w2v):
    ng = seq // CMP_STRIDE
    hpb = LANES // HEAD_DIM
    kcb = COL_KV // LANES
    vcb = kcb + KV_WIDTH // LANES
    const = lambda shape: pl.BlockSpec(shape, lambda b, h: (0,) * len(shape))
    return pl.pallas_call(
        _compress_kernel,
        grid=(bsz, KV_HEADS),
        in_specs=[pl.BlockSpec((seq, LANES), lambda b, h: (b, kcb + h // hpb)),
                  pl.BlockSpec((seq, LANES), lambda b, h: (b, vcb + h // hpb)),
                  const((CMP_BLOCK, LANES)), const((CMP_BLOCK, LANES)),
                  const((CMP_BLOCK, LANES, CMP_HIDDEN)), const((CMP_BLOCK, LANES, CMP_HIDDEN)),
                  const((HEAD_DIM, CMP_HIDDEN)), const((CMP_HIDDEN, LANES))],
        out_specs=[pl.BlockSpec((1, 1, KV_WIDTH, ATT_GROUP * ng), lambda b, h: (b, h, 0, 0)),
                   pl.BlockSpec((1, 1, ATT_GROUP * ng, ATT_GROUP * LANES), lambda b, h: (b, h, 0, 0))],
        out_shape=[jax.ShapeDtypeStruct((bsz, KV_HEADS, KV_WIDTH, ATT_GROUP * ng), BF16),
                   jax.ShapeDtypeStruct((bsz, KV_HEADS, ATT_GROUP * ng, ATT_GROUP * LANES), BF16)],
        scratch_shapes=[pltpu.VMEM((ng + 8, CMP_HIDDEN), F32)],
        compiler_params=_cparams(("parallel", "parallel")),
        name="compress",
    )(proj, proj, pek, pev, w1k, w1v, w2kt, w2v)


TQC = 2 * TQ
SUBLANES = 8


CMP_HEADS_PER_STEP = 4


def _nsacmp_kernel(q_ref, kblk_ref, vblk_ref, cos_ref, s1_ref, s2_ref, ovt_ref, selq_ref, placen_ref, eye_ref,
                   qaug_ref, ocmp_ref):
    for hh in range(CMP_HEADS_PER_STEP):
        lanes = slice(hh * KV_WIDTH, (hh + 1) * KV_WIDTH)
        _nsacmp_head(q_ref.at[:, lanes], kblk_ref.at[0, hh], vblk_ref.at[0, hh], cos_ref, s1_ref, s2_ref,
                     ovt_ref, selq_ref, placen_ref, eye_ref, qaug_ref.at[0, hh], ocmp_ref.at[:, lanes])


def _nsacmp_head(q_ref, kblk_ref, vblk_ref, cos_ref, s1_ref, s2_ref, ovt_ref, selq_ref, placen_ref, eye_ref,
                 qaug_ref, ocmp_ref):
    ncp = kblk_ref.shape[1] // ATT_GROUP
    nsel = ovt_ref.shape[0]
    nt = (((1,), (1,)), ((), ()))
    s0 = pl.program_id(2) * TQC
    q = q_ref[...]
    s_all = jnp.dot((q * SCALE).astype(BF16), kblk_ref[...], preferred_element_type=F32)
    tq_pos = s0 + lax.broadcasted_iota(jnp.int32, (TQC, ncp), 0)
    cmp_end = lax.broadcasted_iota(jnp.int32, (TQC, ncp), 1) * CMP_STRIDE + (CMP_BLOCK - 1)
    mask = cmp_end <= tq_pos
    any_visible = (s0 + lax.broadcasted_iota(jnp.int32, (TQC, 1), 0) >= CMP_BLOCK - 1).astype(F32)
    ps = []
    for g in range(ATT_GROUP):
        s = jnp.where(mask, s_all[:, g * ncp:(g + 1) * ncp], NEG_INF)
        m = jnp.max(s, axis=-1, keepdims=True)
        e = jnp.exp(s - m)
        ps.append(e * (any_visible / jnp.sum(e, axis=-1, keepdims=True)))
    p_all = jnp.concatenate(ps, axis=1)

    def stack_rows(wide):
        return jnp.concatenate([wide[u * TQ:(u + 1) * TQ, g * LANES:(g + 1) * LANES]
                                for u in range(TQC // TQ) for g in range(ATT_GROUP)], axis=0)

    ow = jnp.dot(p_all.astype(BF16), vblk_ref[...], preferred_element_type=F32)
    ocmp_ref[...] = _merge_heads([ow[:, g * LANES:(g + 1) * LANES] for g in range(ATT_GROUP)])

    psum = (ps[0] + ps[1]) + (ps[2] + ps[3])
    p_hi = psum.astype(BF16)
    p_lo = (psum - p_hi.astype(F32)).astype(BF16)
    ovt = ovt_ref[...]
    imp = (lax.dot_general(ovt, p_hi, nt, preferred_element_type=F32)
           + lax.dot_general(ovt, p_lo, nt, preferred_element_type=F32))
    j = lax.broadcasted_iota(jnp.int32, (nsel, TQC), 0)
    cur = (s0 + lax.broadcasted_iota(jnp.int32, (nsel, TQC), 1)) // SEL_BLOCK
    imp = jnp.where((j == 0) | (j == cur) | (j == cur - 1), FORCE_SCORE, imp)
    imp = jnp.where(j <= cur, imp, -1.0)
    nblk = nsel // SUBLANES
    blocks = [imp[k * SUBLANES:(k + 1) * SUBLANES, :] for k in range(nblk)]
    ranks = [jnp.zeros((SUBLANES, TQC), F32) for _ in range(nblk)]
    sub = lax.broadcasted_iota(jnp.int32, (SUBLANES, TQC), 0)
    for jp in range(nsel):
        r = jnp.broadcast_to(imp[jp:jp + 1, :], (SUBLANES, TQC))
        for k in range(nblk):
            if k > jp // SUBLANES:
                inc = jnp.where(r >= blocks[k], 1.0, 0.0)
            elif k < jp // SUBLANES:
                inc = jnp.where(r > blocks[k], 1.0, 0.0)
            else:
                tie = jnp.where(sub > jp % SUBLANES, 1.0, 0.0)
                inc = jnp.where(r > blocks[k], 1.0, jnp.where(r == blocks[k], tie, 0.0))
            ranks[k] = ranks[k] + inc
    rank = jnp.concatenate(ranks, axis=0)
    notsel_t = jnp.where((rank < float(N_SELECT)) & (imp >= 0.0), 0.0, 1.0).astype(BF16)
    notsel = lax.dot_general(eye_ref[...], notsel_t, nt, preferred_element_type=F32)

    q_rot = (_rope(q, cos_ref[...], s1_ref[...], s2_ref[...]) * SCALE).astype(BF16)
    qw = (jnp.dot(q_rot, selq_ref[...], preferred_element_type=F32)
          + jnp.dot(notsel.astype(BF16), placen_ref[...], preferred_element_type=F32))
    qaug_ref[...] = stack_rows(qw).astype(BF16)


def _selection_overlap_t(n_cmp_pad, n_cmp, n_sel):
    cs = np.arange(n_cmp)[:, None] * CMP_STRIDE
    ce = cs + CMP_BLOCK
    ss = np.arange(n_sel)[None, :] * SEL_BLOCK
    se = ss + SEL_BLOCK
    ov = np.clip(np.minimum(ce, se) - np.maximum(cs, ss), 0, None) / CMP_BLOCK
    full = np.zeros((n_cmp_pad, n_sel), np.float32)
    full[:n_cmp] = ov
    return full.T.copy()


def _nsa_cmp(proj, kblk, vblk, bsz, seq, cos_t, s1_t, s2_t):
    nqc = seq // TQC
    ncp = kblk.shape[3] // ATT_GROUP
    n_cmp = (seq - CMP_BLOCK) // CMP_STRIDE + 1
    nsel = seq // SEL_BLOCK
    ovt = jnp.asarray(_selection_overlap_t(ncp, n_cmp, nsel), BF16)
    selq = jnp.asarray(_head_place_matrix(), BF16)
    placen = np.zeros((nsel, ATT_GROUP * LANES), np.float32)
    for g in range(ATT_GROUP):
        for jb in range(nsel):
            placen[jb, g * LANES + HEAD_DIM + jb] = 1.0
    placen = jnp.asarray(placen, BF16)
    eye = jnp.asarray(np.eye(TQC, dtype=np.float32), BF16)
    const = lambda shape: pl.BlockSpec(shape, lambda b, h, t: (0,) * len(shape))
    tab = pl.BlockSpec((TQC, KV_WIDTH), lambda b, h, t: (t, 0))
    hps = CMP_HEADS_PER_STEP
    qb = COL_Q // (hps * KV_WIDTH)
    rows = (TQC // TQ) * ROWS
    stacked = pl.BlockSpec((1, hps, rows, LANES), lambda b, h, t: (b, h, t, 0))
    return pl.pallas_call(
        _nsacmp_kernel,
        grid=(bsz, KV_HEADS // hps, nqc),
        in_specs=[
            pl.BlockSpec((TQC, hps * KV_WIDTH), lambda b, h, t: (b * nqc + t, qb + h)),
            pl.BlockSpec((1, hps, KV_WIDTH, ATT_GROUP * ncp), lambda b, h, t: (b, h, 0, 0)),
            pl.BlockSpec((1, hps, ATT_GROUP * ncp, ATT_GROUP * LANES), lambda b, h, t: (b, h, 0, 0)),
            tab, tab, tab,
            const((nsel, ncp)), const((KV_WIDTH, ATT_GROUP * LANES)),
            const((nsel, ATT_GROUP * LANES)), const((TQC, TQC)),
        ],
        out_specs=[stacked, pl.BlockSpec((TQC, hps * KV_WIDTH), lambda b, h, t: (b * nqc + t, h))],
        out_shape=[jax.ShapeDtypeStruct((bsz, KV_HEADS, nqc * rows, LANES), BF16),
                   jax.ShapeDtypeStruct((bsz * seq, ATT_WIDTH), F32)],
        compiler_params=_cparams(("parallel", "parallel", "parallel")),
        name="nsa_cmp",
    )(proj, kblk, vblk, cos_t, s1_t, s2_t, ovt, selq, placen, eye)


ATT_HEADS_PER_STEP = 2


def _nsaattn_kernel(qaug_ref, ocmp_ref, ks_ref, vs_ref, kw_ref, vw_ref, gate_ref, gexp_ref, db_ref, wb_ref, o_ref,
                    m_ref, acc_ref):
    qt = pl.program_id(2)
    s0 = qt * TQ
    nt = (((1,), (1,)), ((), ()))
    heads = range(ATT_HEADS_PER_STEP)
    qs = [qaug_ref[0, hh] for hh in heads]
    klanes = [slice(hh * LANES, (hh + 1) * LANES) for hh in heads]
    olanes = [slice(hh * KV_WIDTH, (hh + 1) * KV_WIDTH) for hh in heads]

    def per_head(bias):
        return jnp.concatenate([bias] * ATT_GROUP, axis=0)

    m_ref[...] = jnp.full(m_ref.shape, NEG_INF, F32)
    acc_ref[...] = jnp.zeros(acc_ref.shape, F32)

    def tile(first_key, width, causal=False):
        start = pl.multiple_of(first_key, width)
        for hh in heads:
            k = ks_ref[pl.ds(start, width), klanes[hh]]
            v = vs_ref[pl.ds(start, width), klanes[hh]]
            s = lax.dot_general(qs[hh], k, nt, preferred_element_type=F32)
            if causal:
                s = s + per_head(db_ref[0][:, :width])
            m_prev = m_ref[hh]
            m_next = jnp.maximum(m_prev, jnp.max(s, axis=-1, keepdims=True))
            p = jnp.exp(s - jnp.concatenate([m_next] * (width // LANES), axis=1))
            acc_ref[hh] = acc_ref[hh] * jnp.exp(m_prev - m_next) + jnp.dot(
                p.astype(BF16), v, preferred_element_type=F32)
            m_ref[hh] = m_next

    nfull = (qt * TQ) // TK

    def body(i, carry):
        tile(i * (2 * TK), 2 * TK)
        return carry

    lax.fori_loop(0, nfull // 2, body, 0)

    @pl.when(nfull % 2 == 1)
    def _():
        tile((nfull - 1) * TK, TK)

    @pl.when(s0 == nfull * TK)
    def _():
        tile(s0, TQ, causal=True)

    @pl.when(s0 != nfull * TK)
    def _():
        tile(nfull * TK, TK, causal=True)

    def normalised(x):
        parts = [x[g * TQ:(g + 1) * TQ] for g in range(ATT_GROUP)]
        return _merge_heads(parts) / _merge_heads(parts, upper=True)

    wk = WINDOW + TQ
    wstart = pl.multiple_of(jnp.maximum(s0 - WINDOW, 0), TQ)
    sig_terms = _split_bf16(jax.nn.sigmoid(gate_ref[...]), 2)
    for hh in heads:
        kwin = kw_ref[pl.ds(wstart, wk), klanes[hh]]
        vwin = vw_ref[pl.ds(wstart, wk), klanes[hh]]
        sw = lax.dot_general(qs[hh], kwin, nt, preferred_element_type=F32) + per_head(wb_ref[0])
        pw = jnp.exp(sw - jnp.max(sw, axis=-1, keepdims=True))
        ow = jnp.dot(pw.astype(BF16), vwin, preferred_element_type=F32)

        gexp = gexp_ref[hh]
        gmap = sum(jnp.dot(t, gexp, preferred_element_type=F32) for t in sig_terms)
        y = (gmap[:, 0:KV_WIDTH] * ocmp_ref[:, olanes[hh]]
             + gmap[:, KV_WIDTH:2 * KV_WIDTH] * normalised(acc_ref[hh])
             + gmap[:, 2 * KV_WIDTH:3 * KV_WIDTH] * normalised(ow))
        o_ref[:, olanes[hh]] = y.astype(o_ref.dtype)


def _nsa_attn(qaug, ocmp, ksa, vsa, kwa, vwa, proj, bsz, seq):
    nqt = seq // TQ
    gexp = np.zeros((KV_HEADS, LANES, N_BRANCH * KV_WIDTH), np.float32)
    for h in range(KV_HEADS):
        for g in range(ATT_GROUP):
            for br in range(N_BRANCH):
                src = (h * ATT_GROUP + g) * N_BRANCH + br
                gexp[h, src, br * KV_WIDTH + g * HEAD_DIM:br * KV_WIDTH + (g + 1) * HEAD_DIM] = 1.0
    gexp = jnp.asarray(gexp, BF16)
    r = np.arange(TQ)[:, None]
    diag_cases = TK // TQ
    db = np.stack([np.where(np.arange(TK)[None, :] <= c * TQ + r, 0.0, NEG_INF) for c in range(diag_cases)])
    win_cases = WINDOW // TQ + 1
    wk = WINDOW + TQ
    dist = lambda c: c * TQ + r - np.arange(wk)[None, :]
    wb = np.stack([np.where((dist(c) >= 0) & (dist(c) < WINDOW), 0.0, NEG_INF) for c in range(win_cases)])
    db = jnp.asarray(db, F32)
    wb = jnp.asarray(wb, F32)
    hps = ATT_HEADS_PER_STEP
    stacked = pl.BlockSpec((1, hps, ROWS, LANES), lambda b, h, t: (b, h, t, 0))
    kv = pl.BlockSpec((seq, hps * LANES), lambda b, h, t: (b, h))
    merged = pl.BlockSpec((TQ, hps * KV_WIDTH), lambda b, h, t: (b * nqt + t, h))
    return pl.pallas_call(
        _nsaattn_kernel,
        grid=(bsz, KV_HEADS // hps, nqt),
        in_specs=[stacked, merged, kv, kv, kv, kv,
                  pl.BlockSpec((TQ, LANES), lambda b, h, t: (b * nqt + t, COL_GATE // LANES)),
                  pl.BlockSpec((hps, LANES, N_BRANCH * KV_WIDTH), lambda b, h, t: (h, 0, 0)),
                  pl.BlockSpec((1, TQ, TK), lambda b, h, t: (t % diag_cases, 0, 0)),
                  pl.BlockSpec((1, TQ, wk), lambda b, h, t: (jnp.minimum(t, win_cases - 1), 0, 0))],
        out_specs=merged,
        out_shape=jax.ShapeDtypeStruct((bsz * seq, ATT_WIDTH), BF16),
        scratch_shapes=[pltpu.VMEM((hps, ROWS, LANES), F32), pltpu.VMEM((hps, ROWS, LANES), F32)],
        compiler_params=_cparams(("parallel", "parallel", "arbitrary")),
        name="nsa_attn",
    )(qaug, ocmp, ksa, vsa, kwa, vwa, proj, gexp, db, wb)


def _outproj_kernel(x_ref, ys_ref, ya_ref, w1_ref, w2_ref, o_ref):
    o_ref[...] = (x_ref[...] + jnp.dot(ys_ref[...], w1_ref[...], preferred_element_type=F32)
                  + jnp.dot(ya_ref[...], w2_ref[...], preferred_element_type=F32))


def _out_proj(x2, ys, ya, wo_b):
    t = x2.shape[0]
    tm = 512
    return pl.pallas_call(
        _outproj_kernel,
        grid=(t // tm,),
        in_specs=[
            pl.BlockSpec((tm, D_MODEL), lambda i: (i, 0)),
            pl.BlockSpec((tm, SSD_WIDTH), lambda i: (i, 0)),
            pl.BlockSpec((tm, ATT_WIDTH), lambda i: (i, 0)),
            pl.BlockSpec((SSD_WIDTH, D_MODEL), lambda i: (0, 0)),
            pl.BlockSpec((ATT_WIDTH, D_MODEL), lambda i: (1, 0)),
        ],
        out_specs=pl.BlockSpec((tm, D_MODEL), lambda i: (i, 0)),
        out_shape=jax.ShapeDtypeStruct((t, D_MODEL), F32),
        compiler_params=_cparams(("parallel",)),
        name="out_proj",
    )(x2, ys, ya, wo_b, wo_b)


def _ffn_kernel(h_ref, nw_ref, fw_ref, wg_ref, wu_ref, wd_ref, o_ref, v_ref):
    j = pl.program_id(1)

    @pl.when(j == 0)
    def _():
        h = h_ref[...]
        ms = jnp.mean(h * h, axis=-1, keepdims=True)
        v_ref[...] = ((h * lax.rsqrt(ms + NORM_EPS)) * nw_ref[...]).astype(BF16)
        o_ref[...] = h

    v = v_ref[...]
    gate = jnp.dot(v, wg_ref[...].astype(BF16), preferred_element_type=F32)
    up = jnp.dot(v, wu_ref[...].astype(BF16), preferred_element_type=F32)
    o_ref[...] += jnp.dot((_silu(gate) * up).astype(BF16), wd_ref[...].astype(BF16), preferred_element_type=F32)

    @pl.when(j == pl.num_programs(1) - 1)
    def _():
        h2 = o_ref[...]
        ms = jnp.mean(h2 * h2, axis=-1, keepdims=True)
        o_ref[...] = (h2 * lax.rsqrt(ms + NORM_EPS)) * fw_ref[...]


def _ffn(h1, ffn_nw, final_w, wg_b, wu_b, wd_b):
    t = h1.shape[0]
    tm, tf = 1024, 256
    return pl.pallas_call(
        _ffn_kernel,
        grid=(t // tm, D_FF // tf),
        in_specs=[
            pl.BlockSpec((tm, D_MODEL), lambda i, j: (i, 0)),
            pl.BlockSpec((1, D_MODEL), lambda i, j: (0, 0)),
            pl.BlockSpec((1, D_MODEL), lambda i, j: (0, 0)),
            pl.BlockSpec((D_MODEL, tf), lambda i, j: (0, j)),
            pl.BlockSpec((D_MODEL, tf), lambda i, j: (0, j)),
            pl.BlockSpec((tf, D_MODEL), lambda i, j: (j, 0)),
        ],
        out_specs=pl.BlockSpec((tm, D_MODEL), lambda i, j: (i, 0)),
        out_shape=jax.ShapeDtypeStruct((t, D_MODEL), F32),
        scratch_shapes=[pltpu.VMEM((tm, D_MODEL), BF16)],
        compiler_params=_cparams(("parallel", "arbitrary")),
        name="ffn",
    )(h1, ffn_nw, final_w, wg_b, wu_b, wd_b)


def _rope_tables(seq):
    f32 = np.float32
    inv = (f32(1.0) / (f32(ROPE_THETA) ** (np.arange(0, ROPE_DIM, 2, dtype=f32) / f32(ROPE_DIM)))).astype(f32)
    ang = (np.arange(seq, dtype=f32)[:, None] * inv[None, :]).astype(f32)
    cos, sin = np.cos(ang).astype(f32), np.sin(ang).astype(f32)
    half = ROPE_DIM // 2
    rest_one = np.ones((seq, HEAD_DIM - ROPE_DIM), f32)
    rest_zero = np.zeros((seq, HEAD_DIM - ROPE_DIM), f32)
    zero_h = np.zeros((seq, half), f32)
    cos_h = np.concatenate([cos, cos, rest_one], axis=1)
    s1_h = np.concatenate([-sin, zero_h, rest_zero], axis=1)
    s2_h = np.concatenate([zero_h, sin, rest_zero], axis=1)
    tile = lambda a: jnp.asarray(np.tile(a, (1, KV_HEADS)))
    return tile(cos_h), tile(s1_h), tile(s2_h)


def _pad_lanes(a, width):
    return jnp.pad(a, ((0, 0), (0, width - a.shape[1])))


def _layer(h2d, bsz, seq, p):
    (attn_norm_w, w_in, conv_w, conv_b, dt_bias, a_log, d_skip, ssd_norm_w, cmp_w1_k, cmp_w2_k, cmp_w1_v,
     cmp_w2_v, cmp_pe_k, cmp_pe_v, w_out, ffn_norm_w, w_gate, w_up, w_down) = p
    o_xbc, o_dt, o_q, o_kv, o_gate = 1024, 2560, 2576, 3600, 5136
    w_perm = jnp.concatenate([
        w_in[:, o_xbc:o_dt], w_in[:, o_kv:o_gate], w_in[:, :o_xbc], w_in[:, o_q:o_kv],
        _pad_lanes(w_in[:, o_dt:o_q], LANES), _pad_lanes(w_in[:, o_gate:], LANES)], axis=1).astype(BF16)
    proj = _in_proj(h2d, attn_norm_w[None, :], w_perm)

    y_ssd = _ssd(proj, bsz, seq, conv_w, conv_b[None, :], _pad_lanes(dt_bias[None, :], LANES),
                 _pad_lanes(a_log[None, :], LANES), jnp.repeat(d_skip, SSD_HEAD_DIM)[None, :],
                 ssd_norm_w[None, :])

    cos_t, s1_t, s2_t = _rope_tables(seq)
    ksa, vsa, kwa, vwa = _kv_prep(proj, seq, cos_t, s1_t, s2_t)

    hpb = LANES // HEAD_DIM
    pe_rep = lambda pe: jnp.tile(pe, (1, hpb))
    w1_rep = lambda w1: jnp.tile(w1.astype(BF16).reshape(CMP_BLOCK, HEAD_DIM, CMP_HIDDEN), (1, hpb, 1))
    kblk, vblk = _compress(proj, bsz, seq, pe_rep(cmp_pe_k), pe_rep(cmp_pe_v),
                           w1_rep(cmp_w1_k), w1_rep(cmp_w1_v), cmp_w2_k.T.astype(BF16),
                           _pad_lanes(cmp_w2_v, LANES).astype(BF16))
    qaug, ocmp = _nsa_cmp(proj, kblk, vblk, bsz, seq, cos_t, s1_t, s2_t)
    y_att = _nsa_attn(qaug, ocmp, ksa, vsa, kwa, vwa, proj, bsz, seq)

    h1 = _out_proj(h2d, y_ssd, y_att, w_out.astype(BF16))
    return h1, (ffn_norm_w, w_gate, w_up, w_down)


def kernel(x, attn_norm_w, w_in, conv_w, conv_b, dt_bias, a_log, d_skip, ssd_norm_w, cmp_w1_k, cmp_w2_k,
           cmp_w1_v, cmp_w2_v, cmp_pe_k, cmp_pe_v, w_out, ffn_norm_w, w_gate, w_up, w_down, final_norm_w):
    bsz, seq, _ = x.shape
    depth = w_in.shape[0]
    assert depth == 1, "the final rmsnorm is fused into the last layer's ffn kernel"
    h = x.reshape(bsz * seq, D_MODEL)
    l = 0
    params = (attn_norm_w[l], w_in[l], conv_w[l], conv_b[l], dt_bias[l], a_log[l], d_skip[l], ssd_norm_w[l],
              cmp_w1_k[l], cmp_w2_k[l], cmp_w1_v[l], cmp_w2_v[l], cmp_pe_k[l], cmp_pe_v[l], w_out[l],
              ffn_norm_w[l], w_gate[l], w_up[l], w_down[l])
    h1, (fnw, wg_b, wu_b, wd_b) = _layer(h, bsz, seq, params)
    out = _ffn(h1, fnw[None, :], final_norm_w[None, :], wg_b, wu_b, wd_b)
    return out.reshape(bsz, seq, D_MODEL)
```
